```python
import math
import jax, jax.numpy as jnp
from jax import lax
import numpy as np

D_MODEL = 1024
BATCH = 16
SEQ = 2048
DEPTH = 4

CTX_LEN = 256
GRID_W = 64
N_EVEN = (DEPTH + 1) // 2
N_ODD = DEPTH // 2

SSD_HEADS = 16
SSD_HEAD_DIM = 64
SSD_INNER = SSD_HEADS * SSD_HEAD_DIM
SSD_GROUPS = 2
SSD_HPG = SSD_HEADS // SSD_GROUPS
SSD_STATE = 128
SSD_CHUNK = 128
SSD_CONV = 5
SSD_CONV_CH = SSD_INNER + 2 * SSD_GROUPS * SSD_STATE
CONV_CH = 1024
CONV_WIDTH = 31
MLP_CH = 1024
MLP_GROUPS = 8
MLP_GROUP_CH = MLP_CH // MLP_GROUPS
MLP_CHUNK = 128
ATT_HEADS = 16
ATT_KV_HEADS = 4
ATT_REP = ATT_HEADS // ATT_KV_HEADS
ATT_HEAD_DIM = 64
ATT_WINDOW = 128
ATT_BLOCK = 128
ATT_SCALE = ATT_HEAD_DIM ** -0.5
ROPE_BASE = 10000.0

EVEN_SIZES = (SSD_INNER, SSD_CONV_CH, 2 * SSD_HEADS, 2 * CONV_CH, CONV_CH)
EVEN_IN = sum(EVEN_SIZES)
EVEN_MIX = SSD_INNER + CONV_CH
ODD_SIZES = (MLP_CH, MLP_CH, MLP_CH, ATT_HEADS * ATT_HEAD_DIM, ATT_KV_HEADS * ATT_HEAD_DIM,
             ATT_KV_HEADS * ATT_HEAD_DIM, ATT_HEADS * ATT_HEAD_DIM)
ODD_IN = sum(ODD_SIZES)
ODD_MIX = MLP_CH + ATT_HEADS * ATT_HEAD_DIM

DEEPNORM_ALPHA = (2 * DEPTH) ** 0.25
DEEPNORM_BETA = (8 * DEPTH) ** -0.25
LN_EPS = 1e-5

kernel_name = 'hybrid_ssd_conformer_gmlp_swa_dit_trunk'

F32 = jnp.float32


def _split(t, sizes):
    offs = np.cumsum(sizes)[:-1].tolist()
    return jnp.split(t, offs, axis=-1)


def layer_norm(t, g, b):
    tf = t.astype(F32)
    mu = jnp.mean(tf, -1, keepdims=True)
    var = jnp.mean(jnp.square(tf - mu), -1, keepdims=True)
    return ((tf - mu) * lax.rsqrt(var + LN_EPS)).astype(t.dtype) * g + b


def rms_norm(t, w):
    tf = t.astype(F32)
    return (tf * lax.rsqrt(jnp.mean(tf * tf, -1, keepdims=True) + LN_EPS)).astype(t.dtype) * w


def dwconv(t, w, b):
    pad = w.shape[0] // 2
    y = lax.conv_general_dilated(t, w[:, None, :].astype(t.dtype), window_strides=(1,), padding=[(pad, pad)],
                                 dimension_numbers=('NWC', 'WIO', 'NWC'), feature_group_count=t.shape[-1])
    return y + b


def modulation(cond, w, b):
    m = jax.nn.silu(cond) @ w + b
    sh, sc, g = jnp.split(m, 3, axis=-1)
    return sh[:, None, :], sc[:, None, :], g[:, None, :]


def axial_rope(rows):
    t = jnp.arange(rows * GRID_W)
    row = (t // GRID_W).astype(F32)
    col = (t % GRID_W).astype(F32)
    n_freq = ATT_HEAD_DIM // 4
    inv = ROPE_BASE ** (-jnp.arange(n_freq, dtype=F32) / n_freq)
    ang = jnp.concatenate([row[:, None] * inv, col[:, None] * inv], -1)
    return jnp.cos(ang), jnp.sin(ang)


def apply_rope(t, cos, sin):
    tf = t.astype(F32)
    h = ATT_HEAD_DIM // 2
    t1, t2 = tf[..., :h], tf[..., h:]
    c = cos[None, :, None, :]
    s = sin[None, :, None, :]
    return jnp.concatenate([t1 * c - t2 * s, t2 * c + t1 * s], -1).astype(t.dtype)


def ssd_chunked(x, dt, a, bm, cm, h0):
    bsz, L, G, R, P = x.shape
    N = bm.shape[-1]
    Q = SSD_CHUNK
    T = L // Q
    xdt = (x * dt[..., None]).reshape(bsz, T, Q, G, R, P)
    bc = bm.reshape(bsz, T, Q, G, N)
    cc = cm.reshape(bsz, T, Q, G, N)
    a_cum = jnp.cumsum((dt * a).reshape(bsz, T, Q, G, R), axis=2)
    seg = a_cum[:, :, :, None] - a_cum[:, :, None, :]
    lower = jnp.tril(jnp.ones((Q, Q), bool))[:, :, None, None]
    decay = jnp.exp(jnp.where(lower, seg, -jnp.inf))
    cb = jnp.einsum('btlgn,btsgn->btlsg', cc, bc).astype(F32)
    y_diag = jnp.einsum('btlsgr,btsgrp->btlgrp', cb[..., None] * decay, xdt)
    decay_to_end = jnp.exp(a_cum[:, :, -1:] - a_cum)
    states = jnp.einsum('btsgn,btsgr,btsgrp->btgrpn', bc, decay_to_end, xdt).astype(F32)
    chunk_decay = jnp.exp(a_cum[:, :, -1])

    def step(h, inp):
        s, d = inp
        return h * d[..., None, None] + s, h

    h_final, h_in = lax.scan(step, h0, (jnp.moveaxis(states, 1, 0), jnp.moveaxis(chunk_decay, 1, 0)))
    h_in = jnp.moveaxis(h_in, 0, 1)
    y_off = jnp.einsum('btlgn,btgrpn,btlgr->btlgrp', cc, h_in, jnp.exp(a_cum))
    return (y_diag + y_off).reshape(bsz, L, G, R, P).astype(x.dtype), h_final


def ssd_branch(z, xbc, dt_raw, h0_f, h0_b, conv_w, conv_b, dt_bias, a, d_skip, norm_w):
    bsz, L, _ = xbc.shape
    xbc = jax.nn.silu(dwconv(xbc, conv_w, conv_b))
    xs, bm, cm = _split(xbc, (SSD_INNER, SSD_GROUPS * SSD_STATE, SSD_GROUPS * SSD_STATE))
    xs = xs.reshape(bsz, L, SSD_GROUPS, SSD_HPG, SSD_HEAD_DIM)
    bm = bm.reshape(bsz, L, SSD_GROUPS, SSD_STATE)
    cm = cm.reshape(bsz, L, SSD_GROUPS, SSD_STATE)
    dt = jax.nn.softplus((dt_raw + dt_bias).astype(F32))
    dt_f = dt[..., :SSD_HEADS].reshape(bsz, L, SSD_GROUPS, SSD_HPG)
    dt_b = dt[..., SSD_HEADS:].reshape(bsz, L, SSD_GROUPS, SSD_HPG)
    a_f = a[0].reshape(SSD_GROUPS, SSD_HPG)
    a_b = a[1].reshape(SSD_GROUPS, SSD_HPG)
    y_f, h_f = ssd_chunked(xs, dt_f, a_f, bm, cm, h0_f)
    fl = lambda t: jnp.flip(t, axis=1)
    y_b, h_b = ssd_chunked(fl(xs), fl(dt_b), a_b, fl(bm), fl(cm), h0_b)
    y = y_f + fl(y_b) + xs * d_skip.reshape(SSD_GROUPS, SSD_HPG, 1)
    y = y.reshape(bsz, L, SSD_INNER)
    return rms_norm(y * jax.nn.silu(z), norm_w), h_f, h_b


def conformer_conv(glu_in, w, b, ln_g, ln_b):
    val, gt = jnp.split(glu_in, 2, axis=-1)
    h = dwconv(val * jax.nn.sigmoid(gt), w, b)
    return jax.nn.silu(layer_norm(h, ln_g, ln_b))


def even_mixer(h_x, h_c, w_in, conv_w, conv_b, dt_bias, a_log, d_skip, ssd_norm,
               cv_w, cv_b, cv_ln_g, cv_ln_b, w_out, with_ctx):
    a = -jnp.exp(a_log.astype(F32))
    z_c, xbc_c, dt_c, glu_c, gate_c = _split(h_c @ w_in, EVEN_SIZES)
    z_x, xbc_x, dt_x, glu_x, gate_x = _split(h_x @ w_in, EVEN_SIZES)
    zero = jnp.zeros((h_c.shape[0], SSD_GROUPS, SSD_HPG, SSD_HEAD_DIM, SSD_STATE), F32)
    ya_c, hf_c, hb_c = ssd_branch(z_c, xbc_c, dt_c, zero, zero, conv_w, conv_b, dt_bias, a, d_skip, ssd_norm)
    ya_x, _, _ = ssd_branch(z_x, xbc_x, dt_x, hf_c, hb_c, conv_w, conv_b, dt_bias, a, d_skip, ssd_norm)

    def merge(ya, glu, gate):
        yb = conformer_conv(glu, cv_w, cv_b, cv_ln_g, cv_ln_b) * jax.nn.silu(gate)
        return jnp.concatenate([ya, yb], axis=-1) @ w_out

    y_x = merge(ya_x, glu_x, gate_x)
    y_c = merge(ya_c, glu_c, gate_c) if with_ctx else None
    return y_x, y_c


def chunk_mlp(u, v, ln_g, ln_b, ws, bs):
    u = jax.nn.gelu(u)
    v = layer_norm(jax.nn.gelu(v), ln_g, ln_b)
    bsz, L, _ = v.shape
    v = v.reshape(bsz, L // MLP_CHUNK, MLP_CHUNK, MLP_GROUPS, MLP_GROUP_CH)
    mixed = jnp.einsum('gqp,bcpgd->bcqgd', ws, v) + bs.T[None, None, :, :, None]
    return u * mixed.reshape(bsz, L, MLP_CH)


def window_attention(q, k, v, k_ctx, v_ctx, sink):
    bsz, L = q.shape[:2]
    W = ATT_BLOCK
    nb = L // W

    def band_blocks(t):
        tp = jnp.pad(t, ((0, 0), (W, W), (0, 0), (0, 0))).reshape(bsz, nb + 2, W, ATT_KV_HEADS, ATT_HEAD_DIM)
        return jnp.moveaxis(jnp.concatenate([tp[:, :-2], tp[:, 1:-1], tp[:, 2:]], axis=2), 1, 0)

    kb = band_blocks(k)
    vb = band_blocks(v)
    qb = jnp.moveaxis(q.reshape(bsz, nb, W, ATT_KV_HEADS, ATT_REP, ATT_HEAD_DIM), 1, 0)
    rel = jnp.arange(3 * W) - W
    band = jnp.abs(jnp.arange(W)[:, None] - rel[None, :]) <= ATT_WINDOW
    key_pos = jnp.arange(nb)[:, None] * W + rel[None, :]
    mask = band[None] & ((key_pos >= 0) & (key_pos < L))[:, None, :]
    sink_f = sink.astype(F32)[None, :, :, None, None]

    def block(args):
        qblk, kblk, vblk, mblk = args
        s_lat = jnp.einsum('bqkrd,bskd->bkrqs', qblk, kblk).astype(F32) * ATT_SCALE
        s_lat = jnp.where(mblk, s_lat, -jnp.inf)
        s_ctx = jnp.einsum('bqkrd,bskd->bkrqs', qblk, k_ctx).astype(F32) * ATT_SCALE
        s_snk = jnp.broadcast_to(sink_f, s_ctx.shape[:-1] + (1,))
        p = jax.nn.softmax(jnp.concatenate([s_lat, s_ctx, s_snk], axis=-1), axis=-1).astype(vblk.dtype)
        return (jnp.einsum('bkrqs,bskd->bqkrd', p[..., :3 * W], vblk)
                + jnp.einsum('bkrqs,bskd->bqkrd', p[..., 3 * W:-1], v_ctx))

    out = lax.map(block, (qb, kb, vb, mask))
    return jnp.moveaxis(out, 0, 1).reshape(bsz, L, ATT_HEADS * ATT_HEAD_DIM)


def context_attention(qc, kc, vc, sink):
    bsz, C = qc.shape[:2]
    s = jnp.einsum('bqkrd,bskd->bkrqs', qc, kc).astype(F32) * ATT_SCALE
    s_snk = jnp.broadcast_to(sink.astype(F32)[None, :, :, None, None], s.shape[:-1] + (1,))
    p = jax.nn.softmax(jnp.concatenate([s, s_snk], axis=-1), axis=-1)[..., :-1].astype(vc.dtype)
    return jnp.einsum('bkrqs,bskd->bqkrd', p, vc).reshape(bsz, C, ATT_HEADS * ATT_HEAD_DIM)


def odd_mixer(h_x, h_c, w_in, mlp_ln_g, mlp_ln_b, ws, bs, sink, w_out, cos, sin, with_ctx):
    u_x, v_x, gc_x, q_x, k_x, va_x, gd_x = _split(h_x @ w_in, ODD_SIZES)
    u_c, v_c, gc_c, q_c, k_c, va_c, gd_c = _split(h_c @ w_in, ODD_SIZES)
    bsz, L = h_x.shape[:2]
    C = h_c.shape[1]
    heads = lambda t, n: t.reshape(t.shape[0], t.shape[1], n, ATT_HEAD_DIM)
    sink_g = sink.reshape(ATT_KV_HEADS, ATT_REP)
    k_ctx = heads(k_c, ATT_KV_HEADS)
    v_ctx = heads(va_c, ATT_KV_HEADS)
    q = apply_rope(heads(q_x, ATT_HEADS), cos, sin).reshape(bsz, L, ATT_KV_HEADS, ATT_REP, ATT_HEAD_DIM)
    k = apply_rope(heads(k_x, ATT_KV_HEADS), cos, sin)
    yd_x = window_attention(q, k, heads(va_x, ATT_KV_HEADS), k_ctx, v_ctx, sink_g)
    yc_x = chunk_mlp(u_x, v_x, mlp_ln_g, mlp_ln_b, ws, bs)
    y_x = jnp.concatenate([yc_x * jax.nn.silu(gc_x), yd_x * jax.nn.silu(gd_x)], axis=-1) @ w_out
    if with_ctx:
        qc = heads(q_c, ATT_HEADS).reshape(bsz, C, ATT_KV_HEADS, ATT_REP, ATT_HEAD_DIM)
        yd_c = context_attention(qc, k_ctx, v_ctx, sink_g)
        yc_c = chunk_mlp(u_c, v_c, mlp_ln_g, mlp_ln_b, ws, bs)
        y_c = jnp.concatenate([yc_c * jax.nn.silu(gc_c), yd_c * jax.nn.silu(gd_c)], axis=-1) @ w_out
    else:
        y_c = None
    return y_x, y_c


def setup_inputs(seed: int = 0) -> dict:
    key = jax.random.key(seed)
    ks = iter(jax.random.split(key, 40))
    nrm = lambda shape, scale: jax.random.normal(next(ks), shape, F32) * scale
    E, O, D = N_EVEN, N_ODD, D_MODEL
    dt0 = jnp.exp(jax.random.uniform(next(ks), (E, 2 * SSD_HEADS), F32, math.log(1e-3), math.log(1e-1)))
    dt_bias = dt0 + jnp.log(-jnp.expm1(-dt0))
    a_log = jnp.log(jax.random.uniform(next(ks), (E, 2, SSD_HEADS), F32, 1.0, 16.0))
    return {
        'x': nrm((BATCH, SEQ, D), 1.0),
        'c': nrm((BATCH, D), 1.0),
        'ctx': nrm((BATCH, CTX_LEN, D), 1.0),
        'c_ctx': nrm((D,), 1.0),
        'mod_w': nrm((DEPTH, D, 3 * D), D ** -0.5),
        'mod_b': nrm((DEPTH, 3 * D), 0.02),
        'ln_g': 1.0 + nrm((DEPTH, D), 0.02),
        'ln_b': nrm((DEPTH, D), 0.02),
        'ev_w_in': nrm((E, D, EVEN_IN), D ** -0.5),
        'ev_ssd_conv_w': nrm((E, SSD_CONV, SSD_CONV_CH), SSD_CONV ** -0.5),
        'ev_ssd_conv_b': nrm((E, SSD_CONV_CH), 0.02),
        'ev_dt_bias': dt_bias,
        'ev_a_log': a_log,
        'ev_d_skip': 1.0 + nrm((E, SSD_HEADS), 0.02),
        'ev_ssd_norm': 1.0 + nrm((E, SSD_INNER), 0.02),
        'ev_cv_w': nrm((E, CONV_WIDTH, CONV_CH), CONV_WIDTH ** -0.5),
        'ev_cv_b': nrm((E, CONV_CH), 0.02),
        'ev_cv_ln_g': 1.0 + nrm((E, CONV_CH), 0.02),
        'ev_cv_ln_b': nrm((E, CONV_CH), 0.02),
        'ev_w_out': nrm((E, EVEN_MIX, D), EVEN_MIX ** -0.5 * DEEPNORM_BETA),
        'od_w_in': nrm((O, D, ODD_IN), D ** -0.5),
        'od_mlp_ln_g': 1.0 + nrm((O, MLP_CH), 0.02),
        'od_mlp_ln_b': nrm((O, MLP_CH), 0.02),
        'od_ws': nrm((O, MLP_GROUPS, MLP_CHUNK, MLP_CHUNK), MLP_CHUNK ** -0.5),
        'od_bs': 1.0 + nrm((O, MLP_GROUPS, MLP_CHUNK), 0.02),
        'od_sink': nrm((O, ATT_HEADS), 0.5),
        'od_w_out': nrm((O, ODD_MIX, D), ODD_MIX ** -0.5 * DEEPNORM_BETA),
    }


def reference(x, c, ctx, c_ctx, mod_w, mod_b, ln_g, ln_b,
              ev_w_in, ev_ssd_conv_w, ev_ssd_conv_b, ev_dt_bias, ev_a_log, ev_d_skip, ev_ssd_norm,
              ev_cv_w, ev_cv_b, ev_cv_ln_g, ev_cv_ln_b, ev_w_out,
              od_w_in, od_mlp_ln_g, od_mlp_ln_b, od_ws, od_bs, od_sink, od_w_out):
    ROWS = x.shape[1] // GRID_W
    cos, sin = axial_rope(ROWS)
    for layer in range(DEPTH):
        last = layer == DEPTH - 1
        i = layer // 2
        sh_x, sc_x, g_x = modulation(c, mod_w[layer], mod_b[layer])
        sh_c, sc_c, g_c = modulation(c_ctx[None, :], mod_w[layer], mod_b[layer])
        h_x = x * (1.0 + sc_x) + sh_x
        h_c = ctx * (1.0 + sc_c) + sh_c
        if layer % 2 == 0:
            y_x, y_c = even_mixer(h_x, h_c, ev_w_in[i], ev_ssd_conv_w[i], ev_ssd_conv_b[i], ev_dt_bias[i],
                                  ev_a_log[i], ev_d_skip[i], ev_ssd_norm[i], ev_cv_w[i], ev_cv_b[i],
                                  ev_cv_ln_g[i], ev_cv_ln_b[i], ev_w_out[i], not last)
        else:
            y_x, y_c = odd_mixer(h_x, h_c, od_w_in[i], od_mlp_ln_g[i], od_mlp_ln_b[i], od_ws[i], od_bs[i],
                                 od_sink[i], od_w_out[i], cos, sin, not last)
        x = layer_norm(DEEPNORM_ALPHA * x + g_x * y_x, ln_g[layer], ln_b[layer])
        if not last:
            ctx = layer_norm(DEEPNORM_ALPHA * ctx + g_c * y_c, ln_g[layer], ln_b[layer])
    return x
```

```python
import functools
import math

import jax
import jax.numpy as jnp
from jax import lax
from jax.experimental import pallas as pl
from jax.experimental.pallas import tpu as pltpu

F32 = jnp.float32
BF16 = jnp.bfloat16

D_MODEL = 1024
DEPTH = 4
GRID_W = 64

SSD_HEADS = 16
SSD_HEAD_DIM = 64
SSD_INNER = SSD_HEADS * SSD_HEAD_DIM
SSD_GROUPS = 2
SSD_HPG = SSD_HEADS // SSD_GROUPS
SSD_STATE = 128
SSD_CHUNK = 128
SSD_CONV = 5
SSD_GROUP_CH = SSD_HPG * SSD_HEAD_DIM
CONV_CH = 1024
CONV_WIDTH = 31
MLP_CH = 1024
MLP_GROUPS = 8
MLP_GROUP_CH = MLP_CH // MLP_GROUPS
MLP_CHUNK = 128
ATT_HEADS = 16
ATT_KV_HEADS = 4
ATT_REP = ATT_HEADS // ATT_KV_HEADS
ATT_HEAD_DIM = 64
ATT_WINDOW = 128
ATT_BLOCK = 128
ATT_SCALE = ATT_HEAD_DIM ** -0.5
ROPE_BASE = 10000.0
ATT_KV_CH = ATT_KV_HEADS * ATT_HEAD_DIM

DEEPNORM_ALPHA = (2 * DEPTH) ** 0.25
LN_EPS = 1e-5

PROJ_N = 5632
LANE = 128
SUBLANE = 8
VMEM_LIMIT = 56 * 1024 * 1024

EV_Z, EV_VAL, EV_GT, EV_GATE, EV_XS, EV_B, EV_C = 0, 1024, 2048, 3072, 4096, 5120, 5376
OD_U, OD_V, OD_GC, OD_Q, OD_GD, OD_K, OD_VA = 0, 1024, 2048, 3072, 4096, 5120, 5376


def _silu(t):
    return t * jax.nn.sigmoid(t)


def _gelu_tanh(t):
    c = math.sqrt(2.0 / math.pi)
    return t * (0.5 * (1.0 + jnp.tanh(c * (t + 0.044715 * (t * t * t)))))


def _softplus(t):
    return jnp.maximum(t, 0.0) + jnp.log1p(jnp.exp(-jnp.abs(t)))


def _layer_norm(t, g, b):
    mu = jnp.mean(t, -1, keepdims=True)
    d = t - mu
    var = jnp.mean(d * d, -1, keepdims=True)
    return d * lax.rsqrt(var + LN_EPS) * g + b


def _params(*sem):
    return pltpu.CompilerParams(dimension_semantics=sem, vmem_limit_bytes=VMEM_LIMIT)


def _mod_kernel(c_ref, w_ref, b_ref, o_ref):
    s = _silu(c_ref[...]).astype(BF16)
    o_ref[0] = jnp.dot(s, w_ref[0].astype(BF16), preferred_element_type=F32) + b_ref[0]


def _modulation(cond, mod_w, mod_b):
    rows = cond.shape[0]
    d = D_MODEL
    return pl.pallas_call(
        _mod_kernel,
        out_shape=jax.ShapeDtypeStruct((DEPTH, rows, 3 * d), F32),
        grid=(DEPTH, 3),
        in_specs=[pl.BlockSpec((rows, d), lambda l, j: (0, 0)),
                  pl.BlockSpec((1, d, d), lambda l, j: (l, 0, j)),
                  pl.BlockSpec((1, 1, d), lambda l, j: (l, 0, j))],
        out_specs=pl.BlockSpec((1, rows, d), lambda l, j: (l, 0, j)),
        compiler_params=_params("arbitrary", "arbitrary"),
        name="modulation",
    )(cond, mod_w, mod_b.reshape(DEPTH, 1, 3 * d))


def _inproj_kernel(x_ref, sc_ref, sh_ref, w_ref, *rest, with_dt):
    if with_dt:
        wdt_ref, wdtt_ref, o_ref, dt_ref, dtt_ref, h_ref = rest
    else:
        o_ref, h_ref = rest

    @pl.when(pl.program_id(2) == 0)
    def _():
        hb = (x_ref[0] * (1.0 + sc_ref[0]) + sh_ref[0]).astype(BF16)
        h_ref[...] = hb
        if with_dt:
            dt = jnp.dot(hb, wdt_ref[...], preferred_element_type=F32)
            dtt = lax.dot_general(wdtt_ref[...], hb, (((1,), (1,)), ((), ())),
                                  preferred_element_type=F32)
            per = dt.shape[1] // SSD_GROUPS
            for gi in range(SSD_GROUPS):
                dt_ref[0, gi] = dt[:, gi * per:(gi + 1) * per]
                dtt_ref[0, gi] = dtt[gi * per:(gi + 1) * per, :]

    o_ref[0] = jnp.dot(h_ref[...], w_ref[...], preferred_element_type=F32).astype(BF16)


def _inproj(x, sc, sh, w, w_dt=None):
    bsz, seq, d = x.shape
    n = w.shape[1]
    tm = min(seq, 1024)
    tn = 512
    with_dt = w_dt is not None
    in_specs = [pl.BlockSpec((1, tm, d), lambda b, i, j: (b, i, 0)),
                pl.BlockSpec((1, 1, d), lambda b, i, j: (b, 0, 0)),
                pl.BlockSpec((1, 1, d), lambda b, i, j: (b, 0, 0)),
                pl.BlockSpec((d, tn), lambda b, i, j: (0, j))]
    out_shape = [jax.ShapeDtypeStruct((bsz, seq, n), BF16)]
    out_specs = [pl.BlockSpec((1, tm, tn), lambda b, i, j: (b, i, j))]
    args = [x, sc, sh, w]
    if with_dt:
        ndt = w_dt.shape[1]
        per = ndt // SSD_GROUPS
        in_specs += [pl.BlockSpec((d, ndt), lambda b, i, j: (0, 0)),
                     pl.BlockSpec((ndt, d), lambda b, i, j: (0, 0))]
        out_shape += [jax.ShapeDtypeStruct((bsz, SSD_GROUPS, seq, per), F32),
                      jax.ShapeDtypeStruct((bsz, SSD_GROUPS, per, seq), F32)]
        out_specs += [pl.BlockSpec((1, SSD_GROUPS, tm, per), lambda b, i, j: (b, 0, i, 0)),
                      pl.BlockSpec((1, SSD_GROUPS, per, tm), lambda b, i, j: (b, 0, 0, i))]
        args += [w_dt, w_dt.T]
    return pl.pallas_call(
        functools.partial(_inproj_kernel, with_dt=with_dt),
        out_shape=out_shape,
        grid=(bsz, seq // tm, n // tn),
        in_specs=in_specs,
        out_specs=out_specs,
        scratch_shapes=[pltpu.VMEM((tm, d), BF16)],
        compiler_params=_params("arbitrary", "arbitrary", "arbitrary"),
        name="inproj",
    )(*args)


def _conv_silu(src_ref, w_ref, b_ref, pad_ref, dst_ref, seq):
    ch = src_ref.shape[-1]
    halo = SUBLANE
    pad_ref[pl.ds(0, halo), :] = jnp.zeros((halo, ch), F32)
    pad_ref[pl.ds(seq + halo, halo), :] = jnp.zeros((halo, ch), F32)
    pad_ref[pl.ds(halo, seq), :] = src_ref[0].astype(F32)
    w = w_ref[...]
    bias = b_ref[...]
    first = halo - SSD_CONV // 2

    def body(t, carry):
        base = pl.multiple_of(t * SSD_CHUNK, SSD_CHUNK)
        win = pad_ref[pl.ds(base, SSD_CHUNK + 2 * halo), :]
        acc = bias
        for k in range(SSD_CONV):
            acc = acc + w[k:k + 1, :] * win[first + k:first + k + SSD_CHUNK, :]
        dst_ref[pl.ds(base, SSD_CHUNK), :] = _silu(acc).astype(dst_ref.dtype)
        return carry

    lax.fori_loop(0, seq // SSD_CHUNK, body, 0)


def _pair_expand(vals, col0):
    q = vals.shape[0]
    lane = lax.broadcasted_iota(jnp.int32, (q, LANE), 1)
    parts = []
    for k in range(SSD_HPG // 2):
        a = jnp.broadcast_to(vals[:, col0 + 2 * k:col0 + 2 * k + 1], (q, LANE))
        b = jnp.broadcast_to(vals[:, col0 + 2 * k + 1:col0 + 2 * k + 2], (q, LANE))
        parts.append(jnp.where(lane < SSD_HEAD_DIM, a, b))
    return jnp.concatenate(parts, axis=-1)


def _split_dot(lhs, rhs):
    if lhs.dtype == F32:
        hi = lhs.astype(BF16)
        lo = (lhs - hi.astype(F32)).astype(BF16)
        return (jnp.dot(hi, rhs, preferred_element_type=F32)
                + jnp.dot(lo, rhs, preferred_element_type=F32))
    hi = rhs.astype(BF16)
    lo = (rhs - hi.astype(F32)).astype(BF16)
    return (jnp.dot(lhs, hi, preferred_element_type=F32)
            + jnp.dot(lhs, lo, preferred_element_type=F32))


def _ssd_kernel(xs_ref, bm_ref, cm_ref, dt_ref, dtt_ref, h0_ref,
                wx_ref, wb_ref, wc_ref, bx_ref, bb_ref, bc_ref,
                dtb_row_ref, dtb_col_ref, alog_row_ref, alog_col_ref, dskip_ref,
                y_ref, hout_ref,
                padx_ref, padn_ref, xc_ref, bcs_ref, ccs_ref, yacc_ref, state_ref, *, seq):
    q = SSD_CHUNK
    n_chunks = seq // q
    nh = 2 * SSD_HPG

    _conv_silu(xs_ref, wx_ref, bx_ref, padx_ref, xc_ref, seq)
    _conv_silu(bm_ref, wb_ref, bb_ref, padn_ref, bcs_ref, seq)
    _conv_silu(cm_ref, wc_ref, bc_ref, padn_ref, ccs_ref, seq)

    a_row = -jnp.exp(alog_row_ref[0])
    a_col = -jnp.exp(alog_col_ref[0])
    dtb_row = dtb_row_ref[0]
    dtb_col = dtb_col_ref[0]
    row_i = lax.broadcasted_iota(jnp.int32, (q, q), 0)
    col_i = lax.broadcasted_iota(jnp.int32, (q, q), 1)
    lower = row_i >= col_i
    upper = row_i <= col_i
    tri_lo = jnp.where(lower, 1.0, 0.0).astype(BF16)
    tri_up = jnp.where(upper, 1.0, 0.0).astype(BF16)
    lane = lax.broadcasted_iota(jnp.int32, (q, LANE), 1)
    head_lo = lane < SSD_HEAD_DIM

    def chunk(t, direction):
        base = pl.multiple_of(t * q, q)
        rows = pl.ds(base, q)
        dt = _softplus(dt_ref[0, 0, rows, :] + dtb_row)
        dtt = _softplus(dtt_ref[0, 0, :, rows] + dtb_col)
        da = dt * a_row
        dat = dtt * a_col
        if direction == 0:
            cum = _split_dot(tri_lo, da)
            cumt = _split_dot(dat, tri_up)
            mask = lower
            total = cum[q - 1:q, :]
        else:
            cum = _split_dot(tri_up, da)
            cumt = _split_dot(dat, tri_lo)
            mask = upper
            total = cum[0:1, :]
        col0 = direction * SSD_HPG
        xc = xc_ref[rows, :]
        bc = bcs_ref[rows, :]
        cc = ccs_ref[rows, :]
        cb = lax.dot_general(cc, bc, (((1,), (1,)), ((), ())), preferred_element_type=F32)
        xdt = xc * _pair_expand(dt, col0)
        xdt_b = xdt.astype(BF16)
        pieces = []
        for k in range(SSD_HPG // 2):
            xpair = xdt_b[:, k * LANE:(k + 1) * LANE]
            acc = None
            for half in range(2):
                c = col0 + 2 * k + half
                seg = jnp.broadcast_to(cum[:, c:c + 1], (q, q)) - jnp.broadcast_to(cumt[c:c + 1, :], (q, q))
                dec = jnp.exp(jnp.where(mask, seg, -jnp.inf))
                m = (cb * dec).astype(BF16)
                keep = head_lo if half == 0 else jnp.logical_not(head_lo)
                part = jnp.dot(m, jnp.where(keep, xpair, jnp.zeros_like(xpair)),
                               preferred_element_type=F32)
                acc = part if acc is None else acc + part
            pieces.append(acc)
        y_diag = jnp.concatenate(pieces, axis=-1)
        st = state_ref[direction]
        y_off = jnp.dot(cc, st.astype(BF16), preferred_element_type=F32) * _pair_expand(jnp.exp(cum), col0)
        to_edge = jnp.exp(total - cum)
        xw = (xdt * _pair_expand(to_edge, col0)).astype(BF16)
        bct = bc.astype(F32).T.astype(BF16)
        new = jnp.dot(bct, xw, preferred_element_type=F32)
        state_ref[direction] = st * _pair_expand(jnp.exp(total), col0) + new
        return rows, xc, y_diag + y_off

    state_ref[...] = h0_ref[0, :, 0]

    def fwd(t, carry):
        rows, _, y = chunk(t, 0)
        yacc_ref[rows, :] = y
        return carry

    lax.fori_loop(0, n_chunks, fwd, 0)

    dskip = dskip_ref[...]

    def bwd(i, carry):
        rows, xc, y = chunk(n_chunks - 1 - i, 1)
        y_ref[0, rows, :] = (yacc_ref[rows, :] + y + xc * dskip).astype(y_ref.dtype)
        return carry

    lax.fori_loop(0, n_chunks, bwd, 0)
    hout_ref[0, :, 0] = state_ref[...]


def _group_dt_order():
    order = []
    for g in range(SSD_GROUPS):
        for direction in range(2):
            order += [direction * SSD_HEADS + g * SSD_HPG + r for r in range(SSD_HPG)]
    return order


def _ssd(proj, dt, dtt, h0, conv_w, conv_b, dt_bias, a_log, d_skip):
    bsz, seq, _ = proj.shape
    gc = SSD_GROUP_CH
    ns = SSD_STATE
    nh = 2 * SSD_HPG
    xs_blk, b_blk, c_blk = EV_XS // gc, EV_B // ns, EV_C // ns
    cw_b0, cw_c0 = SSD_INNER // ns, (SSD_INNER + SSD_GROUPS * ns) // ns
    conv_b2 = conv_b.reshape(1, -1)
    d_skip_x = jnp.repeat(d_skip, SSD_HEAD_DIM).reshape(1, SSD_INNER)
    order = jnp.array(_group_dt_order(), jnp.int32)
    dt_bias_g = dt_bias.reshape(-1)[order].reshape(SSD_GROUPS, nh)
    a_log_g = a_log.reshape(-1)[order].reshape(SSD_GROUPS, nh)
    in_specs = [
        pl.BlockSpec((1, seq, gc), lambda b, g: (b, 0, xs_blk + g)),
        pl.BlockSpec((1, seq, ns), lambda b, g: (b, 0, b_blk + g)),
        pl.BlockSpec((1, seq, ns), lambda b, g: (b, 0, c_blk + g)),
        pl.BlockSpec((1, 1, seq, nh), lambda b, g: (b, g, 0, 0)),
        pl.BlockSpec((1, 1, nh, seq), lambda b, g: (b, g, 0, 0)),
        pl.BlockSpec((1, 2, 1, ns, gc), lambda b, g: (b, 0, g, 0, 0)),
        pl.BlockSpec((SSD_CONV, gc), lambda b, g: (0, g)),
        pl.BlockSpec((SSD_CONV, ns), lambda b, g: (0, cw_b0 + g)),
        pl.BlockSpec((SSD_CONV, ns), lambda b, g: (0, cw_c0 + g)),
        pl.BlockSpec((1, gc), lambda b, g: (0, g)),
        pl.BlockSpec((1, ns), lambda b, g: (0, cw_b0 + g)),
        pl.BlockSpec((1, ns), lambda b, g: (0, cw_c0 + g)),
        pl.BlockSpec((1, 1, nh), lambda b, g: (g, 0, 0)),
        pl.BlockSpec((1, nh, 1), lambda b, g: (g, 0, 0)),
        pl.BlockSpec((1, 1, nh), lambda b, g: (g, 0, 0)),
        pl.BlockSpec((1, nh, 1), lambda b, g: (g, 0, 0)),
        pl.BlockSpec((1, gc), lambda b, g: (0, g)),
    ]
    return pl.pallas_call(
        functools.partial(_ssd_kernel, seq=seq),
        out_shape=[jax.ShapeDtypeStruct((bsz, seq, SSD_INNER), BF16),
                   jax.ShapeDtypeStruct((bsz, 2, SSD_GROUPS, ns, gc), F32)],
        grid=(bsz, SSD_GROUPS),
        in_specs=in_specs,
        out_specs=[pl.BlockSpec((1, seq, gc), lambda b, g: (b, 0, g)),
                   pl.BlockSpec((1, 2, 1, ns, gc), lambda b, g: (b, 0, g, 0, 0))],
        scratch_shapes=[pltpu.VMEM((seq + 2 * SUBLANE, gc), F32),
                        pltpu.VMEM((seq + 2 * SUBLANE, ns), F32),
                        pltpu.VMEM((seq, gc), F32),
                        pltpu.VMEM((seq, ns), BF16),
                        pltpu.VMEM((seq, ns), BF16),
                        pltpu.VMEM((seq, gc), F32),
                        pltpu.VMEM((2, ns, gc), F32)],
        compiler_params=_params("arbitrary", "arbitrary"),
        name="ssd",
    )(proj, proj, proj, dt, dtt, h0,
      conv_w, conv_w, conv_w, conv_b2, conv_b2, conv_b2,
      dt_bias_g.reshape(SSD_GROUPS, 1, nh), dt_bias_g.reshape(SSD_GROUPS, nh, 1),
      a_log_g.reshape(SSD_GROUPS, 1, nh), a_log_g.reshape(SSD_GROUPS, nh, 1), d_skip_x)


CV_ROWS = 64
CV_HALO = 16


def _cvconv_kernel(val_ref, gt_ref, w_ref, b_ref, o_ref, pad_ref, *, seq):
    ch = val_ref.shape[-1]
    pad_ref[pl.ds(0, CV_HALO), :] = jnp.zeros((CV_HALO, ch), F32)
    pad_ref[pl.ds(seq + CV_HALO, CV_HALO), :] = jnp.zeros((CV_HALO, ch), F32)
    pad_ref[pl.ds(CV_HALO, seq), :] = val_ref[0].astype(F32) * jax.nn.sigmoid(gt_ref[0].astype(F32))
    w = w_ref[...]
    bias = b_ref[...]
    first = CV_HALO - CONV_WIDTH // 2

    def body(t, carry):
        base = pl.multiple_of(t * CV_ROWS, CV_ROWS)
        win = pad_ref[pl.ds(base, CV_ROWS + 2 * CV_HALO), :]
        acc = bias
        for k in range(CONV_WIDTH):
            acc = acc + w[k:k + 1, :] * win[first + k:first + k + CV_ROWS, :]
        o_ref[0, pl.ds(base, CV_ROWS), :] = acc.astype(o_ref.dtype)
        return carry

    lax.fori_loop(0, seq // CV_ROWS, body, 0)


def _cvconv(proj, cv_w, cv_b):
    bsz, seq, _ = proj.shape
    tc = 256
    return pl.pallas_call(
        functools.partial(_cvconv_kernel, seq=seq),
        out_shape=jax.ShapeDtypeStruct((bsz, seq, CONV_CH), BF16),
        grid=(bsz, CONV_CH // tc),
        in_specs=[pl.BlockSpec((1, seq, tc), lambda b, j: (b, 0, EV_VAL // tc + j)),
                  pl.BlockSpec((1, seq, tc), lambda b, j: (b, 0, EV_GT // tc + j)),
                  pl.BlockSpec((CONV_WIDTH, tc), lambda b, j: (0, j)),
                  pl.BlockSpec((1, tc), lambda b, j: (0, j))],
        out_specs=pl.BlockSpec((1, seq, tc), lambda b, j: (b, 0, j)),
        scratch_shapes=[pltpu.VMEM((seq + 2 * CV_HALO, tc), F32)],
        compiler_params=_params("arbitrary", "arbitrary"),
        name="cvconv",
    )(proj, proj, cv_w, cv_b.reshape(1, CONV_CH))


def _gmlp_kernel(u_ref, v_ref, gc_ref, lng_ref, lnb_ref, ws_ref, bias_ref, o_ref, *, tm):
    lng = lng_ref[...]
    lnb = lnb_ref[...]
    bias = bias_ref[...]
    for c in range(tm // MLP_CHUNK):
        rows = pl.ds(c * MLP_CHUNK, MLP_CHUNK)
        vn = _layer_norm(_gelu_tanh(v_ref[0, rows, :].astype(F32)), lng, lnb).astype(BF16)
        mixed = jnp.concatenate(
            [jnp.dot(ws_ref[gi], vn[:, gi * MLP_GROUP_CH:(gi + 1) * MLP_GROUP_CH], preferred_element_type=F32)
             for gi in range(MLP_GROUPS)], axis=-1)
        u = _gelu_tanh(u_ref[0, rows, :].astype(F32))
        o_ref[0, rows, :] = (u * (mixed + bias) * _silu(gc_ref[0, rows, :].astype(F32))).astype(o_ref.dtype)


def _gmlp(proj, ln_g, ln_b, ws, bs):
    bsz, seq, _ = proj.shape
    tm = min(seq, 512)
    ch = MLP_CH
    bias = jnp.repeat(bs.T, MLP_GROUP_CH, axis=1)
    return pl.pallas_call(
        functools.partial(_gmlp_kernel, tm=tm),
        out_shape=jax.ShapeDtypeStruct((bsz, seq, ch), BF16),
        grid=(bsz, seq // tm),
        in_specs=[pl.BlockSpec((1, tm, ch), lambda b, i: (b, i, OD_U // ch)),
                  pl.BlockSpec((1, tm, ch), lambda b, i: (b, i, OD_V // ch)),
                  pl.BlockSpec((1, tm, ch), lambda b, i: (b, i, OD_GC // ch)),
                  pl.BlockSpec((1, ch), lambda b, i: (0, 0)),
                  pl.BlockSpec((1, ch), lambda b, i: (0, 0)),
                  pl.BlockSpec((MLP_GROUPS, MLP_CHUNK, MLP_CHUNK), lambda b, i: (0, 0, 0)),
                  pl.BlockSpec((MLP_CHUNK, ch), lambda b, i: (0, 0))],
        out_specs=pl.BlockSpec((1, tm, ch), lambda b, i: (b, i, 0)),
        compiler_params=_params("arbitrary", "arbitrary"),
        name="gmlp",
    )(proj, proj, proj, ln_g.reshape(1, ch), ln_b.reshape(1, ch), ws.astype(BF16), bias)


def _rope(t, cos, sin_signed):
    width = t.shape[1]
    reps = width // LANE
    half = ATT_HEAD_DIM // 2
    lane = lax.broadcasted_iota(jnp.int32, t.shape, 1)
    first_half = (lane % ATT_HEAD_DIM) < half
    swapped = jnp.where(first_half, pltpu.roll(t, width - half, 1), pltpu.roll(t, half, 1))
    c = jnp.concatenate([cos] * reps, axis=-1)
    s = jnp.concatenate([sin_signed] * reps, axis=-1)
    return t * c + swapped * s


def _attn_kernel(*refs, seq, ctx_len, latent):
    if latent:
        (q_ref, gd_ref, k_ref, v_ref, kc_ref, vc_ref, sink_ref, cos_ref, sin_ref,
         o_ref, kpad_ref, vpad_ref) = refs
    else:
        q_ref, gd_ref, kc_ref, vc_ref, sink_ref, o_ref = refs
    i = pl.program_id(1)
    w = ATT_BLOCK
    hd = ATT_HEAD_DIM

    if latent:
        @pl.when(i == 0)
        def _():
            zeros = jnp.zeros((w, ATT_KV_CH), BF16)
            kpad_ref[pl.ds(0, w), :] = zeros
            kpad_ref[pl.ds(seq + w, w), :] = zeros
            vpad_ref[pl.ds(0, w), :] = zeros
            vpad_ref[pl.ds(seq + w, w), :] = zeros
            kpad_ref[pl.ds(w, seq), :] = _rope(k_ref[0].astype(F32), cos_ref[...], sin_ref[...]).astype(BF16)
            vpad_ref[pl.ds(w, seq), :] = v_ref[0]

        base = pl.multiple_of(i * w, w)
        q = _rope(q_ref[0].astype(F32), cos_ref[pl.ds(base, w), :], sin_ref[pl.ds(base, w), :])
        kwin = kpad_ref[pl.ds(base, 3 * w), :]
        vwin = vpad_ref[pl.ds(base, 3 * w), :]
        qi = lax.broadcasted_iota(jnp.int32, (w, 3 * w), 0)
        rel = lax.broadcasted_iota(jnp.int32, (w, 3 * w), 1) - w
        key_pos = base + rel
        allowed = (jnp.abs(qi - rel) <= ATT_WINDOW) & (key_pos >= 0) & (key_pos < seq)
    else:
        q = q_ref[0].astype(F32)
    qs = (q * ATT_SCALE).astype(BF16)
    kc = kc_ref[0]
    vc = vc_ref[0]
    gd = gd_ref[0].astype(F32)
    outs = []
    nt = (((1,), (1,)), ((), ()))
    for h in range(ATT_HEADS):
        kh = h // ATT_REP
        qh = qs[:, h * hd:(h + 1) * hd]
        s_ctx = lax.dot_general(qh, kc[:, kh * hd:(kh + 1) * hd], nt, preferred_element_type=F32)
        snk = sink_ref[:, h:h + 1]
        m = jnp.maximum(jnp.max(s_ctx, axis=-1, keepdims=True), snk)
        if latent:
            s_lat = lax.dot_general(qh, kwin[:, kh * hd:(kh + 1) * hd], nt, preferred_element_type=F32)
            s_lat = jnp.where(allowed, s_lat, -jnp.inf)
            m = jnp.maximum(m, jnp.max(s_lat, axis=-1, keepdims=True))
        p_ctx = jnp.exp(s_ctx - m)
        den = jnp.sum(p_ctx, axis=-1, keepdims=True) + jnp.exp(snk - m)
        acc = jnp.dot(p_ctx.astype(BF16), vc[:, kh * hd:(kh + 1) * hd], preferred_element_type=F32)
        if latent:
            p_lat = jnp.exp(s_lat - m)
            den = den + jnp.sum(p_lat, axis=-1, keepdims=True)
            acc = acc + jnp.dot(p_lat.astype(BF16), vwin[:, kh * hd:(kh + 1) * hd], preferred_element_type=F32)
        outs.append(acc / den)
    o_ref[0] = (jnp.concatenate(outs, axis=-1) * _silu(gd)).astype(o_ref.dtype)


def _attention(proj, proj_ctx, sink, cos, sin_signed, latent):
    bsz, seq, _ = proj.shape
    ctx_len = proj_ctx.shape[1]
    w = ATT_BLOCK
    qch = ATT_HEADS * ATT_HEAD_DIM
    kvc = ATT_KV_CH
    q_spec = pl.BlockSpec((1, w, qch), lambda b, i: (b, i, OD_Q // qch))
    gd_spec = pl.BlockSpec((1, w, qch), lambda b, i: (b, i, OD_GD // qch))
    kc_spec = pl.BlockSpec((1, ctx_len, kvc), lambda b, i: (b, 0, OD_K // kvc))
    vc_spec = pl.BlockSpec((1, ctx_len, kvc), lambda b, i: (b, 0, OD_VA // kvc))
    sink_spec = pl.BlockSpec((1, ATT_HEADS), lambda b, i: (0, 0))
    sink2 = sink.reshape(1, ATT_HEADS)
    if latent:
        in_specs = [q_spec, gd_spec,
                    pl.BlockSpec((1, seq, kvc), lambda b, i: (b, 0, OD_K // kvc)),
                    pl.BlockSpec((1, seq, kvc), lambda b, i: (b, 0, OD_VA // kvc)),
                    kc_spec, vc_spec, sink_spec,
                    pl.BlockSpec((seq, LANE), lambda b, i: (0, 0)),
                    pl.BlockSpec((seq, LANE), lambda b, i: (0, 0))]
        args = (proj, proj, proj, proj, proj_ctx, proj_ctx, sink2, cos, sin_signed)
        scratch = [pltpu.VMEM((seq + 2 * w, kvc), BF16), pltpu.VMEM((seq + 2 * w, kvc), BF16)]
    else:
        in_specs = [q_spec, gd_spec, kc_spec, vc_spec, sink_spec]
        args = (proj, proj, proj_ctx, proj_ctx, sink2)
        scratch = []
    return pl.pallas_call(
        functools.partial(_attn_kernel, seq=seq, ctx_len=ctx_len, latent=latent),
        out_shape=jax.ShapeDtypeStruct((bsz, seq, qch), BF16),
        grid=(bsz, seq // w),
        in_specs=in_specs,
        out_specs=pl.BlockSpec((1, w, qch), lambda b, i: (b, i, 0)),
        scratch_shapes=scratch,
        compiler_params=_params("arbitrary", "arbitrary"),
        name="attention" if latent else "ctx_attention",
    )(*args)


def _outproj_kernel(*refs, even):
    if even:
        (y_ref, z_ref, cv_ref, gate_ref, nw_ref, cvg_ref, cvb_ref,
         x_ref, g_ref, lng_ref, lnb_ref, w_ref, o_ref) = refs
        t = y_ref[0].astype(F32) * _silu(z_ref[0].astype(F32))
        ya = t * lax.rsqrt(jnp.mean(t * t, -1, keepdims=True) + LN_EPS) * nw_ref[...]
        yb = (_silu(_layer_norm(cv_ref[0].astype(F32), cvg_ref[...], cvb_ref[...]))
              * _silu(gate_ref[0].astype(F32)))
        ya = ya.astype(BF16)
        yb = yb.astype(BF16)
    else:
        ya_ref, yb_ref, x_ref, g_ref, lng_ref, lnb_ref, w_ref, o_ref = refs
        ya = ya_ref[0]
        yb = yb_ref[0]
    half = w_ref.shape[0] // 2
    y = (jnp.dot(ya, w_ref[pl.ds(0, half), :], preferred_element_type=F32)
         + jnp.dot(yb, w_ref[pl.ds(half, half), :], preferred_element_type=F32))
    r = DEEPNORM_ALPHA * x_ref[0] + g_ref[0] * y
    o_ref[0] = _layer_norm(r, lng_ref[...], lnb_ref[...])


def _outproj(mix_args, mix_specs, x, g, ln_g, ln_b, w_out, even, tm):
    bsz, seq, d = x.shape
    vec = pl.BlockSpec((1, d), lambda b, i: (0, 0))
    in_specs = list(mix_specs) + [
        pl.BlockSpec((1, tm, d), lambda b, i: (b, i, 0)),
        pl.BlockSpec((1, 1, d), lambda b, i: (b, 0, 0)),
        vec, vec,
        pl.BlockSpec(w_out.shape, lambda b, i: (0, 0))]
    return pl.pallas_call(
        functools.partial(_outproj_kernel, even=even),
        out_shape=jax.ShapeDtypeStruct((bsz, seq, d), F32),
        grid=(bsz, seq // tm),
        in_specs=in_specs,
        out_specs=pl.BlockSpec((1, tm, d), lambda b, i: (b, i, 0)),
        compiler_params=_params("arbitrary", "arbitrary"),
        name="outproj_even" if even else "outproj_odd",
    )(*mix_args, x, g, ln_g.reshape(1, d), ln_b.reshape(1, d), w_out)


def _outproj_even(y_ssd, cv, proj, norm_w, cv_ln_g, cv_ln_b, x, g, ln_g, ln_b, w_out):
    seq = x.shape[1]
    tm = min(seq, 512)
    ch = SSD_INNER
    blk = lambda col: pl.BlockSpec((1, tm, ch), lambda b, i: (b, i, col // ch))
    vec = pl.BlockSpec((1, ch), lambda b, i: (0, 0))
    specs = [blk(0), blk(EV_Z), blk(0), blk(EV_GATE), vec, vec, vec]
    args = (y_ssd, proj, cv, proj, norm_w.reshape(1, ch), cv_ln_g.reshape(1, ch), cv_ln_b.reshape(1, ch))
    return _outproj(args, specs, x, g, ln_g, ln_b, w_out, True, tm)


def _outproj_odd(yc, yd, x, g, ln_g, ln_b, w_out):
    seq = x.shape[1]
    tm = min(seq, 512)
    ch = MLP_CH
    blk = pl.BlockSpec((1, tm, ch), lambda b, i: (b, i, 0))
    return _outproj((yc, yd), [blk, blk], x, g, ln_g, ln_b, w_out, False, tm)


def _rope_tables(seq):
    t = jnp.arange(seq)
    row = (t // GRID_W).astype(F32)
    col = (t % GRID_W).astype(F32)
    n_freq = ATT_HEAD_DIM // 4
    inv = ROPE_BASE ** (-jnp.arange(n_freq, dtype=F32) / n_freq)
    ang = jnp.concatenate([row[:, None] * inv, col[:, None] * inv], -1)
    cos, sin = jnp.cos(ang), jnp.sin(ang)
    reps = LANE // ATT_HEAD_DIM
    return (jnp.tile(jnp.concatenate([cos, cos], -1), (1, reps)),
            jnp.tile(jnp.concatenate([-sin, sin], -1), (1, reps)))


def _even_weights(w_in):
    o_z, o_xbc, o_dt, o_glu, o_gate = 0, 1024, 2560, 2592, 4640
    main = jnp.concatenate([w_in[:, o_z:o_xbc], w_in[:, o_glu:o_gate], w_in[:, o_gate:],
                            w_in[:, o_xbc:o_dt]], axis=1)
    w_dt = w_in[:, o_dt:o_glu][:, jnp.array(_group_dt_order(), jnp.int32)]
    return main.astype(BF16), w_dt.astype(BF16)


def _odd_weights(w_in):
    o_k, o_gd = 4096, 4608
    return jnp.concatenate([w_in[:, :o_k], w_in[:, o_gd:], w_in[:, o_k:o_gd]], axis=1).astype(BF16)


def kernel(x, c, ctx, c_ctx, mod_w, mod_b, ln_g, ln_b, ev_w_in, ev_ssd_conv_w, ev_ssd_conv_b, ev_dt_bias, ev_a_log, ev_d_skip, ev_ssd_norm, ev_cv_w, ev_cv_b, ev_cv_ln_g, ev_cv_ln_b, ev_w_out, od_w_in, od_mlp_ln_g, od_mlp_ln_b, od_ws, od_bs, od_sink, od_w_out):
    bsz, seq, d = x.shape
    cos, sin_signed = _rope_tables(seq)
    rows = -(-(bsz + 1) // SUBLANE) * SUBLANE
    cond = jnp.concatenate([c, c_ctx[None, :], jnp.zeros((rows - bsz - 1, d), F32)], axis=0)
    mod = _modulation(cond, mod_w, mod_b)
    zero_state = jnp.zeros((bsz, 2, SSD_GROUPS, SSD_STATE, SSD_GROUP_CH), F32)

    for layer in range(DEPTH):
        last = layer == DEPTH - 1
        i = layer // 2
        m = mod[layer]
        sh_x, sc_x, g_x = (m[:bsz, None, k * d:(k + 1) * d] for k in range(3))
        sh_c, sc_c, g_c = (jnp.broadcast_to(m[bsz:bsz + 1, None, k * d:(k + 1) * d], (bsz, 1, d)) for k in range(3))
        if layer % 2 == 0:
            w_main, w_dt = _even_weights(ev_w_in[i])
            w_out = ev_w_out[i].astype(BF16)
            ssd_args = (ev_ssd_conv_w[i], ev_ssd_conv_b[i], ev_dt_bias[i], ev_a_log[i], ev_d_skip[i])
            p_c, dt_c, dtt_c = _inproj(ctx, sc_c, sh_c, w_main, w_dt)
            y_c, h_c = _ssd(p_c, dt_c, dtt_c, zero_state, *ssd_args)
            p_x, dt_x, dtt_x = _inproj(x, sc_x, sh_x, w_main, w_dt)
            y_x, _ = _ssd(p_x, dt_x, dtt_x, h_c, *ssd_args)
            cv_x = _cvconv(p_x, ev_cv_w[i], ev_cv_b[i])
            x = _outproj_even(y_x, cv_x, p_x, ev_ssd_norm[i], ev_cv_ln_g[i], ev_cv_ln_b[i],
                              x, g_x, ln_g[layer], ln_b[layer], w_out)
            if not last:
                cv_c = _cvconv(p_c, ev_cv_w[i], ev_cv_b[i])
                ctx = _outproj_even(y_c, cv_c, p_c, ev_ssd_norm[i], ev_cv_ln_g[i], ev_cv_ln_b[i],
                                    ctx, g_c, ln_g[layer], ln_b[layer], w_out)
        else:
            w_main = _odd_weights(od_w_in[i])
            w_out = od_w_out[i].astype(BF16)
            p_c = _inproj(ctx, sc_c, sh_c, w_main)[0]
            p_x = _inproj(x, sc_x, sh_x, w_main)[0]
            yd_x = _attention(p_x, p_c, od_sink[i], cos, sin_signed, True)
            yc_x = _gmlp(p_x, od_mlp_ln_g[i], od_mlp_ln_b[i], od_ws[i], od_bs[i])
            x = _outproj_odd(yc_x, yd_x, x, g_x, ln_g[layer], ln_b[layer], w_out)
            if not last:
                yd_c = _attention(p_c, p_c, od_sink[i], cos, sin_signed, False)
                yc_c = _gmlp(p_c, od_mlp_ln_g[i], od_mlp_ln_b[i], od_ws[i], od_bs[i])
                ctx = _outproj_odd(yc_c, yd_c, ctx, g_c, ln_g[layer], ln_b[layer], w_out)
    return x
```

```python
import functools
import math

import jax
import jax.numpy as jnp
from jax import lax
from jax.experimental import pallas as pl
from jax.experimental.pallas import tpu as pltpu

F32 = jnp.float32
BF16 = jnp.bfloat16

D_MODEL = 1024
DEPTH = 4
GRID_W = 64

SSD_HEADS = 16
SSD_HEAD_DIM = 64
SSD_INNER = SSD_HEADS * SSD_HEAD_DIM
SSD_GROUPS = 2
SSD_HPG = SSD_HEADS // SSD_GROUPS
SSD_STATE = 128
SSD_CHUNK = 128
SSD_CONV = 5
SSD_GROUP_CH = SSD_HPG * SSD_HEAD_DIM
CONV_CH = 1024
CONV_WIDTH = 31
MLP_CH = 1024
MLP_GROUPS = 8
MLP_GROUP_CH = MLP_CH // MLP_GROUPS
MLP_CHUNK = 128
ATT_HEADS = 16
ATT_KV_HEADS = 4
ATT_REP = ATT_HEADS // ATT_KV_HEADS
ATT_HEAD_DIM = 64
ATT_WINDOW = 128
ATT_BLOCK = 128
ATT_SCALE = ATT_HEAD_DIM ** -0.5
LOG2E = math.log2(math.e)
ROPE_BASE = 10000.0
ATT_KV_CH = ATT_KV_HEADS * ATT_HEAD_DIM

DEEPNORM_ALPHA = (2 * DEPTH) ** 0.25
LN_EPS = 1e-5

PROJ_N = 5632
LANE = 128
SUBLANE = 8
VMEM_LIMIT = 56 * 1024 * 1024

EV_Z, EV_VAL, EV_GT, EV_GATE, EV_XS, EV_B, EV_C = 0, 1024, 2048, 3072, 4096, 5120, 5376
OD_U, OD_V, OD_GC, OD_Q, OD_GD, OD_K, OD_VA = 0, 1024, 2048, 3072, 4096, 5120, 5376


def _silu(t):
    return t * jax.nn.sigmoid(t)


def _gelu_tanh(t):
    c = math.sqrt(2.0 / math.pi)
    return t * (0.5 * (1.0 + jnp.tanh(c * (t + 0.044715 * (t * t * t)))))


def _softplus(t):
    return jnp.maximum(t, 0.0) + jnp.log1p(jnp.exp(-jnp.abs(t)))


def _layer_norm(t, g, b):
    mu = jnp.mean(t, -1, keepdims=True)
    d = t - mu
    var = jnp.mean(d * d, -1, keepdims=True)
    return d * lax.rsqrt(var + LN_EPS) * g + b


def _params(*sem):
    return pltpu.CompilerParams(dimension_semantics=sem, vmem_limit_bytes=VMEM_LIMIT)


def _mod_kernel(c_ref, w_ref, b_ref, o_ref):
    s = _silu(c_ref[...]).astype(BF16)
    o_ref[0] = jnp.dot(s, w_ref[0].astype(BF16), preferred_element_type=F32) + b_ref[0]


def _modulation(cond, mod_w, mod_b):
    rows = cond.shape[0]
    d = D_MODEL
    return pl.pallas_call(
        _mod_kernel,
        out_shape=jax.ShapeDtypeStruct((DEPTH, rows, 3 * d), F32),
        grid=(DEPTH, 3),
        in_specs=[pl.BlockSpec((rows, d), lambda l, j: (0, 0)),
                  pl.BlockSpec((1, d, d), lambda l, j: (l, 0, j)),
                  pl.BlockSpec((1, 1, d), lambda l, j: (l, 0, j))],
        out_specs=pl.BlockSpec((1, rows, d), lambda l, j: (l, 0, j)),
        compiler_params=_params("arbitrary", "arbitrary"),
        name="modulation",
    )(cond, mod_w, mod_b.reshape(DEPTH, 1, 3 * d))


def _inproj_kernel(x_ref, sc_ref, sh_ref, w_ref, *rest, with_dt):
    if with_dt:
        wdt_ref, wdtt_ref, o_ref, dt_ref, dtt_ref, h_ref = rest
    else:
        o_ref, h_ref = rest

    @pl.when(pl.program_id(2) == 0)
    def _():
        hb = (x_ref[0] * (1.0 + sc_ref[0]) + sh_ref[0]).astype(BF16)
        h_ref[...] = hb
        if with_dt:
            dt = jnp.dot(hb, wdt_ref[...], preferred_element_type=F32)
            dtt = lax.dot_general(wdtt_ref[...], hb, (((1,), (1,)), ((), ())),
                                  preferred_element_type=F32)
            per = dt.shape[1] // SSD_GROUPS
            for gi in range(SSD_GROUPS):
                dt_ref[0, gi] = dt[:, gi * per:(gi + 1) * per]
                dtt_ref[0, gi] = dtt[gi * per:(gi + 1) * per, :]

    o_ref[0] = jnp.dot(h_ref[...], w_ref[...], preferred_element_type=F32).astype(BF16)


def _inproj(x, sc, sh, w, w_dt=None):
    bsz, seq, d = x.shape
    n = w.shape[1]
    tm = min(seq, 1024)
    tn = 512
    with_dt = w_dt is not None
    in_specs = [pl.BlockSpec((1, tm, d), lambda b, i, j: (b, i, 0)),
                pl.BlockSpec((1, 1, d), lambda b, i, j: (b, 0, 0)),
                pl.BlockSpec((1, 1, d), lambda b, i, j: (b, 0, 0)),
                pl.BlockSpec((d, tn), lambda b, i, j: (0, j))]
    out_shape = [jax.ShapeDtypeStruct((bsz, seq, n), BF16)]
    out_specs = [pl.BlockSpec((1, tm, tn), lambda b, i, j: (b, i, j))]
    args = [x, sc, sh, w]
    if with_dt:
        ndt = w_dt.shape[1]
        per = ndt // SSD_GROUPS
        in_specs += [pl.BlockSpec((d, ndt), lambda b, i, j: (0, 0)),
                     pl.BlockSpec((ndt, d), lambda b, i, j: (0, 0))]
        out_shape += [jax.ShapeDtypeStruct((bsz, SSD_GROUPS, seq, per), F32),
                      jax.ShapeDtypeStruct((bsz, SSD_GROUPS, per, seq), F32)]
        out_specs += [pl.BlockSpec((1, SSD_GROUPS, tm, per), lambda b, i, j: (b, 0, i, 0)),
                      pl.BlockSpec((1, SSD_GROUPS, per, tm), lambda b, i, j: (b, 0, 0, i))]
        args += [w_dt, w_dt.T]
    return pl.pallas_call(
        functools.partial(_inproj_kernel, with_dt=with_dt),
        out_shape=out_shape,
        grid=(bsz, seq // tm, n // tn),
        in_specs=in_specs,
        out_specs=out_specs,
        scratch_shapes=[pltpu.VMEM((tm, d), BF16)],
        compiler_params=_params("arbitrary", "arbitrary", "arbitrary"),
        name="inproj",
    )(*args)


def _conv_silu(src_ref, w_ref, b_ref, pad_ref, dst_ref, seq):
    ch = src_ref.shape[-1]
    halo = SUBLANE
    pad_ref[pl.ds(0, halo), :] = jnp.zeros((halo, ch), F32)
    pad_ref[pl.ds(seq + halo, halo), :] = jnp.zeros((halo, ch), F32)
    pad_ref[pl.ds(halo, seq), :] = src_ref[0].astype(F32)
    w = w_ref[...]
    bias = b_ref[...]
    first = halo - SSD_CONV // 2

    def body(t, carry):
        base = pl.multiple_of(t * SSD_CHUNK, SSD_CHUNK)
        win = pad_ref[pl.ds(base, SSD_CHUNK + 2 * halo), :]
        acc = bias
        for k in range(SSD_CONV):
            acc = acc + w[k:k + 1, :] * win[first + k:first + k + SSD_CHUNK, :]
        dst_ref[pl.ds(base, SSD_CHUNK), :] = _silu(acc).astype(dst_ref.dtype)
        return carry

    lax.fori_loop(0, seq // SSD_CHUNK, body, 0)


def _pair_expand(vals, col0):
    q = vals.shape[0]
    lane = lax.broadcasted_iota(jnp.int32, (q, LANE), 1)
    parts = []
    for k in range(SSD_HPG // 2):
        a = jnp.broadcast_to(vals[:, col0 + 2 * k:col0 + 2 * k + 1], (q, LANE))
        b = jnp.broadcast_to(vals[:, col0 + 2 * k + 1:col0 + 2 * k + 2], (q, LANE))
        parts.append(jnp.where(lane < SSD_HEAD_DIM, a, b))
    return jnp.concatenate(parts, axis=-1)


def _split_dot(lhs, rhs):
    if lhs.dtype == F32:
        hi = lhs.astype(BF16)
        lo = (lhs - hi.astype(F32)).astype(BF16)
        return (jnp.dot(hi, rhs, preferred_element_type=F32)
                + jnp.dot(lo, rhs, preferred_element_type=F32))
    hi = rhs.astype(BF16)
    lo = (rhs - hi.astype(F32)).astype(BF16)
    return (jnp.dot(lhs, hi, preferred_element_type=F32)
            + jnp.dot(lhs, lo, preferred_element_type=F32))


def _ssd_kernel(xs_ref, bm_ref, cm_ref, dt_ref, dtt_ref, h0_ref,
                wx_ref, wb_ref, wc_ref, bx_ref, bb_ref, bc_ref,
                dtb_row_ref, dtb_col_ref, alog_row_ref, alog_col_ref, dskip_ref,
                y_ref, hout_ref,
                padx_ref, padn_ref, xc_ref, bcs_ref, ccs_ref, yfwd_ref, ybwd_ref, state_ref, *, seq):
    q = SSD_CHUNK
    n_chunks = seq // q
    nh = 2 * SSD_HPG

    _conv_silu(xs_ref, wx_ref, bx_ref, padx_ref, xc_ref, seq)
    _conv_silu(bm_ref, wb_ref, bb_ref, padn_ref, bcs_ref, seq)
    _conv_silu(cm_ref, wc_ref, bc_ref, padn_ref, ccs_ref, seq)

    a_row = -jnp.exp(alog_row_ref[0])
    a_col = -jnp.exp(alog_col_ref[0])
    dtb_row = dtb_row_ref[0]
    dtb_col = dtb_col_ref[0]
    row_i = lax.broadcasted_iota(jnp.int32, (q, q), 0)
    col_i = lax.broadcasted_iota(jnp.int32, (q, q), 1)
    lower = row_i >= col_i
    upper = row_i <= col_i
    tri_lo = jnp.where(lower, 1.0, 0.0).astype(BF16)
    tri_up = jnp.where(upper, 1.0, 0.0).astype(BF16)
    lane = lax.broadcasted_iota(jnp.int32, (q, LANE), 1)
    head_lo = lane < SSD_HEAD_DIM

    def chunk(t, direction):
        base = pl.multiple_of(t * q, q)
        rows = pl.ds(base, q)
        dt = _softplus(dt_ref[0, 0, rows, :] + dtb_row)
        dtt = _softplus(dtt_ref[0, 0, :, rows] + dtb_col)
        da = dt * a_row
        dat = dtt * a_col
        if direction == 0:
            cum = _split_dot(tri_lo, da)
            cumt = _split_dot(dat, tri_up)
            mask = lower
        else:
            cum = _split_dot(tri_up, da)
            cumt = _split_dot(dat, tri_lo)
            mask = upper
        edge = q - 1 if direction == 0 else 0
        total = cum[edge:edge + 1, :]
        wt = dtt * jnp.exp(cumt[:, edge:edge + 1] - cumt)
        col0 = direction * SSD_HPG
        xb = xc_ref[rows, :]
        bc = bcs_ref[rows, :]
        cc = ccs_ref[rows, :]
        cb = lax.dot_general(cc, bc, (((1,), (1,)), ((), ())), preferred_element_type=F32)
        ccf = cc.astype(F32)
        bctf = bc.astype(F32).T
        st = state_ref[direction]
        stb = st.astype(BF16)
        zero = jnp.zeros((q, LANE), BF16)
        ys, news = [], []
        for k in range(SSD_HPG // 2):
            xp = xb[:, k * LANE:(k + 1) * LANE]
            sp = stb[:, k * LANE:(k + 1) * LANE]
            rhs_x = jnp.concatenate([jnp.where(head_lo, xp, zero), jnp.where(head_lo, zero, xp)], axis=0)
            rhs_s = jnp.concatenate([jnp.where(head_lo, sp, zero), jnp.where(head_lo, zero, sp)], axis=0)
            ms, cs, ws = [], [], []
            for half in range(2):
                c = col0 + 2 * k + half
                bcol = jnp.broadcast_to(cum[:, c:c + 1], (q, q))
                seg = bcol - jnp.broadcast_to(cumt[c:c + 1, :], (q, q))
                dec = jnp.exp(jnp.where(mask, seg, -jnp.inf))
                ms.append((cb * dec * jnp.broadcast_to(dtt[c:c + 1, :], (q, q))).astype(BF16))
                cs.append((ccf * jnp.exp(bcol)).astype(BF16))
                ws.append((bctf * jnp.broadcast_to(wt[c:c + 1, :], (q, q))).astype(BF16))
            ys.append(jnp.dot(jnp.concatenate(ms, axis=1), rhs_x, preferred_element_type=F32)
                      + jnp.dot(jnp.concatenate(cs, axis=1), rhs_s, preferred_element_type=F32))
            news.append(jnp.dot(jnp.concatenate(ws, axis=1), rhs_x, preferred_element_type=F32))
        state_ref[direction] = st * _pair_expand(jnp.exp(total), col0) + jnp.concatenate(news, axis=-1)
        return rows, jnp.concatenate(ys, axis=-1)

    state_ref[...] = h0_ref[0, :, 0]

    def both(i, carry):
        rows, y = chunk(i, 0)
        yfwd_ref[rows, :] = y
        rows, y = chunk(n_chunks - 1 - i, 1)
        ybwd_ref[rows, :] = y
        return carry

    lax.fori_loop(0, n_chunks, both, 0)
    hout_ref[0, :, 0] = state_ref[...]

    dskip = dskip_ref[...]

    def finish(t, carry):
        rows = pl.ds(pl.multiple_of(t * q, q), q)
        y = yfwd_ref[rows, :] + ybwd_ref[rows, :] + xc_ref[rows, :].astype(F32) * dskip
        y_ref[0, rows, :] = y.astype(y_ref.dtype)
        return carry

    lax.fori_loop(0, n_chunks, finish, 0)


def _group_dt_order():
    order = []
    for g in range(SSD_GROUPS):
        for direction in range(2):
            order += [direction * SSD_HEADS + g * SSD_HPG + r for r in range(SSD_HPG)]
    return order


def _ssd(proj, dt, dtt, h0, conv_w, conv_b, dt_bias, a_log, d_skip):
    bsz, seq, _ = proj.shape
    gc = SSD_GROUP_CH
    ns = SSD_STATE
    nh = 2 * SSD_HPG
    xs_blk, b_blk, c_blk = EV_XS // gc, EV_B // ns, EV_C // ns
    cw_b0, cw_c0 = SSD_INNER // ns, (SSD_INNER + SSD_GROUPS * ns) // ns
    conv_b2 = conv_b.reshape(1, -1)
    d_skip_x = jnp.repeat(d_skip, SSD_HEAD_DIM).reshape(1, SSD_INNER)
    order = jnp.array(_group_dt_order(), jnp.int32)
    dt_bias_g = dt_bias.reshape(-1)[order].reshape(SSD_GROUPS, nh)
    a_log_g = a_log.reshape(-1)[order].reshape(SSD_GROUPS, nh)
    in_specs = [
        pl.BlockSpec((1, seq, gc), lambda b, g: (b, 0, xs_blk + g)),
        pl.BlockSpec((1, seq, ns), lambda b, g: (b, 0, b_blk + g)),
        pl.BlockSpec((1, seq, ns), lambda b, g: (b, 0, c_blk + g)),
        pl.BlockSpec((1, 1, seq, nh), lambda b, g: (b, g, 0, 0)),
        pl.BlockSpec((1, 1, nh, seq), lambda b, g: (b, g, 0, 0)),
        pl.BlockSpec((1, 2, 1, ns, gc), lambda b, g: (b, 0, g, 0, 0)),
        pl.BlockSpec((SSD_CONV, gc), lambda b, g: (0, g)),
        pl.BlockSpec((SSD_CONV, ns), lambda b, g: (0, cw_b0 + g)),
        pl.BlockSpec((SSD_CONV, ns), lambda b, g: (0, cw_c0 + g)),
        pl.BlockSpec((1, gc), lambda b, g: (0, g)),
        pl.BlockSpec((1, ns), lambda b, g: (0, cw_b0 + g)),
        pl.BlockSpec((1, ns), lambda b, g: (0, cw_c0 + g)),
        pl.BlockSpec((1, 1, nh), lambda b, g: (g, 0, 0)),
        pl.BlockSpec((1, nh, 1), lambda b, g: (g, 0, 0)),
        pl.BlockSpec((1, 1, nh), lambda b, g: (g, 0, 0)),
        pl.BlockSpec((1, nh, 1), lambda b, g: (g, 0, 0)),
        pl.BlockSpec((1, gc), lambda b, g: (0, g)),
    ]
    return pl.pallas_call(
        functools.partial(_ssd_kernel, seq=seq),
        out_shape=[jax.ShapeDtypeStruct((bsz, seq, SSD_INNER), BF16),
                   jax.ShapeDtypeStruct((bsz, 2, SSD_GROUPS, ns, gc), F32)],
        grid=(bsz, SSD_GROUPS),
        in_specs=in_specs,
        out_specs=[pl.BlockSpec((1, seq, gc), lambda b, g: (b, 0, g)),
                   pl.BlockSpec((1, 2, 1, ns, gc), lambda b, g: (b, 0, g, 0, 0))],
        scratch_shapes=[pltpu.VMEM((seq + 2 * SUBLANE, gc), F32),
                        pltpu.VMEM((seq + 2 * SUBLANE, ns), F32),
                        pltpu.VMEM((seq, gc), BF16),
                        pltpu.VMEM((seq, ns), BF16),
                        pltpu.VMEM((seq, ns), BF16),
                        pltpu.VMEM((seq, gc), F32),
                        pltpu.VMEM((seq, gc), F32),
                        pltpu.VMEM((2, ns, gc), F32)],
        compiler_params=_params("arbitrary", "arbitrary"),
        name="ssd",
    )(proj, proj, proj, dt, dtt, h0,
      conv_w, conv_w, conv_w, conv_b2, conv_b2, conv_b2,
      dt_bias_g.reshape(SSD_GROUPS, 1, nh), dt_bias_g.reshape(SSD_GROUPS, nh, 1),
      a_log_g.reshape(SSD_GROUPS, 1, nh), a_log_g.reshape(SSD_GROUPS, nh, 1), d_skip_x)


CV_ROWS = 256
CV_HALO = 16
CV_TC = 128
CV_SPAN = CV_ROWS + 2 * CV_HALO - SUBLANE


def _cvconv_kernel(val_ref, gt_ref, w_ref, b_ref, o_ref, pad_ref, sh_ref, *, seq):
    ch = val_ref.shape[-1]
    pad_ref[pl.ds(0, CV_HALO), :] = jnp.zeros((CV_HALO, ch), F32)
    pad_ref[pl.ds(seq + CV_HALO, CV_HALO), :] = jnp.zeros((CV_HALO, ch), F32)
    pad_ref[pl.ds(CV_HALO, seq), :] = val_ref[0].astype(F32) * jax.nn.sigmoid(gt_ref[0].astype(F32))
    w = w_ref[...]
    bias = b_ref[...]
    first = CV_HALO - CONV_WIDTH // 2
    rows = min(CV_ROWS, seq)
    span = rows + 2 * CV_HALO - SUBLANE

    def body(t, carry):
        base = pl.multiple_of(t * rows, rows)
        win = pad_ref[pl.ds(base, rows + 2 * CV_HALO), :]
        for s in range(SUBLANE):
            sh_ref[s, pl.ds(0, span), :] = win[s:s + span, :]
        acc = jnp.broadcast_to(bias, (rows, ch))
        for k in range(CONV_WIDTH):
            a, s = divmod(first + k, SUBLANE)
            acc = acc + w[k:k + 1, :] * sh_ref[s, pl.ds(a * SUBLANE, rows), :]
        o_ref[0, pl.ds(base, rows), :] = acc.astype(o_ref.dtype)
        return carry

    lax.fori_loop(0, seq // rows, body, 0)


def _cvconv(proj, cv_w, cv_b):
    bsz, seq, _ = proj.shape
    tc = CV_TC
    return pl.pallas_call(
        functools.partial(_cvconv_kernel, seq=seq),
        out_shape=jax.ShapeDtypeStruct((bsz, seq, CONV_CH), BF16),
        grid=(bsz, CONV_CH // tc),
        in_specs=[pl.BlockSpec((1, seq, tc), lambda b, j: (b, 0, EV_VAL // tc + j)),
                  pl.BlockSpec((1, seq, tc), lambda b, j: (b, 0, EV_GT // tc + j)),
                  pl.BlockSpec((CONV_WIDTH, tc), lambda b, j: (0, j)),
                  pl.BlockSpec((1, tc), lambda b, j: (0, j))],
        out_specs=pl.BlockSpec((1, seq, tc), lambda b, j: (b, 0, j)),
        scratch_shapes=[pltpu.VMEM((seq + 2 * CV_HALO, tc), F32),
                        pltpu.VMEM((SUBLANE, CV_SPAN, tc), F32)],
        compiler_params=_params("arbitrary", "arbitrary"),
        name="cvconv",
    )(proj, proj, cv_w, cv_b.reshape(1, CONV_CH))


def _gmlp_kernel(u_ref, v_ref, gc_ref, lng_ref, lnb_ref, ws_ref, bias_ref, o_ref, *, tm):
    lng = lng_ref[...]
    lnb = lnb_ref[...]
    bias = bias_ref[...]
    for c in range(tm // MLP_CHUNK):
        rows = pl.ds(c * MLP_CHUNK, MLP_CHUNK)
        vn = _layer_norm(_gelu_tanh(v_ref[0, rows, :].astype(F32)), lng, lnb).astype(BF16)
        mixed = jnp.concatenate(
            [jnp.dot(ws_ref[gi], vn[:, gi * MLP_GROUP_CH:(gi + 1) * MLP_GROUP_CH], preferred_element_type=F32)
             for gi in range(MLP_GROUPS)], axis=-1)
        u = _gelu_tanh(u_ref[0, rows, :].astype(F32))
        o_ref[0, rows, :] = (u * (mixed + bias) * _silu(gc_ref[0, rows, :].astype(F32))).astype(o_ref.dtype)


def _gmlp(proj, ln_g, ln_b, ws, bs):
    bsz, seq, _ = proj.shape
    tm = min(seq, 512)
    ch = MLP_CH
    bias = jnp.repeat(bs.T, MLP_GROUP_CH, axis=1)
    return pl.pallas_call(
        functools.partial(_gmlp_kernel, tm=tm),
        out_shape=jax.ShapeDtypeStruct((bsz, seq, ch), BF16),
        grid=(bsz, seq // tm),
        in_specs=[pl.BlockSpec((1, tm, ch), lambda b, i: (b, i, OD_U // ch)),
                  pl.BlockSpec((1, tm, ch), lambda b, i: (b, i, OD_V // ch)),
                  pl.BlockSpec((1, tm, ch), lambda b, i: (b, i, OD_GC // ch)),
                  pl.BlockSpec((1, ch), lambda b, i: (0, 0)),
                  pl.BlockSpec((1, ch), lambda b, i: (0, 0)),
                  pl.BlockSpec((MLP_GROUPS, MLP_CHUNK, MLP_CHUNK), lambda b, i: (0, 0, 0)),
                  pl.BlockSpec((MLP_CHUNK, ch), lambda b, i: (0, 0))],
        out_specs=pl.BlockSpec((1, tm, ch), lambda b, i: (b, i, 0)),
        compiler_params=_params("arbitrary", "arbitrary"),
        name="gmlp",
    )(proj, proj, proj, ln_g.reshape(1, ch), ln_b.reshape(1, ch), ws.astype(BF16), bias)


def _rope(t, cos, sin_signed):
    width = t.shape[1]
    reps = width // LANE
    half = ATT_HEAD_DIM // 2
    lane = lax.broadcasted_iota(jnp.int32, t.shape, 1)
    first_half = (lane % ATT_HEAD_DIM) < half
    swapped = jnp.where(first_half, pltpu.roll(t, width - half, 1), pltpu.roll(t, half, 1))
    c = jnp.concatenate([cos] * reps, axis=-1)
    s = jnp.concatenate([sin_signed] * reps, axis=-1)
    return t * c + swapped * s


def _att_head_order():
    order = []
    for c in range(ATT_HEADS // 2):
        j, r = divmod(c, ATT_REP)
        order += [(2 * j) * ATT_REP + r, (2 * j + 1) * ATT_REP + r]
    return order


def _att_col_perm():
    return [h * ATT_HEAD_DIM + d for h in _att_head_order() for d in range(ATT_HEAD_DIM)]


def _attn_kernel(*refs, seq, ctx_len, latent):
    if latent:
        (q_ref, gd_ref, k_ref, v_ref, kc_ref, vc_ref, sink_ref, cos_ref, sin_ref,
         o_ref, kpad_ref, vtp_ref, vct_ref) = refs
    else:
        q_ref, gd_ref, kc_ref, vc_ref, sink_ref, o_ref, vct_ref = refs
    i = pl.program_id(1)
    w = ATT_BLOCK
    nt = (((1,), (1,)), ((), ()))

    @pl.when(i == 0)
    def _():
        vct_ref[...] = vc_ref[0].astype(F32).T.astype(BF16)
        if latent:
            kpad_ref[pl.ds(0, w), :] = jnp.zeros((w, ATT_KV_CH), BF16)
            kpad_ref[pl.ds(seq + w, w), :] = jnp.zeros((w, ATT_KV_CH), BF16)
            kpad_ref[pl.ds(w, seq), :] = _rope(k_ref[0].astype(F32), cos_ref[...], sin_ref[...]).astype(BF16)
            vtp_ref[:, pl.ds(0, w)] = jnp.zeros((ATT_KV_CH, w), BF16)
            vtp_ref[:, pl.ds(seq + w, w)] = jnp.zeros((ATT_KV_CH, w), BF16)
            for t in range(seq // w):
                vtp_ref[:, pl.ds((t + 1) * w, w)] = v_ref[0, pl.ds(t * w, w), :].astype(F32).T.astype(BF16)

    if latent:
        base = pl.multiple_of(i * w, w)
        q = _rope(q_ref[0].astype(F32), cos_ref[pl.ds(base, w), :], sin_ref[pl.ds(base, w), :])
        kwin = kpad_ref[pl.ds(base, 3 * w), :]
        vtwin = vtp_ref[:, pl.ds(base, 3 * w)]
        rel = lax.broadcasted_iota(jnp.int32, (3 * w, w), 0) - w
        qi = lax.broadcasted_iota(jnp.int32, (3 * w, w), 1)
        key_pos = base + rel
        allowed = (jnp.abs(qi - rel) <= ATT_WINDOW) & (key_pos >= 0) & (key_pos < seq)
        bias = jnp.where(allowed, 0.0, -jnp.inf)
        bias2 = jnp.concatenate([bias, bias], axis=1)
    else:
        q = q_ref[0].astype(F32)
    qs = (q * (ATT_SCALE * LOG2E)).astype(BF16)
    kc = kc_ref[0]
    vct = vct_ref[...]
    lane = lax.broadcasted_iota(jnp.int32, (w, LANE), 1)
    low_lanes = lane < ATT_HEAD_DIM
    low_rows = lax.broadcasted_iota(jnp.int32, (LANE, w), 0) < ATT_HEAD_DIM
    zero = jnp.zeros((w, LANE), BF16)
    outs = []
    for c in range(ATT_HEADS // 2):
        j, r = divmod(c, ATT_REP)
        kv = slice(j * LANE, (j + 1) * LANE)
        qc = qs[:, c * LANE:(c + 1) * LANE]
        rhs = jnp.concatenate([jnp.where(low_lanes, qc, zero), jnp.where(low_lanes, zero, qc)], axis=0)
        s_ctx = lax.dot_general(kc[:, kv], rhs, nt, preferred_element_type=F32)
        snk = sink_ref[0, :, pl.ds(c * 2 * w, 2 * w)]
        m = jnp.maximum(jnp.max(s_ctx, axis=0, keepdims=True), snk)
        if latent:
            s_lat = lax.dot_general(kwin[:, kv], rhs, nt, preferred_element_type=F32) + bias2
            m = jnp.maximum(m, jnp.max(s_lat, axis=0, keepdims=True))
        p_ctx = jnp.exp2(s_ctx - m)
        den = jnp.sum(p_ctx, axis=0, keepdims=True) + jnp.exp2(snk - m)
        acc = jnp.dot(vct[kv, :], p_ctx.astype(BF16), preferred_element_type=F32)
        if latent:
            p_lat = jnp.exp2(s_lat - m)
            den = den + jnp.sum(p_lat, axis=0, keepdims=True)
            acc = acc + jnp.dot(vtwin[kv, :], p_lat.astype(BF16), preferred_element_type=F32)
        acc = acc * (1.0 / den)
        outs.append(jnp.where(low_rows, acc[:, :w], acc[:, w:]).T)
    gd = gd_ref[0].astype(F32)
    o_ref[0] = (jnp.concatenate(outs, axis=-1) * _silu(gd)).astype(o_ref.dtype)


def _attention(proj, proj_ctx, sink, cos, sin_signed, latent):
    bsz, seq, _ = proj.shape
    ctx_len = proj_ctx.shape[1]
    w = ATT_BLOCK
    qch = ATT_HEADS * ATT_HEAD_DIM
    kvc = ATT_KV_CH
    q_spec = pl.BlockSpec((1, w, qch), lambda b, i: (b, i, OD_Q // qch))
    gd_spec = pl.BlockSpec((1, w, qch), lambda b, i: (b, i, OD_GD // qch))
    kc_spec = pl.BlockSpec((1, ctx_len, kvc), lambda b, i: (b, 0, OD_K // kvc))
    vc_spec = pl.BlockSpec((1, ctx_len, kvc), lambda b, i: (b, 0, OD_VA // kvc))
    sink2 = jnp.repeat(sink[jnp.array(_att_head_order(), jnp.int32)] * LOG2E, w).reshape(1, 1, ATT_HEADS * w)
    sink_spec = pl.BlockSpec((1, 1, ATT_HEADS * w), lambda b, i: (0, 0, 0))
    vct = pltpu.VMEM((kvc, ctx_len), BF16)
    if latent:
        in_specs = [q_spec, gd_spec,
                    pl.BlockSpec((1, seq, kvc), lambda b, i: (b, 0, OD_K // kvc)),
                    pl.BlockSpec((1, seq, kvc), lambda b, i: (b, 0, OD_VA // kvc)),
                    kc_spec, vc_spec, sink_spec,
                    pl.BlockSpec((seq, LANE), lambda b, i: (0, 0)),
                    pl.BlockSpec((seq, LANE), lambda b, i: (0, 0))]
        args = (proj, proj, proj, proj, proj_ctx, proj_ctx, sink2, cos, sin_signed)
        scratch = [pltpu.VMEM((seq + 2 * w, kvc), BF16), pltpu.VMEM((kvc, seq + 2 * w), BF16), vct]
    else:
        in_specs = [q_spec, gd_spec, kc_spec, vc_spec, sink_spec]
        args = (proj, proj, proj_ctx, proj_ctx, sink2)
        scratch = [vct]
    return pl.pallas_call(
        functools.partial(_attn_kernel, seq=seq, ctx_len=ctx_len, latent=latent),
        out_shape=jax.ShapeDtypeStruct((bsz, seq, qch), BF16),
        grid=(bsz, seq // w),
        in_specs=in_specs,
        out_specs=pl.BlockSpec((1, w, qch), lambda b, i: (b, i, 0)),
        scratch_shapes=scratch,
        compiler_params=_params("arbitrary", "arbitrary"),
        name="attention" if latent else "ctx_attention",
    )(*args)


def _outproj_kernel(*refs, even):
    if even:
        (y_ref, z_ref, cv_ref, gate_ref, nw_ref, cvg_ref, cvb_ref,
         x_ref, g_ref, lng_ref, lnb_ref, w_ref, o_ref) = refs
        t = y_ref[0].astype(F32) * _silu(z_ref[0].astype(F32))
        ya = t * lax.rsqrt(jnp.mean(t * t, -1, keepdims=True) + LN_EPS) * nw_ref[...]
        yb = (_silu(_layer_norm(cv_ref[0].astype(F32), cvg_ref[...], cvb_ref[...]))
              * _silu(gate_ref[0].astype(F32)))
        ya = ya.astype(BF16)
        yb = yb.astype(BF16)
    else:
        ya_ref, yb_ref, x_ref, g_ref, lng_ref, lnb_ref, w_ref, o_ref = refs
        ya = ya_ref[0]
        yb = yb_ref[0]
    half = w_ref.shape[0] // 2
    y = (jnp.dot(ya, w_ref[pl.ds(0, half), :], preferred_element_type=F32)
         + jnp.dot(yb, w_ref[pl.ds(half, half), :], preferred_element_type=F32))
    r = DEEPNORM_ALPHA * x_ref[0] + g_ref[0] * y
    o_ref[0] = _layer_norm(r, lng_ref[...], lnb_ref[...])


def _outproj(mix_args, mix_specs, x, g, ln_g, ln_b, w_out, even, tm):
    bsz, seq, d = x.shape
    vec = pl.BlockSpec((1, d), lambda b, i: (0, 0))
    in_specs = list(mix_specs) + [
        pl.BlockSpec((1, tm, d), lambda b, i: (b, i, 0)),
        pl.BlockSpec((1, 1, d), lambda b, i: (b, 0, 0)),
        vec, vec,
        pl.BlockSpec(w_out.shape, lambda b, i: (0, 0))]
    return pl.pallas_call(
        functools.partial(_outproj_kernel, even=even),
        out_shape=jax.ShapeDtypeStruct((bsz, seq, d), F32),
        grid=(bsz, seq // tm),
        in_specs=in_specs,
        out_specs=pl.BlockSpec((1, tm, d), lambda b, i: (b, i, 0)),
        compiler_params=_params("arbitrary", "arbitrary"),
        name="outproj_even" if even else "outproj_odd",
    )(*mix_args, x, g, ln_g.reshape(1, d), ln_b.reshape(1, d), w_out)


def _outproj_even(y_ssd, cv, proj, norm_w, cv_ln_g, cv_ln_b, x, g, ln_g, ln_b, w_out):
    seq = x.shape[1]
    tm = min(seq, 512)
    ch = SSD_INNER
    blk = lambda col: pl.BlockSpec((1, tm, ch), lambda b, i: (b, i, col // ch))
    vec = pl.BlockSpec((1, ch), lambda b, i: (0, 0))
    specs = [blk(0), blk(EV_Z), blk(0), blk(EV_GATE), vec, vec, vec]
    args = (y_ssd, proj, cv, proj, norm_w.reshape(1, ch), cv_ln_g.reshape(1, ch), cv_ln_b.reshape(1, ch))
    return _outproj(args, specs, x, g, ln_g, ln_b, w_out, True, tm)


def _outproj_odd(yc, yd, x, g, ln_g, ln_b, w_out):
    seq = x.shape[1]
    tm = min(seq, 512)
    ch = MLP_CH
    blk = pl.BlockSpec((1, tm, ch), lambda b, i: (b, i, 0))
    return _outproj((yc, yd), [blk, blk], x, g, ln_g, ln_b, w_out, False, tm)


def _rope_tables(seq):
    t = jnp.arange(seq)
    row = (t // GRID_W).astype(F32)
    col = (t % GRID_W).astype(F32)
    n_freq = ATT_HEAD_DIM // 4
    inv = ROPE_BASE ** (-jnp.arange(n_freq, dtype=F32) / n_freq)
    ang = jnp.concatenate([row[:, None] * inv, col[:, None] * inv], -1)
    cos, sin = jnp.cos(ang), jnp.sin(ang)
    reps = LANE // ATT_HEAD_DIM
    return (jnp.tile(jnp.concatenate([cos, cos], -1), (1, reps)),
            jnp.tile(jnp.concatenate([-sin, sin], -1), (1, reps)))


def _even_weights(w_in):
    o_z, o_xbc, o_dt, o_glu, o_gate = 0, 1024, 2560, 2592, 4640
    main = jnp.concatenate([w_in[:, o_z:o_xbc], w_in[:, o_glu:o_gate], w_in[:, o_gate:],
                            w_in[:, o_xbc:o_dt]], axis=1)
    w_dt = w_in[:, o_dt:o_glu][:, jnp.array(_group_dt_order(), jnp.int32)]
    return main.astype(BF16), w_dt.astype(BF16)


def _odd_weights(w_in):
    o_q, o_k, o_gd = 3072, 4096, 4608
    perm = jnp.array(_att_col_perm(), jnp.int32)
    return jnp.concatenate([w_in[:, :o_q], w_in[:, o_q:o_k][:, perm], w_in[:, o_gd:][:, perm],
                            w_in[:, o_k:o_gd]], axis=1).astype(BF16)


def _odd_out_weights(w_out):
    perm = jnp.array(_att_col_perm(), jnp.int32)
    return jnp.concatenate([w_out[:MLP_CH], w_out[MLP_CH:][perm]], axis=0).astype(BF16)


def kernel(x, c, ctx, c_ctx, mod_w, mod_b, ln_g, ln_b, ev_w_in, ev_ssd_conv_w, ev_ssd_conv_b, ev_dt_bias, ev_a_log, ev_d_skip, ev_ssd_norm, ev_cv_w, ev_cv_b, ev_cv_ln_g, ev_cv_ln_b, ev_w_out, od_w_in, od_mlp_ln_g, od_mlp_ln_b, od_ws, od_bs, od_sink, od_w_out):
    bsz, seq, d = x.shape
    cos, sin_signed = _rope_tables(seq)
    rows = -(-(bsz + 1) // SUBLANE) * SUBLANE
    cond = jnp.concatenate([c, c_ctx[None, :], jnp.zeros((rows - bsz - 1, d), F32)], axis=0)
    mod = _modulation(cond, mod_w, mod_b)
    zero_state = jnp.zeros((bsz, 2, SSD_GROUPS, SSD_STATE, SSD_GROUP_CH), F32)

    for layer in range(DEPTH):
        last = layer == DEPTH - 1
        i = layer // 2
        m = mod[layer]
        sh_x, sc_x, g_x = (m[:bsz, None, k * d:(k + 1) * d] for k in range(3))
        sh_c, sc_c, g_c = (jnp.broadcast_to(m[bsz:bsz + 1, None, k * d:(k + 1) * d], (bsz, 1, d)) for k in range(3))
        if layer % 2 == 0:
            w_main, w_dt = _even_weights(ev_w_in[i])
            w_out = ev_w_out[i].astype(BF16)
            ssd_args = (ev_ssd_conv_w[i], ev_ssd_conv_b[i], ev_dt_bias[i], ev_a_log[i], ev_d_skip[i])
            p_c, dt_c, dtt_c = _inproj(ctx, sc_c, sh_c, w_main, w_dt)
            y_c, h_c = _ssd(p_c, dt_c, dtt_c, zero_state, *ssd_args)
            p_x, dt_x, dtt_x = _inproj(x, sc_x, sh_x, w_main, w_dt)
            y_x, _ = _ssd(p_x, dt_x, dtt_x, h_c, *ssd_args)
            cv_x = _cvconv(p_x, ev_cv_w[i], ev_cv_b[i])
            x = _outproj_even(y_x, cv_x, p_x, ev_ssd_norm[i], ev_cv_ln_g[i], ev_cv_ln_b[i],
                              x, g_x, ln_g[layer], ln_b[layer], w_out)
            if not last:
                cv_c = _cvconv(p_c, ev_cv_w[i], ev_cv_b[i])
                ctx = _outproj_even(y_c, cv_c, p_c, ev_ssd_norm[i], ev_cv_ln_g[i], ev_cv_ln_b[i],
                                    ctx, g_c, ln_g[layer], ln_b[layer], w_out)
        else:
            w_main = _odd_weights(od_w_in[i])
            w_out = _odd_out_weights(od_w_out[i])
            p_c = _inproj(ctx, sc_c, sh_c, w_main)[0]
            p_x = _inproj(x, sc_x, sh_x, w_main)[0]
            yd_x = _attention(p_x, p_c, od_sink[i], cos, sin_signed, True)
            yc_x = _gmlp(p_x, od_mlp_ln_g[i], od_mlp_ln_b[i], od_ws[i], od_bs[i])
            x = _outproj_odd(yc_x, yd_x, x, g_x, ln_g[layer], ln_b[layer], w_out)
            if not last:
                yd_c = _attention(p_c, p_c, od_sink[i], cos, sin_signed, False)
                yc_c = _gmlp(p_c, od_mlp_ln_g[i], od_mlp_ln_b[i], od_ws[i], od_bs[i])
                ctx = _outproj_odd(yc_c, yd_c, ctx, g_c, ln_g[layer], ln_b[layer], w_out)
    return x
```

```python
import functools
import math

import jax
import jax.numpy as jnp
from jax import lax
from jax.experimental import pallas as pl
from jax.experimental.pallas import tpu as pltpu

F32 = jnp.float32
BF16 = jnp.bfloat16

D_MODEL = 1024
DEPTH = 4
GRID_W = 64

SSD_HEADS = 16
SSD_HEAD_DIM = 64
SSD_INNER = SSD_HEADS * SSD_HEAD_DIM
SSD_GROUPS = 2
SSD_HPG = SSD_HEADS // SSD_GROUPS
SSD_STATE = 128
SSD_CHUNK = 128
SSD_CONV = 5
SSD_GROUP_CH = SSD_HPG * SSD_HEAD_DIM
CONV_CH = 1024
CONV_WIDTH = 31
MLP_CH = 1024
MLP_GROUPS = 8
MLP_GROUP_CH = MLP_CH // MLP_GROUPS
MLP_CHUNK = 128
ATT_HEADS = 16
ATT_KV_HEADS = 4
ATT_REP = ATT_HEADS // ATT_KV_HEADS
ATT_HEAD_DIM = 64
ATT_WINDOW = 128
ATT_BLOCK = 128
ATT_SCALE = ATT_HEAD_DIM ** -0.5
LOG2E = math.log2(math.e)
ROPE_BASE = 10000.0
ATT_KV_CH = ATT_KV_HEADS * ATT_HEAD_DIM
ATT_ONES = 16

DEEPNORM_ALPHA = (2 * DEPTH) ** 0.25
LN_EPS = 1e-5

PROJ_N = 5632
LANE = 128
SUBLANE = 8
VMEM_LIMIT = 56 * 1024 * 1024

EV_Z, EV_VAL, EV_GT, EV_GATE, EV_XS, EV_B, EV_C = 0, 1024, 2048, 3072, 4096, 5120, 5376
OD_U, OD_V, OD_GC, OD_Q, OD_GD, OD_K, OD_VA = 0, 1024, 2048, 3072, 4096, 5120, 5376


def _silu(t):
    return t * jax.nn.sigmoid(t)


def _gelu_tanh(t):
    c = math.sqrt(2.0 / math.pi)
    return t * (0.5 * (1.0 + jnp.tanh(c * (t + 0.044715 * (t * t * t)))))


def _softplus(t):
    return jnp.maximum(t, 0.0) + jnp.log1p(jnp.exp(-jnp.abs(t)))


def _layer_norm(t, g, b):
    mu = jnp.mean(t, -1, keepdims=True)
    d = t - mu
    var = jnp.mean(d * d, -1, keepdims=True)
    return d * lax.rsqrt(var + LN_EPS) * g + b


def _params(*sem):
    return pltpu.CompilerParams(dimension_semantics=sem, vmem_limit_bytes=VMEM_LIMIT)


def _mod_kernel(c_ref, w_ref, b_ref, o_ref):
    s = _silu(c_ref[...]).astype(BF16)
    o_ref[0] = jnp.dot(s, w_ref[0].astype(BF16), preferred_element_type=F32) + b_ref[0]


def _modulation(cond, mod_w, mod_b):
    rows = cond.shape[0]
    d = D_MODEL
    return pl.pallas_call(
        _mod_kernel,
        out_shape=jax.ShapeDtypeStruct((DEPTH, rows, 3 * d), F32),
        grid=(DEPTH, 3),
        in_specs=[pl.BlockSpec((rows, d), lambda l, j: (0, 0)),
                  pl.BlockSpec((1, d, d), lambda l, j: (l, 0, j)),
                  pl.BlockSpec((1, 1, d), lambda l, j: (l, 0, j))],
        out_specs=pl.BlockSpec((1, rows, d), lambda l, j: (l, 0, j)),
        compiler_params=_params("arbitrary", "arbitrary"),
        name="modulation",
    )(cond, mod_w, mod_b.reshape(DEPTH, 1, 3 * d))


INPROJ_TM = 512
INPROJ_TN = 512


def _inproj_kernel(x_ref, sc_ref, sh_ref, w_ref, *rest, with_dt):
    if with_dt:
        wdt_ref, wdtt_ref, o_ref, dt_ref, dtt_ref, h_ref = rest
    else:
        o_ref, h_ref = rest
    h_ref[...] = (x_ref[0] * (1.0 + sc_ref[0]) + sh_ref[0]).astype(BF16)
    if with_dt:
        dt = jnp.dot(h_ref[...], wdt_ref[...], preferred_element_type=F32)
        dtt = lax.dot_general(wdtt_ref[...], h_ref[...], (((1,), (1,)), ((), ())),
                              preferred_element_type=F32)
        per = dt.shape[1] // SSD_GROUPS
        for gi in range(SSD_GROUPS):
            dt_ref[0, gi] = dt[:, gi * per:(gi + 1) * per]
            dtt_ref[0, gi] = dtt[gi * per:(gi + 1) * per, :]
    for j in range(w_ref.shape[1] // INPROJ_TN):
        cols = pl.ds(j * INPROJ_TN, INPROJ_TN)
        o_ref[0, :, cols] = jnp.dot(h_ref[...], w_ref[:, cols], preferred_element_type=F32).astype(BF16)


def _inproj(x, sc, sh, w, w_dt=None):
    bsz, seq, d = x.shape
    n = w.shape[1]
    tm = min(seq, INPROJ_TM)
    with_dt = w_dt is not None
    resident = dict(pipeline_mode=pl.Buffered(1))
    in_specs = [pl.BlockSpec((1, tm, d), lambda b, i: (b, i, 0)),
                pl.BlockSpec((1, 1, d), lambda b, i: (b, 0, 0)),
                pl.BlockSpec((1, 1, d), lambda b, i: (b, 0, 0)),
                pl.BlockSpec((d, n), lambda b, i: (0, 0), **resident)]
    out_shape = [jax.ShapeDtypeStruct((bsz, seq, n), BF16)]
    out_specs = [pl.BlockSpec((1, tm, n), lambda b, i: (b, i, 0))]
    args = [x, sc, sh, w]
    if with_dt:
        ndt = w_dt.shape[1]
        per = ndt // SSD_GROUPS
        in_specs += [pl.BlockSpec((d, ndt), lambda b, i: (0, 0)),
                     pl.BlockSpec((ndt, d), lambda b, i: (0, 0))]
        out_shape += [jax.ShapeDtypeStruct((bsz, SSD_GROUPS, seq, per), F32),
                      jax.ShapeDtypeStruct((bsz, SSD_GROUPS, per, seq), F32)]
        out_specs += [pl.BlockSpec((1, SSD_GROUPS, tm, per), lambda b, i: (b, 0, i, 0)),
                      pl.BlockSpec((1, SSD_GROUPS, per, tm), lambda b, i: (b, 0, 0, i))]
        args += [w_dt, w_dt.T]
    return pl.pallas_call(
        functools.partial(_inproj_kernel, with_dt=with_dt),
        out_shape=out_shape,
        grid=(bsz, seq // tm),
        in_specs=in_specs,
        out_specs=out_specs,
        scratch_shapes=[pltpu.VMEM((tm, d), BF16)],
        compiler_params=_params("arbitrary", "arbitrary"),
        name="inproj",
    )(*args)


def _inproj_ctx(ctx, sc, sh, w, w_dt=None):
    bsz, clen, d = ctx.shape
    outs = _inproj(ctx.reshape(1, bsz * clen, d), sc[:1], sh[:1], w, w_dt)
    proj = outs[0].reshape(bsz, clen, -1)
    if w_dt is None:
        return (proj,)
    per = outs[1].shape[-1]
    dt = outs[1].reshape(SSD_GROUPS, bsz, clen, per).transpose(1, 0, 2, 3)
    dtt = outs[2].reshape(SSD_GROUPS, per, bsz, clen).transpose(2, 0, 1, 3)
    return proj, dt, dtt


def _conv_silu(src_ref, w_ref, b_ref, pad_ref, dst_ref, seq):
    ch = src_ref.shape[-1]
    halo = SUBLANE
    pad_ref[pl.ds(0, halo), :] = jnp.zeros((halo, ch), F32)
    pad_ref[pl.ds(seq + halo, halo), :] = jnp.zeros((halo, ch), F32)
    pad_ref[pl.ds(halo, seq), :] = src_ref[0].astype(F32)
    w = w_ref[...]
    bias = b_ref[...]
    first = halo - SSD_CONV // 2

    def body(t, carry):
        base = pl.multiple_of(t * SSD_CHUNK, SSD_CHUNK)
        win = pad_ref[pl.ds(base, SSD_CHUNK + 2 * halo), :]
        acc = bias
        for k in range(SSD_CONV):
            acc = acc + w[k:k + 1, :] * win[first + k:first + k + SSD_CHUNK, :]
        dst_ref[pl.ds(base, SSD_CHUNK), :] = _silu(acc).astype(dst_ref.dtype)
        return carry

    lax.fori_loop(0, seq // SSD_CHUNK, body, 0)


def _pair_expand(vals, col0):
    q = vals.shape[0]
    lane = lax.broadcasted_iota(jnp.int32, (q, LANE), 1)
    parts = []
    for k in range(SSD_HPG // 2):
        a = jnp.broadcast_to(vals[:, col0 + 2 * k:col0 + 2 * k + 1], (q, LANE))
        b = jnp.broadcast_to(vals[:, col0 + 2 * k + 1:col0 + 2 * k + 2], (q, LANE))
        parts.append(jnp.where(lane < SSD_HEAD_DIM, a, b))
    return jnp.concatenate(parts, axis=-1)


def _split_dot(lhs, rhs):
    if lhs.dtype == F32:
        hi = lhs.astype(BF16)
        lo = (lhs - hi.astype(F32)).astype(BF16)
        return (jnp.dot(hi, rhs, preferred_element_type=F32)
                + jnp.dot(lo, rhs, preferred_element_type=F32))
    hi = rhs.astype(BF16)
    lo = (rhs - hi.astype(F32)).astype(BF16)
    return (jnp.dot(lhs, hi, preferred_element_type=F32)
            + jnp.dot(lhs, lo, preferred_element_type=F32))


def _ssd_kernel(xs_ref, bm_ref, cm_ref, dt_ref, dtt_ref, h0_ref,
                wx_ref, wb_ref, wc_ref, bx_ref, bb_ref, bc_ref,
                dtb_row_ref, dtb_col_ref, alog_row_ref, alog_col_ref, dskip_ref,
                y_ref, hout_ref,
                padx_ref, padn_ref, xc_ref, bcs_ref, ccs_ref, yfwd_ref, ybwd_ref, state_ref, *, seq):
    q = SSD_CHUNK
    n_chunks = seq // q
    nh = 2 * SSD_HPG

    _conv_silu(xs_ref, wx_ref, bx_ref, padx_ref, xc_ref, seq)
    _conv_silu(bm_ref, wb_ref, bb_ref, padn_ref, bcs_ref, seq)
    _conv_silu(cm_ref, wc_ref, bc_ref, padn_ref, ccs_ref, seq)

    a_row = -jnp.exp(alog_row_ref[0])
    a_col = -jnp.exp(alog_col_ref[0])
    dtb_row = dtb_row_ref[0]
    dtb_col = dtb_col_ref[0]
    row_i = lax.broadcasted_iota(jnp.int32, (q, q), 0)
    col_i = lax.broadcasted_iota(jnp.int32, (q, q), 1)
    lower = row_i >= col_i
    upper = row_i <= col_i
    tri_lo = jnp.where(lower, 1.0, 0.0).astype(BF16)
    tri_up = jnp.where(upper, 1.0, 0.0).astype(BF16)
    lane = lax.broadcasted_iota(jnp.int32, (q, LANE), 1)
    head_lo = lane < SSD_HEAD_DIM

    def chunk(t, direction):
        base = pl.multiple_of(t * q, q)
        rows = pl.ds(base, q)
        dt = _softplus(dt_ref[0, 0, rows, :] + dtb_row)
        dtt = _softplus(dtt_ref[0, 0, :, rows] + dtb_col)
        da = dt * a_row
        dat = dtt * a_col
        if direction == 0:
            cum = _split_dot(tri_lo, da)
            cumt = _split_dot(dat, tri_up)
            mask = lower
        else:
            cum = _split_dot(tri_up, da)
            cumt = _split_dot(dat, tri_lo)
            mask = upper
        edge = q - 1 if direction == 0 else 0
        total = cum[edge:edge + 1, :]
        wt = dtt * jnp.exp(cumt[:, edge:edge + 1] - cumt)
        col0 = direction * SSD_HPG
        xb = xc_ref[rows, :]
        bc = bcs_ref[rows, :]
        cc = ccs_ref[rows, :]
        cb = lax.dot_general(cc, bc, (((1,), (1,)), ((), ())), preferred_element_type=F32)
        ccf = cc.astype(F32)
        bctf = bc.astype(F32).T
        st = state_ref[direction]
        stb = st.astype(BF16)
        zero = jnp.zeros((q, LANE), BF16)
        ys, news = [], []
        for k in range(SSD_HPG // 2):
            xp = xb[:, k * LANE:(k + 1) * LANE]
            sp = stb[:, k * LANE:(k + 1) * LANE]
            rhs_x = jnp.concatenate([jnp.where(head_lo, xp, zero), jnp.where(head_lo, zero, xp)], axis=0)
            rhs_s = jnp.concatenate([jnp.where(head_lo, sp, zero), jnp.where(head_lo, zero, sp)], axis=0)
            ms, cs, ws = [], [], []
            for half in range(2):
                c = col0 + 2 * k + half
                bcol = jnp.broadcast_to(cum[:, c:c + 1], (q, q))
                seg = bcol - jnp.broadcast_to(cumt[c:c + 1, :], (q, q))
                dec = jnp.exp(jnp.where(mask, seg, -jnp.inf))
                ms.append((cb * dec * jnp.broadcast_to(dtt[c:c + 1, :], (q, q))).astype(BF16))
                cs.append((ccf * jnp.exp(bcol)).astype(BF16))
                ws.append((bctf * jnp.broadcast_to(wt[c:c + 1, :], (q, q))).astype(BF16))
            ys.append(jnp.dot(jnp.concatenate(ms, axis=1), rhs_x, preferred_element_type=F32)
                      + jnp.dot(jnp.concatenate(cs, axis=1), rhs_s, preferred_element_type=F32))
            news.append(jnp.dot(jnp.concatenate(ws, axis=1), rhs_x, preferred_element_type=F32))
        state_ref[direction] = st * _pair_expand(jnp.exp(total), col0) + jnp.concatenate(news, axis=-1)
        return rows, jnp.concatenate(ys, axis=-1)

    state_ref[...] = h0_ref[0, :, 0]

    def both(i, carry):
        rows, y = chunk(i, 0)
        yfwd_ref[rows, :] = y
        rows, y = chunk(n_chunks - 1 - i, 1)
        ybwd_ref[rows, :] = y
        return carry

    lax.fori_loop(0, n_chunks, both, 0)
    hout_ref[0, :, 0] = state_ref[...]

    dskip = dskip_ref[...]

    def finish(t, carry):
        rows = pl.ds(pl.multiple_of(t * q, q), q)
        y = yfwd_ref[rows, :] + ybwd_ref[rows, :] + xc_ref[rows, :].astype(F32) * dskip
        y_ref[0, rows, :] = y.astype(y_ref.dtype)
        return carry

    lax.fori_loop(0, n_chunks, finish, 0)


def _group_dt_order():
    order = []
    for g in range(SSD_GROUPS):
        for direction in range(2):
            order += [direction * SSD_HEADS + g * SSD_HPG + r for r in range(SSD_HPG)]
    return order


def _ssd(proj, dt, dtt, h0, conv_w, conv_b, dt_bias, a_log, d_skip):
    bsz, seq, _ = proj.shape
    gc = SSD_GROUP_CH
    ns = SSD_STATE
    nh = 2 * SSD_HPG
    xs_blk, b_blk, c_blk = EV_XS // gc, EV_B // ns, EV_C // ns
    cw_b0, cw_c0 = SSD_INNER // ns, (SSD_INNER + SSD_GROUPS * ns) // ns
    conv_b2 = conv_b.reshape(1, -1)
    d_skip_x = jnp.repeat(d_skip, SSD_HEAD_DIM).reshape(1, SSD_INNER)
    order = jnp.array(_group_dt_order(), jnp.int32)
    dt_bias_g = dt_bias.reshape(-1)[order].reshape(SSD_GROUPS, nh)
    a_log_g = a_log.reshape(-1)[order].reshape(SSD_GROUPS, nh)
    in_specs = [
        pl.BlockSpec((1, seq, gc), lambda b, g: (b, 0, xs_blk + g)),
        pl.BlockSpec((1, seq, ns), lambda b, g: (b, 0, b_blk + g)),
        pl.BlockSpec((1, seq, ns), lambda b, g: (b, 0, c_blk + g)),
        pl.BlockSpec((1, 1, seq, nh), lambda b, g: (b, g, 0, 0)),
        pl.BlockSpec((1, 1, nh, seq), lambda b, g: (b, g, 0, 0)),
        pl.BlockSpec((1, 2, 1, ns, gc), lambda b, g: (b, 0, g, 0, 0)),
        pl.BlockSpec((SSD_CONV, gc), lambda b, g: (0, g)),
        pl.BlockSpec((SSD_CONV, ns), lambda b, g: (0, cw_b0 + g)),
        pl.BlockSpec((SSD_CONV, ns), lambda b, g: (0, cw_c0 + g)),
        pl.BlockSpec((1, gc), lambda b, g: (0, g)),
        pl.BlockSpec((1, ns), lambda b, g: (0, cw_b0 + g)),
        pl.BlockSpec((1, ns), lambda b, g: (0, cw_c0 + g)),
        pl.BlockSpec((1, 1, nh), lambda b, g: (g, 0, 0)),
        pl.BlockSpec((1, nh, 1), lambda b, g: (g, 0, 0)),
        pl.BlockSpec((1, 1, nh), lambda b, g: (g, 0, 0)),
        pl.BlockSpec((1, nh, 1), lambda b, g: (g, 0, 0)),
        pl.BlockSpec((1, gc), lambda b, g: (0, g)),
    ]
    return pl.pallas_call(
        functools.partial(_ssd_kernel, seq=seq),
        out_shape=[jax.ShapeDtypeStruct((bsz, seq, SSD_INNER), BF16),
                   jax.ShapeDtypeStruct((bsz, 2, SSD_GROUPS, ns, gc), F32)],
        grid=(bsz, SSD_GROUPS),
        in_specs=in_specs,
        out_specs=[pl.BlockSpec((1, seq, gc), lambda b, g: (b, 0, g)),
                   pl.BlockSpec((1, 2, 1, ns, gc), lambda b, g: (b, 0, g, 0, 0))],
        scratch_shapes=[pltpu.VMEM((seq + 2 * SUBLANE, gc), F32),
                        pltpu.VMEM((seq + 2 * SUBLANE, ns), F32),
                        pltpu.VMEM((seq, gc), BF16),
                        pltpu.VMEM((seq, ns), BF16),
                        pltpu.VMEM((seq, ns), BF16),
                        pltpu.VMEM((seq, gc), F32),
                        pltpu.VMEM((seq, gc), F32),
                        pltpu.VMEM((2, ns, gc), F32)],
        compiler_params=_params("arbitrary", "arbitrary"),
        name="ssd",
    )(proj, proj, proj, dt, dtt, h0,
      conv_w, conv_w, conv_w, conv_b2, conv_b2, conv_b2,
      dt_bias_g.reshape(SSD_GROUPS, 1, nh), dt_bias_g.reshape(SSD_GROUPS, nh, 1),
      a_log_g.reshape(SSD_GROUPS, 1, nh), a_log_g.reshape(SSD_GROUPS, nh, 1), d_skip_x)


CV_ROWS = 256
CV_HALO = 16
CV_TC = 128
CV_SPAN = CV_ROWS + 2 * CV_HALO - SUBLANE


def _cvconv_kernel(val_ref, gt_ref, w_ref, b_ref, o_ref, pad_ref, sh_ref, *, seq):
    ch = val_ref.shape[-1]
    pad_ref[pl.ds(0, CV_HALO), :] = jnp.zeros((CV_HALO, ch), F32)
    pad_ref[pl.ds(seq + CV_HALO, CV_HALO), :] = jnp.zeros((CV_HALO, ch), F32)
    pad_ref[pl.ds(CV_HALO, seq), :] = val_ref[0].astype(F32) * jax.nn.sigmoid(gt_ref[0].astype(F32))
    w = w_ref[...]
    bias = b_ref[...]
    first = CV_HALO - CONV_WIDTH // 2
    rows = min(CV_ROWS, seq)
    span = rows + 2 * CV_HALO - SUBLANE

    def body(t, carry):
        base = pl.multiple_of(t * rows, rows)
        win = pad_ref[pl.ds(base, rows + 2 * CV_HALO), :]
        for s in range(SUBLANE):
            sh_ref[s, pl.ds(0, span), :] = win[s:s + span, :]
        acc = jnp.broadcast_to(bias, (rows, ch))
        for k in range(CONV_WIDTH):
            a, s = divmod(first + k, SUBLANE)
            acc = acc + w[k:k + 1, :] * sh_ref[s, pl.ds(a * SUBLANE, rows), :]
        o_ref[0, pl.ds(base, rows), :] = acc.astype(o_ref.dtype)
        return carry

    lax.fori_loop(0, seq // rows, body, 0)


def _cvconv(proj, cv_w, cv_b):
    bsz, seq, _ = proj.shape
    tc = CV_TC
    return pl.pallas_call(
        functools.partial(_cvconv_kernel, seq=seq),
        out_shape=jax.ShapeDtypeStruct((bsz, seq, CONV_CH), BF16),
        grid=(bsz, CONV_CH // tc),
        in_specs=[pl.BlockSpec((1, seq, tc), lambda b, j: (b, 0, EV_VAL // tc + j)),
                  pl.BlockSpec((1, seq, tc), lambda b, j: (b, 0, EV_GT // tc + j)),
                  pl.BlockSpec((CONV_WIDTH, tc), lambda b, j: (0, j)),
                  pl.BlockSpec((1, tc), lambda b, j: (0, j))],
        out_specs=pl.BlockSpec((1, seq, tc), lambda b, j: (b, 0, j)),
        scratch_shapes=[pltpu.VMEM((seq + 2 * CV_HALO, tc), F32),
                        pltpu.VMEM((SUBLANE, CV_SPAN, tc), F32)],
        compiler_params=_params("arbitrary", "arbitrary"),
        name="cvconv",
    )(proj, proj, cv_w, cv_b.reshape(1, CONV_CH))


def _gmlp_kernel(u_ref, v_ref, gc_ref, lng_ref, lnb_ref, ws_ref, bias_ref, o_ref, *, tm):
    lng = lng_ref[...]
    lnb = lnb_ref[...]
    bias = bias_ref[...]
    for c in range(tm // MLP_CHUNK):
        rows = pl.ds(c * MLP_CHUNK, MLP_CHUNK)
        vn = _layer_norm(_gelu_tanh(v_ref[0, rows, :].astype(F32)), lng, lnb).astype(BF16)
        mixed = jnp.concatenate(
            [jnp.dot(ws_ref[gi], vn[:, gi * MLP_GROUP_CH:(gi + 1) * MLP_GROUP_CH], preferred_element_type=F32)
             for gi in range(MLP_GROUPS)], axis=-1)
        u = _gelu_tanh(u_ref[0, rows, :].astype(F32))
        o_ref[0, rows, :] = (u * (mixed + bias) * _silu(gc_ref[0, rows, :].astype(F32))).astype(o_ref.dtype)


def _gmlp(proj, ln_g, ln_b, ws, bs):
    bsz, seq, _ = proj.shape
    tm = min(seq, 512)
    ch = MLP_CH
    bias = jnp.repeat(bs.T, MLP_GROUP_CH, axis=1)
    return pl.pallas_call(
        functools.partial(_gmlp_kernel, tm=tm),
        out_shape=jax.ShapeDtypeStruct((bsz, seq, ch), BF16),
        grid=(bsz, seq // tm),
        in_specs=[pl.BlockSpec((1, tm, ch), lambda b, i: (b, i, OD_U // ch)),
                  pl.BlockSpec((1, tm, ch), lambda b, i: (b, i, OD_V // ch)),
                  pl.BlockSpec((1, tm, ch), lambda b, i: (b, i, OD_GC // ch)),
                  pl.BlockSpec((1, ch), lambda b, i: (0, 0)),
                  pl.BlockSpec((1, ch), lambda b, i: (0, 0)),
                  pl.BlockSpec((MLP_GROUPS, MLP_CHUNK, MLP_CHUNK), lambda b, i: (0, 0, 0)),
                  pl.BlockSpec((MLP_CHUNK, ch), lambda b, i: (0, 0))],
        out_specs=pl.BlockSpec((1, tm, ch), lambda b, i: (b, i, 0)),
        compiler_params=_params("arbitrary", "arbitrary"),
        name="gmlp",
    )(proj, proj, proj, ln_g.reshape(1, ch), ln_b.reshape(1, ch), ws.astype(BF16), bias)


def _rope(t, cos, sin_signed):
    width = t.shape[1]
    reps = width // LANE
    half = ATT_HEAD_DIM // 2
    lane = lax.broadcasted_iota(jnp.int32, t.shape, 1)
    first_half = (lane % ATT_HEAD_DIM) < half
    swapped = jnp.where(first_half, pltpu.roll(t, width - half, 1), pltpu.roll(t, half, 1))
    c = jnp.concatenate([cos] * reps, axis=-1)
    s = jnp.concatenate([sin_signed] * reps, axis=-1)
    return t * c + swapped * s


def _att_head_order():
    order = []
    for c in range(ATT_HEADS // 2):
        j, r = divmod(c, ATT_REP)
        order += [(2 * j) * ATT_REP + r, (2 * j + 1) * ATT_REP + r]
    return order


def _att_col_perm():
    return [h * ATT_HEAD_DIM + d for h in _att_head_order() for d in range(ATT_HEAD_DIM)]


def _attn_kernel(*refs, seq, ctx_len, latent):
    if latent:
        (q_ref, gd_ref, k_ref, v_ref, kc_ref, vc_ref, sink_ref, cos_ref, sin_ref,
         o_ref, kpad_ref, vtp_ref, vct_ref) = refs
    else:
        q_ref, gd_ref, kc_ref, vc_ref, sink_ref, o_ref, vct_ref = refs
    i = pl.program_id(1)
    w = ATT_BLOCK
    nt = (((1,), (1,)), ((), ()))

    vrows = LANE + ATT_ONES

    @pl.when(i == 0)
    def _():
        vct = vc_ref[0].astype(F32).T.astype(BF16)
        for j in range(ATT_KV_CH // LANE):
            vct_ref[pl.ds(j * vrows, LANE), :] = vct[j * LANE:(j + 1) * LANE, :]
            vct_ref[pl.ds(j * vrows + LANE, ATT_ONES), :] = jnp.ones((ATT_ONES, ctx_len), BF16)
        if latent:
            kpad_ref[pl.ds(0, w), :] = jnp.zeros((w, ATT_KV_CH), BF16)
            kpad_ref[pl.ds(seq + w, w), :] = jnp.zeros((w, ATT_KV_CH), BF16)
            kpad_ref[pl.ds(w, seq), :] = _rope(k_ref[0].astype(F32), cos_ref[...], sin_ref[...]).astype(BF16)
            for j in range(ATT_KV_CH // LANE):
                vtp_ref[pl.ds(j * vrows, LANE), pl.ds(0, w)] = jnp.zeros((LANE, w), BF16)
                vtp_ref[pl.ds(j * vrows, LANE), pl.ds(seq + w, w)] = jnp.zeros((LANE, w), BF16)
                vtp_ref[pl.ds(j * vrows + LANE, ATT_ONES), :] = jnp.ones((ATT_ONES, seq + 2 * w), BF16)
            for t in range(seq // w):
                vt = v_ref[0, pl.ds(t * w, w), :].astype(F32).T.astype(BF16)
                for j in range(ATT_KV_CH // LANE):
                    vtp_ref[pl.ds(j * vrows, LANE), pl.ds((t + 1) * w, w)] = vt[j * LANE:(j + 1) * LANE, :]

    if latent:
        base = pl.multiple_of(i * w, w)
        q = _rope(q_ref[0].astype(F32), cos_ref[pl.ds(base, w), :], sin_ref[pl.ds(base, w), :])
        kwin = kpad_ref[pl.ds(base, 3 * w), :]
        qi = lax.broadcasted_iota(jnp.int32, (w, w), 1)
        u = lax.broadcasted_iota(jnp.int32, (w, w), 0)
        before = (u - w >= qi - ATT_WINDOW) & (base + u - w >= 0)
        after = (u + w <= qi + ATT_WINDOW) & (base + u + w < seq)
        bias_lo = jnp.where(before, 0.0, -jnp.inf)
        bias_hi = jnp.where(after, 0.0, -jnp.inf)
        bias_lo = jnp.concatenate([bias_lo, bias_lo], axis=1)
        bias_hi = jnp.concatenate([bias_hi, bias_hi], axis=1)
    else:
        q = q_ref[0].astype(F32)
    qs = (q * (ATT_SCALE * LOG2E)).astype(BF16)
    kc = kc_ref[0]
    lane = lax.broadcasted_iota(jnp.int32, (w, LANE), 1)
    low_lanes = lane < ATT_HEAD_DIM
    low_rows = lax.broadcasted_iota(jnp.int32, (LANE, w), 0) < ATT_HEAD_DIM
    zero = jnp.zeros((w, LANE), BF16)
    outs = []
    for c in range(ATT_HEADS // 2):
        j, r = divmod(c, ATT_REP)
        kv = slice(j * LANE, (j + 1) * LANE)
        qc = qs[:, c * LANE:(c + 1) * LANE]
        rhs = jnp.concatenate([jnp.where(low_lanes, qc, zero), jnp.where(low_lanes, zero, qc)], axis=0)
        s_ctx = lax.dot_general(kc[:, kv], rhs, nt, preferred_element_type=F32)
        snk = sink_ref[0, :, pl.ds(c * 2 * w, 2 * w)]
        m = jnp.maximum(jnp.max(s_ctx, axis=0, keepdims=True), snk)
        if latent:
            s_lat = lax.dot_general(kwin[:, kv], rhs, nt, preferred_element_type=F32)
            s_lo = s_lat[:w] + bias_lo
            s_mid = s_lat[w:2 * w]
            s_hi = s_lat[2 * w:] + bias_hi
            m = jnp.maximum(m, jnp.maximum(jnp.maximum(jnp.max(s_lo, axis=0, keepdims=True),
                                                       jnp.max(s_mid, axis=0, keepdims=True)),
                                           jnp.max(s_hi, axis=0, keepdims=True)))
        p_ctx = jnp.exp2(s_ctx - m).astype(BF16)
        acc = jnp.dot(vct_ref[pl.ds(j * vrows, vrows), :], p_ctx, preferred_element_type=F32)
        if latent:
            p_lat = jnp.concatenate([jnp.exp2(s_lo - m), jnp.exp2(s_mid - m), jnp.exp2(s_hi - m)],
                                    axis=0).astype(BF16)
            acc = acc + jnp.dot(vtp_ref[pl.ds(j * vrows, vrows), pl.ds(base, 3 * w)], p_lat,
                                preferred_element_type=F32)
        den = acc[LANE:LANE + 1, :] + jnp.exp2(snk - m)
        acc = acc[:LANE, :] * (1.0 / den)
        outs.append(jnp.where(low_rows, acc[:, :w], acc[:, w:]).T)
    gd = gd_ref[0].astype(F32)
    o_ref[0] = (jnp.concatenate(outs, axis=-1) * _silu(gd)).astype(o_ref.dtype)


def _attention(proj, proj_ctx, sink, cos, sin_signed, latent):
    bsz, seq, _ = proj.shape
    ctx_len = proj_ctx.shape[1]
    w = ATT_BLOCK
    qch = ATT_HEADS * ATT_HEAD_DIM
    kvc = ATT_KV_CH
    q_spec = pl.BlockSpec((1, w, qch), lambda b, i: (b, i, OD_Q // qch))
    gd_spec = pl.BlockSpec((1, w, qch), lambda b, i: (b, i, OD_GD // qch))
    kc_spec = pl.BlockSpec((1, ctx_len, kvc), lambda b, i: (b, 0, OD_K // kvc))
    vc_spec = pl.BlockSpec((1, ctx_len, kvc), lambda b, i: (b, 0, OD_VA // kvc))
    sink2 = jnp.repeat(sink[jnp.array(_att_head_order(), jnp.int32)] * LOG2E, w).reshape(1, 1, ATT_HEADS * w)
    sink_spec = pl.BlockSpec((1, 1, ATT_HEADS * w), lambda b, i: (0, 0, 0))
    vrows = (kvc // LANE) * (LANE + ATT_ONES)
    vct = pltpu.VMEM((vrows, ctx_len), BF16)
    if latent:
        in_specs = [q_spec, gd_spec,
                    pl.BlockSpec((1, seq, kvc), lambda b, i: (b, 0, OD_K // kvc)),
                    pl.BlockSpec((1, seq, kvc), lambda b, i: (b, 0, OD_VA // kvc)),
                    kc_spec, vc_spec, sink_spec,
                    pl.BlockSpec((seq, LANE), lambda b, i: (0, 0)),
                    pl.BlockSpec((seq, LANE), lambda b, i: (0, 0))]
        args = (proj, proj, proj, proj, proj_ctx, proj_ctx, sink2, cos, sin_signed)
        scratch = [pltpu.VMEM((seq + 2 * w, kvc), BF16), pltpu.VMEM((vrows, seq + 2 * w), BF16), vct]
    else:
        in_specs = [q_spec, gd_spec, kc_spec, vc_spec, sink_spec]
        args = (proj, proj, proj_ctx, proj_ctx, sink2)
        scratch = [vct]
    return pl.pallas_call(
        functools.partial(_attn_kernel, seq=seq, ctx_len=ctx_len, latent=latent),
        out_shape=jax.ShapeDtypeStruct((bsz, seq, qch), BF16),
        grid=(bsz, seq // w),
        in_specs=in_specs,
        out_specs=pl.BlockSpec((1, w, qch), lambda b, i: (b, i, 0)),
        scratch_shapes=scratch,
        compiler_params=_params("arbitrary", "arbitrary"),
        name="attention" if latent else "ctx_attention",
    )(*args)


def _outproj_kernel(*refs, even):
    if even:
        (y_ref, z_ref, cv_ref, gate_ref, nw_ref, cvg_ref, cvb_ref,
         x_ref, g_ref, lng_ref, lnb_ref, w_ref, o_ref) = refs
        t = y_ref[0].astype(F32) * _silu(z_ref[0].astype(F32))
        ya = t * lax.rsqrt(jnp.mean(t * t, -1, keepdims=True) + LN_EPS) * nw_ref[...]
        yb = (_silu(_layer_norm(cv_ref[0].astype(F32), cvg_ref[...], cvb_ref[...]))
              * _silu(gate_ref[0].astype(F32)))
        ya = ya.astype(BF16)
        yb = yb.astype(BF16)
    else:
        ya_ref, yb_ref, x_ref, g_ref, lng_ref, lnb_ref, w_ref, o_ref = refs
        ya = ya_ref[0]
        yb = yb_ref[0]
    half = w_ref.shape[0] // 2
    y = (jnp.dot(ya, w_ref[pl.ds(0, half), :], preferred_element_type=F32)
         + jnp.dot(yb, w_ref[pl.ds(half, half), :], preferred_element_type=F32))
    r = DEEPNORM_ALPHA * x_ref[0] + g_ref[0] * y
    o_ref[0] = _layer_norm(r, lng_ref[...], lnb_ref[...])


def _outproj(mix_args, mix_specs, x, g, ln_g, ln_b, w_out, even, tm):
    bsz, seq, d = x.shape
    vec = pl.BlockSpec((1, d), lambda b, i: (0, 0))
    in_specs = list(mix_specs) + [
        pl.BlockSpec((1, tm, d), lambda b, i: (b, i, 0)),
        pl.BlockSpec((1, 1, d), lambda b, i: (b, 0, 0)),
        vec, vec,
        pl.BlockSpec(w_out.shape, lambda b, i: (0, 0))]
    return pl.pallas_call(
        functools.partial(_outproj_kernel, even=even),
        out_shape=jax.ShapeDtypeStruct((bsz, seq, d), F32),
        grid=(bsz, seq // tm),
        in_specs=in_specs,
        out_specs=pl.BlockSpec((1, tm, d), lambda b, i: (b, i, 0)),
        compiler_params=_params("arbitrary", "arbitrary"),
        name="outproj_even" if even else "outproj_odd",
    )(*mix_args, x, g, ln_g.reshape(1, d), ln_b.reshape(1, d), w_out)


def _outproj_even(y_ssd, cv, proj, norm_w, cv_ln_g, cv_ln_b, x, g, ln_g, ln_b, w_out):
    seq = x.shape[1]
    tm = min(seq, 512)
    ch = SSD_INNER
    blk = lambda col: pl.BlockSpec((1, tm, ch), lambda b, i: (b, i, col // ch))
    vec = pl.BlockSpec((1, ch), lambda b, i: (0, 0))
    specs = [blk(0), blk(EV_Z), blk(0), blk(EV_GATE), vec, vec, vec]
    args = (y_ssd, proj, cv, proj, norm_w.reshape(1, ch), cv_ln_g.reshape(1, ch), cv_ln_b.reshape(1, ch))
    return _outproj(args, specs, x, g, ln_g, ln_b, w_out, True, tm)


def _outproj_odd(yc, yd, x, g, ln_g, ln_b, w_out):
    seq = x.shape[1]
    tm = min(seq, 512)
    ch = MLP_CH
    blk = pl.BlockSpec((1, tm, ch), lambda b, i: (b, i, 0))
    return _outproj((yc, yd), [blk, blk], x, g, ln_g, ln_b, w_out, False, tm)


def _rope_tables(seq):
    t = jnp.arange(seq)
    row = (t // GRID_W).astype(F32)
    col = (t % GRID_W).astype(F32)
    n_freq = ATT_HEAD_DIM // 4
    inv = ROPE_BASE ** (-jnp.arange(n_freq, dtype=F32) / n_freq)
    ang = jnp.concatenate([row[:, None] * inv, col[:, None] * inv], -1)
    cos, sin = jnp.cos(ang), jnp.sin(ang)
    reps = LANE // ATT_HEAD_DIM
    return (jnp.tile(jnp.concatenate([cos, cos], -1), (1, reps)),
            jnp.tile(jnp.concatenate([-sin, sin], -1), (1, reps)))


def _even_weights(w_in):
    o_z, o_xbc, o_dt, o_glu, o_gate = 0, 1024, 2560, 2592, 4640
    main = jnp.concatenate([w_in[:, o_z:o_xbc], w_in[:, o_glu:o_gate], w_in[:, o_gate:],
                            w_in[:, o_xbc:o_dt]], axis=1)
    w_dt = w_in[:, o_dt:o_glu][:, jnp.array(_group_dt_order(), jnp.int32)]
    return main.astype(BF16), w_dt.astype(BF16)


def _odd_weights(w_in):
    o_q, o_k, o_gd = 3072, 4096, 4608
    perm = jnp.array(_att_col_perm(), jnp.int32)
    return jnp.concatenate([w_in[:, :o_q], w_in[:, o_q:o_k][:, perm], w_in[:, o_gd:][:, perm],
                            w_in[:, o_k:o_gd]], axis=1).astype(BF16)


def _odd_out_weights(w_out):
    perm = jnp.array(_att_col_perm(), jnp.int32)
    return jnp.concatenate([w_out[:MLP_CH], w_out[MLP_CH:][perm]], axis=0).astype(BF16)


def kernel(x, c, ctx, c_ctx, mod_w, mod_b, ln_g, ln_b, ev_w_in, ev_ssd_conv_w, ev_ssd_conv_b, ev_dt_bias, ev_a_log, ev_d_skip, ev_ssd_norm, ev_cv_w, ev_cv_b, ev_cv_ln_g, ev_cv_ln_b, ev_w_out, od_w_in, od_mlp_ln_g, od_mlp_ln_b, od_ws, od_bs, od_sink, od_w_out):
    bsz, seq, d = x.shape
    cos, sin_signed = _rope_tables(seq)
    rows = -(-(bsz + 1) // SUBLANE) * SUBLANE
    cond = jnp.concatenate([c, c_ctx[None, :], jnp.zeros((rows - bsz - 1, d), F32)], axis=0)
    mod = _modulation(cond, mod_w, mod_b)
    zero_state = jnp.zeros((bsz, 2, SSD_GROUPS, SSD_STATE, SSD_GROUP_CH), F32)

    for layer in range(DEPTH):
        last = layer == DEPTH - 1
        i = layer // 2
        m = mod[layer]
        sh_x, sc_x, g_x = (m[:bsz, None, k * d:(k + 1) * d] for k in range(3))
        sh_c, sc_c, g_c = (jnp.broadcast_to(m[bsz:bsz + 1, None, k * d:(k + 1) * d], (bsz, 1, d)) for k in range(3))
        if layer % 2 == 0:
            w_main, w_dt = _even_weights(ev_w_in[i])
            w_out = ev_w_out[i].astype(BF16)
            ssd_args = (ev_ssd_conv_w[i], ev_ssd_conv_b[i], ev_dt_bias[i], ev_a_log[i], ev_d_skip[i])
            p_c, dt_c, dtt_c = _inproj_ctx(ctx, sc_c, sh_c, w_main, w_dt)
            y_c, h_c = _ssd(p_c, dt_c, dtt_c, zero_state, *ssd_args)
            p_x, dt_x, dtt_x = _inproj(x, sc_x, sh_x, w_main, w_dt)
            y_x, _ = _ssd(p_x, dt_x, dtt_x, h_c, *ssd_args)
            cv_x = _cvconv(p_x, ev_cv_w[i], ev_cv_b[i])
            x = _outproj_even(y_x, cv_x, p_x, ev_ssd_norm[i], ev_cv_ln_g[i], ev_cv_ln_b[i],
                              x, g_x, ln_g[layer], ln_b[layer], w_out)
            if not last:
                cv_c = _cvconv(p_c, ev_cv_w[i], ev_cv_b[i])
                ctx = _outproj_even(y_c, cv_c, p_c, ev_ssd_norm[i], ev_cv_ln_g[i], ev_cv_ln_b[i],
                                    ctx, g_c, ln_g[layer], ln_b[layer], w_out)
        else:
            w_main = _odd_weights(od_w_in[i])
            w_out = _odd_out_weights(od_w_out[i])
            p_c = _inproj_ctx(ctx, sc_c, sh_c, w_main)[0]
            p_x = _inproj(x, sc_x, sh_x, w_main)[0]
            yd_x = _attention(p_x, p_c, od_sink[i], cos, sin_signed, True)
            yc_x = _gmlp(p_x, od_mlp_ln_g[i], od_mlp_ln_b[i], od_ws[i], od_bs[i])
            x = _outproj_odd(yc_x, yd_x, x, g_x, ln_g[layer], ln_b[layer], w_out)
            if not last:
                yd_c = _attention(p_c, p_c, od_sink[i], cos, sin_signed, False)
                yc_c = _gmlp(p_c, od_mlp_ln_g[i], od_mlp_ln_b[i], od_ws[i], od_bs[i])
                ctx = _outproj_odd(yc_c, yd_c, ctx, g_c, ln_g[layer], ln_b[layer], w_out)
    return x
```

```python
import functools
import math

import jax
import jax.numpy as jnp
from jax import lax
from jax.experimental import pallas as pl
from jax.experimental.pallas import tpu as pltpu

F32 = jnp.float32
BF16 = jnp.bfloat16

D_MODEL = 1024
DEPTH = 4
GRID_W = 64

SSD_HEADS = 16
SSD_HEAD_DIM = 64
SSD_INNER = SSD_HEADS * SSD_HEAD_DIM
SSD_GROUPS = 2
SSD_HPG = SSD_HEADS // SSD_GROUPS
SSD_STATE = 128
SSD_CHUNK = 128
SSD_CONV = 5
SSD_GROUP_CH = SSD_HPG * SSD_HEAD_DIM
CONV_CH = 1024
CONV_WIDTH = 31
MLP_CH = 1024
MLP_GROUPS = 8
MLP_GROUP_CH = MLP_CH // MLP_GROUPS
MLP_CHUNK = 128
ATT_HEADS = 16
ATT_KV_HEADS = 4
ATT_REP = ATT_HEADS // ATT_KV_HEADS
ATT_HEAD_DIM = 64
ATT_WINDOW = 128
ATT_BLOCK = 128
ATT_SCALE = ATT_HEAD_DIM ** -0.5
LOG2E = math.log2(math.e)
ROPE_BASE = 10000.0
ATT_KV_CH = ATT_KV_HEADS * ATT_HEAD_DIM
ATT_ONES = 16

DEEPNORM_ALPHA = (2 * DEPTH) ** 0.25
LN_EPS = 1e-5

PROJ_N = 5632
LANE = 128
SUBLANE = 8
VMEM_LIMIT = 56 * 1024 * 1024

EV_Z, EV_VAL, EV_GT, EV_GATE, EV_XS, EV_B, EV_C = 0, 1024, 2048, 3072, 4096, 5120, 5376
EVO_Z, EVO_GLU, EVO_GATE, EVO_XS, EVO_B, EVO_C = 0, 1024, 2048, 3072, 4096, 4352
OD_U, OD_V, OD_GC, OD_Q, OD_GD, OD_K, OD_VA = 0, 1024, 2048, 3072, 4096, 5120, 5376


def _silu(t):
    return t * jax.nn.sigmoid(t)


def _gelu_tanh(t):
    c = math.sqrt(2.0 / math.pi)
    return t * (0.5 * (1.0 + jnp.tanh(c * (t + 0.044715 * (t * t * t)))))


def _softplus(t):
    return jnp.maximum(t, 0.0) + jnp.log1p(jnp.exp(-jnp.abs(t)))


def _layer_norm(t, g, b):
    mu = jnp.mean(t, -1, keepdims=True)
    d = t - mu
    var = jnp.mean(d * d, -1, keepdims=True)
    return d * lax.rsqrt(var + LN_EPS) * g + b


def _params(*sem):
    return pltpu.CompilerParams(dimension_semantics=sem, vmem_limit_bytes=VMEM_LIMIT)


def _mod_kernel(c_ref, w_ref, b_ref, o_ref):
    s = _silu(c_ref[...]).astype(BF16)
    o_ref[0] = jnp.dot(s, w_ref[0].astype(BF16), preferred_element_type=F32) + b_ref[0]


def _modulation(cond, mod_w, mod_b):
    rows = cond.shape[0]
    d = D_MODEL
    return pl.pallas_call(
        _mod_kernel,
        out_shape=jax.ShapeDtypeStruct((DEPTH, rows, 3 * d), F32),
        grid=(DEPTH, 3),
        in_specs=[pl.BlockSpec((rows, d), lambda l, j: (0, 0)),
                  pl.BlockSpec((1, d, d), lambda l, j: (l, 0, j)),
                  pl.BlockSpec((1, 1, d), lambda l, j: (l, 0, j))],
        out_specs=pl.BlockSpec((1, rows, d), lambda l, j: (l, 0, j)),
        compiler_params=_params("arbitrary", "arbitrary"),
        name="modulation",
    )(cond, mod_w, mod_b.reshape(DEPTH, 1, 3 * d))


INPROJ_TM = 512
INPROJ_TN = 512


def _tile_plan(even):
    t = lambda col: col // INPROJ_TN
    if even:
        plan = [("silu", t(EV_Z) + i) for i in range(2)]
        plan += [("glu", t(EV_VAL) + i, t(EV_GT) + i) for i in range(2)]
        plan += [("silu", t(EV_GATE) + i) for i in range(2)]
        plan += [("id", t(EV_XS) + i) for i in range(3)]
    else:
        plan = [("gelu", t(OD_U) + i) for i in range(4)]
        plan += [("silu", t(OD_GC) + i) for i in range(2)]
        plan += [("id", t(OD_Q) + i) for i in range(2)]
        plan += [("silu", t(OD_GD) + i) for i in range(2)]
        plan += [("id", t(OD_K))]
    return plan


def _inproj_kernel(x_ref, sc_ref, sh_ref, w_ref, *rest, even):
    if even:
        wdt_ref, wdtt_ref, o_ref, dt_ref, dtt_ref, h_ref = rest
    else:
        o_ref, h_ref = rest
    h_ref[...] = (x_ref[0] * (1.0 + sc_ref[0]) + sh_ref[0]).astype(BF16)
    if even:
        dt = jnp.dot(h_ref[...], wdt_ref[...], preferred_element_type=F32)
        dtt = lax.dot_general(wdtt_ref[...], h_ref[...], (((1,), (1,)), ((), ())),
                              preferred_element_type=F32)
        per = dt.shape[1] // SSD_GROUPS
        for gi in range(SSD_GROUPS):
            dt_ref[0, gi] = dt[:, gi * per:(gi + 1) * per]
            dtt_ref[0, gi] = dtt[gi * per:(gi + 1) * per, :]
    tile = lambda j: jnp.dot(h_ref[...], w_ref[:, pl.ds(j * INPROJ_TN, INPROJ_TN)], preferred_element_type=F32)
    for out_j, (kind, *src) in enumerate(_tile_plan(even)):
        r = tile(src[0])
        if kind == "silu":
            r = _silu(r)
        elif kind == "gelu":
            r = _gelu_tanh(r)
        elif kind == "glu":
            r = r * jax.nn.sigmoid(tile(src[1]))
        o_ref[0, :, pl.ds(out_j * INPROJ_TN, INPROJ_TN)] = r.astype(BF16)


def _inproj(x, sc, sh, w, w_dt=None):
    bsz, seq, d = x.shape
    n = w.shape[1]
    tm = min(seq, INPROJ_TM)
    even = w_dt is not None
    n_out = len(_tile_plan(even)) * INPROJ_TN
    resident = dict(pipeline_mode=pl.Buffered(1))
    in_specs = [pl.BlockSpec((1, tm, d), lambda b, i: (b, i, 0)),
                pl.BlockSpec((1, 1, d), lambda b, i: (b, 0, 0)),
                pl.BlockSpec((1, 1, d), lambda b, i: (b, 0, 0)),
                pl.BlockSpec((d, n), lambda b, i: (0, 0), **resident)]
    out_shape = [jax.ShapeDtypeStruct((bsz, seq, n_out), BF16)]
    out_specs = [pl.BlockSpec((1, tm, n_out), lambda b, i: (b, i, 0))]
    args = [x, sc, sh, w]
    if even:
        ndt = w_dt.shape[1]
        per = ndt // SSD_GROUPS
        in_specs += [pl.BlockSpec((d, ndt), lambda b, i: (0, 0)),
                     pl.BlockSpec((ndt, d), lambda b, i: (0, 0))]
        out_shape += [jax.ShapeDtypeStruct((bsz, SSD_GROUPS, seq, per), F32),
                      jax.ShapeDtypeStruct((bsz, SSD_GROUPS, per, seq), F32)]
        out_specs += [pl.BlockSpec((1, SSD_GROUPS, tm, per), lambda b, i: (b, 0, i, 0)),
                      pl.BlockSpec((1, SSD_GROUPS, per, tm), lambda b, i: (b, 0, 0, i))]
        args += [w_dt, w_dt.T]
    return pl.pallas_call(
        functools.partial(_inproj_kernel, even=even),
        out_shape=out_shape,
        grid=(bsz, seq // tm),
        in_specs=in_specs,
        out_specs=out_specs,
        scratch_shapes=[pltpu.VMEM((tm, d), BF16)],
        compiler_params=_params("arbitrary", "arbitrary"),
        name="inproj",
    )(*args)


def _inproj_ctx(ctx, sc, sh, w, w_dt=None):
    bsz, clen, d = ctx.shape
    outs = _inproj(ctx.reshape(1, bsz * clen, d), sc[:1], sh[:1], w, w_dt)
    proj = outs[0].reshape(bsz, clen, -1)
    if w_dt is None:
        return (proj,)
    per = outs[1].shape[-1]
    dt = outs[1].reshape(SSD_GROUPS, bsz, clen, per).transpose(1, 0, 2, 3)
    dtt = outs[2].reshape(SSD_GROUPS, per, bsz, clen).transpose(2, 0, 1, 3)
    return proj, dt, dtt


def _conv_silu(src_ref, w_ref, b_ref, pad_ref, dst_ref, seq):
    ch = src_ref.shape[-1]
    halo = SUBLANE
    pad_ref[pl.ds(0, halo), :] = jnp.zeros((halo, ch), F32)
    pad_ref[pl.ds(seq + halo, halo), :] = jnp.zeros((halo, ch), F32)
    pad_ref[pl.ds(halo, seq), :] = src_ref[0].astype(F32)
    w = w_ref[...]
    bias = b_ref[...]
    first = halo - SSD_CONV // 2

    def body(t, carry):
        base = pl.multiple_of(t * SSD_CHUNK, SSD_CHUNK)
        win = pad_ref[pl.ds(base, SSD_CHUNK + 2 * halo), :]
        acc = bias
        for k in range(SSD_CONV):
            acc = acc + w[k:k + 1, :] * win[first + k:first + k + SSD_CHUNK, :]
        dst_ref[pl.ds(base, SSD_CHUNK), :] = _silu(acc).astype(dst_ref.dtype)
        return carry

    lax.fori_loop(0, seq // SSD_CHUNK, body, 0)


def _pair_expand(vals, col0):
    q = vals.shape[0]
    lane = lax.broadcasted_iota(jnp.int32, (q, LANE), 1)
    parts = []
    for k in range(SSD_HPG // 2):
        a = jnp.broadcast_to(vals[:, col0 + 2 * k:col0 + 2 * k + 1], (q, LANE))
        b = jnp.broadcast_to(vals[:, col0 + 2 * k + 1:col0 + 2 * k + 2], (q, LANE))
        parts.append(jnp.where(lane < SSD_HEAD_DIM, a, b))
    return jnp.concatenate(parts, axis=-1)


def _split_dot(lhs, rhs):
    if lhs.dtype == F32:
        hi = lhs.astype(BF16)
        lo = (lhs - hi.astype(F32)).astype(BF16)
        return (jnp.dot(hi, rhs, preferred_element_type=F32)
                + jnp.dot(lo, rhs, preferred_element_type=F32))
    hi = rhs.astype(BF16)
    lo = (rhs - hi.astype(F32)).astype(BF16)
    return (jnp.dot(lhs, hi, preferred_element_type=F32)
            + jnp.dot(lhs, lo, preferred_element_type=F32))


def _ssd_kernel(xs_ref, bm_ref, cm_ref, dt_ref, dtt_ref, h0_ref,
                wx_ref, wb_ref, wc_ref, bx_ref, bb_ref, bc_ref,
                dtb_row_ref, dtb_col_ref, alog_row_ref, alog_col_ref, dskip_ref,
                y_ref, hout_ref,
                padx_ref, padn_ref, xc_ref, bcs_ref, ccs_ref, yfwd_ref, ybwd_ref, state_ref, *, seq):
    q = SSD_CHUNK
    n_chunks = seq // q
    nh = 2 * SSD_HPG

    _conv_silu(xs_ref, wx_ref, bx_ref, padx_ref, xc_ref, seq)
    _conv_silu(bm_ref, wb_ref, bb_ref, padn_ref, bcs_ref, seq)
    _conv_silu(cm_ref, wc_ref, bc_ref, padn_ref, ccs_ref, seq)

    a_row = -jnp.exp(alog_row_ref[0])
    a_col = -jnp.exp(alog_col_ref[0])
    dtb_row = dtb_row_ref[0]
    dtb_col = dtb_col_ref[0]
    row_i = lax.broadcasted_iota(jnp.int32, (q, q), 0)
    col_i = lax.broadcasted_iota(jnp.int32, (q, q), 1)
    lower = row_i >= col_i
    upper = row_i <= col_i
    tri_lo = jnp.where(lower, 1.0, 0.0).astype(BF16)
    tri_up = jnp.where(upper, 1.0, 0.0).astype(BF16)
    lane = lax.broadcasted_iota(jnp.int32, (q, LANE), 1)
    head_lo = lane < SSD_HEAD_DIM

    def prologue(t, direction):
        rows = pl.ds(pl.multiple_of(t * q, q), q)
        dt = _softplus(dt_ref[0, 0, rows, :] + dtb_row)
        dtt = _softplus(dtt_ref[0, 0, :, rows] + dtb_col)
        da = dt * a_row
        dat = dtt * a_col
        if direction == 0:
            cum = _split_dot(tri_lo, da)
            cumt = _split_dot(dat, tri_up)
        else:
            cum = _split_dot(tri_up, da)
            cumt = _split_dot(dat, tri_lo)
        edge = q - 1 if direction == 0 else 0
        wt = dtt * jnp.exp(cumt[:, edge:edge + 1] - cumt)
        bc = bcs_ref[rows, :]
        cc = ccs_ref[rows, :]
        cb = lax.dot_general(cc, bc, (((1,), (1,)), ((), ())), preferred_element_type=F32)
        bctf = bc.astype(F32).T
        return cum, cumt, dtt, wt, cb, bctf

    def chunk(t, direction, pro):
        cum, cumt, dtt, wt, cb, bctf = pro
        rows = pl.ds(pl.multiple_of(t * q, q), q)
        mask = lower if direction == 0 else upper
        edge = q - 1 if direction == 0 else 0
        total = cum[edge:edge + 1, :]
        col0 = direction * SSD_HPG
        xb = xc_ref[rows, :]
        ccf = ccs_ref[rows, :].astype(F32)
        st = state_ref[direction]
        stb = st.astype(BF16)
        zero = jnp.zeros((q, LANE), BF16)
        ys, news = [], []
        for k in range(SSD_HPG // 2):
            xp = xb[:, k * LANE:(k + 1) * LANE]
            sp = stb[:, k * LANE:(k + 1) * LANE]
            rhs_x = jnp.concatenate([jnp.where(head_lo, xp, zero), jnp.where(head_lo, zero, xp)], axis=0)
            rhs_s = jnp.concatenate([jnp.where(head_lo, sp, zero), jnp.where(head_lo, zero, sp)], axis=0)
            ms, cs, ws = [], [], []
            for half in range(2):
                c = col0 + 2 * k + half
                bcol = jnp.broadcast_to(cum[:, c:c + 1], (q, q))
                seg = bcol - jnp.broadcast_to(cumt[c:c + 1, :], (q, q))
                dec = jnp.exp(jnp.where(mask, seg, -jnp.inf))
                ms.append((cb * dec * jnp.broadcast_to(dtt[c:c + 1, :], (q, q))).astype(BF16))
                cs.append((ccf * jnp.exp(bcol)).astype(BF16))
                ws.append((bctf * jnp.broadcast_to(wt[c:c + 1, :], (q, q))).astype(BF16))
            ys.append(jnp.dot(jnp.concatenate(ms, axis=1), rhs_x, preferred_element_type=F32)
                      + jnp.dot(jnp.concatenate(cs, axis=1), rhs_s, preferred_element_type=F32))
            news.append(jnp.dot(jnp.concatenate(ws, axis=1), rhs_x, preferred_element_type=F32))
        state_ref[direction] = st * _pair_expand(jnp.exp(total), col0) + jnp.concatenate(news, axis=-1)
        return rows, jnp.concatenate(ys, axis=-1)

    state_ref[...] = h0_ref[0, :, 0]

    def both(i, carry):
        pro_f, pro_b = carry
        nxt_f = prologue(jnp.minimum(i + 1, n_chunks - 1), 0)
        nxt_b = prologue(jnp.maximum(n_chunks - 2 - i, 0), 1)
        rows, y = chunk(i, 0, pro_f)
        yfwd_ref[rows, :] = y
        rows, y = chunk(n_chunks - 1 - i, 1, pro_b)
        ybwd_ref[rows, :] = y
        return nxt_f, nxt_b

    lax.fori_loop(0, n_chunks, both, (prologue(0, 0), prologue(n_chunks - 1, 1)))
    hout_ref[0, :, 0] = state_ref[...]

    dskip = dskip_ref[...]

    def finish(t, carry):
        rows = pl.ds(pl.multiple_of(t * q, q), q)
        y = yfwd_ref[rows, :] + ybwd_ref[rows, :] + xc_ref[rows, :].astype(F32) * dskip
        y_ref[0, rows, :] = y.astype(y_ref.dtype)
        return carry

    lax.fori_loop(0, n_chunks, finish, 0)


def _group_dt_order():
    order = []
    for g in range(SSD_GROUPS):
        for direction in range(2):
            order += [direction * SSD_HEADS + g * SSD_HPG + r for r in range(SSD_HPG)]
    return order


def _ssd(proj, dt, dtt, h0, conv_w, conv_b, dt_bias, a_log, d_skip):
    bsz, seq, _ = proj.shape
    gc = SSD_GROUP_CH
    ns = SSD_STATE
    nh = 2 * SSD_HPG
    xs_blk, b_blk, c_blk = EVO_XS // gc, EVO_B // ns, EVO_C // ns
    cw_b0, cw_c0 = SSD_INNER // ns, (SSD_INNER + SSD_GROUPS * ns) // ns
    conv_b2 = conv_b.reshape(1, -1)
    d_skip_x = jnp.repeat(d_skip, SSD_HEAD_DIM).reshape(1, SSD_INNER)
    order = jnp.array(_group_dt_order(), jnp.int32)
    dt_bias_g = dt_bias.reshape(-1)[order].reshape(SSD_GROUPS, nh)
    a_log_g = a_log.reshape(-1)[order].reshape(SSD_GROUPS, nh)
    in_specs = [
        pl.BlockSpec((1, seq, gc), lambda b, g: (b, 0, xs_blk + g)),
        pl.BlockSpec((1, seq, ns), lambda b, g: (b, 0, b_blk + g)),
        pl.BlockSpec((1, seq, ns), lambda b, g: (b, 0, c_blk + g)),
        pl.BlockSpec((1, 1, seq, nh), lambda b, g: (b, g, 0, 0)),
        pl.BlockSpec((1, 1, nh, seq), lambda b, g: (b, g, 0, 0)),
        pl.BlockSpec((1, 2, 1, ns, gc), lambda b, g: (b, 0, g, 0, 0)),
        pl.BlockSpec((SSD_CONV, gc), lambda b, g: (0, g)),
        pl.BlockSpec((SSD_CONV, ns), lambda b, g: (0, cw_b0 + g)),
        pl.BlockSpec((SSD_CONV, ns), lambda b, g: (0, cw_c0 + g)),
        pl.BlockSpec((1, gc), lambda b, g: (0, g)),
        pl.BlockSpec((1, ns), lambda b, g: (0, cw_b0 + g)),
        pl.BlockSpec((1, ns), lambda b, g: (0, cw_c0 + g)),
        pl.BlockSpec((1, 1, nh), lambda b, g: (g, 0, 0)),
        pl.BlockSpec((1, nh, 1), lambda b, g: (g, 0, 0)),
        pl.BlockSpec((1, 1, nh), lambda b, g: (g, 0, 0)),
        pl.BlockSpec((1, nh, 1), lambda b, g: (g, 0, 0)),
        pl.BlockSpec((1, gc), lambda b, g: (0, g)),
    ]
    return pl.pallas_call(
        functools.partial(_ssd_kernel, seq=seq),
        out_shape=[jax.ShapeDtypeStruct((bsz, seq, SSD_INNER), BF16),
                   jax.ShapeDtypeStruct((bsz, 2, SSD_GROUPS, ns, gc), F32)],
        grid=(bsz, SSD_GROUPS),
        in_specs=in_specs,
        out_specs=[pl.BlockSpec((1, seq, gc), lambda b, g: (b, 0, g)),
                   pl.BlockSpec((1, 2, 1, ns, gc), lambda b, g: (b, 0, g, 0, 0))],
        scratch_shapes=[pltpu.VMEM((seq + 2 * SUBLANE, gc), F32),
                        pltpu.VMEM((seq + 2 * SUBLANE, ns), F32),
                        pltpu.VMEM((seq, gc), BF16),
                        pltpu.VMEM((seq, ns), BF16),
                        pltpu.VMEM((seq, ns), BF16),
                        pltpu.VMEM((seq, gc), F32),
                        pltpu.VMEM((seq, gc), F32),
                        pltpu.VMEM((2, ns, gc), F32)],
        compiler_params=_params("arbitrary", "arbitrary"),
        name="ssd",
    )(proj, proj, proj, dt, dtt, h0,
      conv_w, conv_w, conv_w, conv_b2, conv_b2, conv_b2,
      dt_bias_g.reshape(SSD_GROUPS, 1, nh), dt_bias_g.reshape(SSD_GROUPS, nh, 1),
      a_log_g.reshape(SSD_GROUPS, 1, nh), a_log_g.reshape(SSD_GROUPS, nh, 1), d_skip_x)


CV_ROWS = 256
CV_HALO = 16
CV_TC = 128
CV_SPAN = CV_ROWS + 2 * CV_HALO - SUBLANE


def _cvconv_kernel(glu_ref, w_ref, b_ref, o_ref, pad_ref, sh_ref, *, seq):
    ch = glu_ref.shape[-1]
    pad_ref[pl.ds(0, CV_HALO), :] = jnp.zeros((CV_HALO, ch), F32)
    pad_ref[pl.ds(seq + CV_HALO, CV_HALO), :] = jnp.zeros((CV_HALO, ch), F32)
    pad_ref[pl.ds(CV_HALO, seq), :] = glu_ref[0].astype(F32)
    w = w_ref[...]
    bias = b_ref[...]
    first = CV_HALO - CONV_WIDTH // 2
    rows = min(CV_ROWS, seq)
    span = rows + 2 * CV_HALO - SUBLANE

    def body(t, carry):
        base = pl.multiple_of(t * rows, rows)
        win = pad_ref[pl.ds(base, rows + 2 * CV_HALO), :]
        for s in range(SUBLANE):
            sh_ref[s, pl.ds(0, span), :] = win[s:s + span, :]
        acc = jnp.broadcast_to(bias, (rows, ch))
        for k in range(CONV_WIDTH):
            a, s = divmod(first + k, SUBLANE)
            acc = acc + w[k:k + 1, :] * sh_ref[s, pl.ds(a * SUBLANE, rows), :]
        o_ref[0, pl.ds(base, rows), :] = acc.astype(o_ref.dtype)
        return carry

    lax.fori_loop(0, seq // rows, body, 0)


def _cvconv(proj, cv_w, cv_b):
    bsz, seq, _ = proj.shape
    tc = CV_TC
    return pl.pallas_call(
        functools.partial(_cvconv_kernel, seq=seq),
        out_shape=jax.ShapeDtypeStruct((bsz, seq, CONV_CH), BF16),
        grid=(bsz, CONV_CH // tc),
        in_specs=[pl.BlockSpec((1, seq, tc), lambda b, j: (b, 0, EVO_GLU // tc + j)),
                  pl.BlockSpec((CONV_WIDTH, tc), lambda b, j: (0, j)),
                  pl.BlockSpec((1, tc), lambda b, j: (0, j))],
        out_specs=pl.BlockSpec((1, seq, tc), lambda b, j: (b, 0, j)),
        scratch_shapes=[pltpu.VMEM((seq + 2 * CV_HALO, tc), F32),
                        pltpu.VMEM((SUBLANE, CV_SPAN, tc), F32)],
        compiler_params=_params("arbitrary", "arbitrary"),
        name="cvconv",
    )(proj, cv_w, cv_b.reshape(1, CONV_CH))


def _gmlp_kernel(u_ref, v_ref, gc_ref, lng_ref, lnb_ref, ws_ref, bias_ref, o_ref, *, tm):
    lng = lng_ref[...]
    lnb = lnb_ref[...]
    bias = bias_ref[...]
    for c in range(tm // MLP_CHUNK):
        rows = pl.ds(c * MLP_CHUNK, MLP_CHUNK)
        vn = _layer_norm(v_ref[0, rows, :].astype(F32), lng, lnb).astype(BF16)
        mixed = jnp.concatenate(
            [jnp.dot(ws_ref[gi], vn[:, gi * MLP_GROUP_CH:(gi + 1) * MLP_GROUP_CH], preferred_element_type=F32)
             for gi in range(MLP_GROUPS)], axis=-1)
        u = u_ref[0, rows, :].astype(F32)
        o_ref[0, rows, :] = (u * (mixed + bias) * gc_ref[0, rows, :].astype(F32)).astype(o_ref.dtype)


def _gmlp(proj, ln_g, ln_b, ws, bs):
    bsz, seq, _ = proj.shape
    tm = min(seq, 512)
    ch = MLP_CH
    bias = jnp.repeat(bs.T, MLP_GROUP_CH, axis=1)
    return pl.pallas_call(
        functools.partial(_gmlp_kernel, tm=tm),
        out_shape=jax.ShapeDtypeStruct((bsz, seq, ch), BF16),
        grid=(bsz, seq // tm),
        in_specs=[pl.BlockSpec((1, tm, ch), lambda b, i: (b, i, OD_U // ch)),
                  pl.BlockSpec((1, tm, ch), lambda b, i: (b, i, OD_V // ch)),
                  pl.BlockSpec((1, tm, ch), lambda b, i: (b, i, OD_GC // ch)),
                  pl.BlockSpec((1, ch), lambda b, i: (0, 0)),
                  pl.BlockSpec((1, ch), lambda b, i: (0, 0)),
                  pl.BlockSpec((MLP_GROUPS, MLP_CHUNK, MLP_CHUNK), lambda b, i: (0, 0, 0)),
                  pl.BlockSpec((MLP_CHUNK, ch), lambda b, i: (0, 0))],
        out_specs=pl.BlockSpec((1, tm, ch), lambda b, i: (b, i, 0)),
        compiler_params=_params("arbitrary", "arbitrary"),
        name="gmlp",
    )(proj, proj, proj, ln_g.reshape(1, ch), ln_b.reshape(1, ch), ws.astype(BF16), bias)


def _rope(t, cos, sin_signed):
    width = t.shape[1]
    reps = width // LANE
    half = ATT_HEAD_DIM // 2
    lane = lax.broadcasted_iota(jnp.int32, t.shape, 1)
    first_half = (lane % ATT_HEAD_DIM) < half
    swapped = jnp.where(first_half, pltpu.roll(t, width - half, 1), pltpu.roll(t, half, 1))
    c = jnp.concatenate([cos] * reps, axis=-1)
    s = jnp.concatenate([sin_signed] * reps, axis=-1)
    return t * c + swapped * s


def _att_head_order():
    order = []
    for c in range(ATT_HEADS // 2):
        j, r = divmod(c, ATT_REP)
        order += [(2 * j) * ATT_REP + r, (2 * j + 1) * ATT_REP + r]
    return order


def _att_col_perm():
    return [h * ATT_HEAD_DIM + d for h in _att_head_order() for d in range(ATT_HEAD_DIM)]


def _attn_kernel(*refs, seq, ctx_len, latent):
    if latent:
        (q_ref, gd_ref, k_ref, v_ref, kc_ref, vc_ref, sink_ref, cos_ref, sin_ref,
         o_ref, kpad_ref, vtp_ref, vct_ref) = refs
    else:
        q_ref, gd_ref, kc_ref, vc_ref, sink_ref, o_ref, vct_ref = refs
    i = pl.program_id(1)
    w = ATT_BLOCK
    nt = (((1,), (1,)), ((), ()))

    vrows = LANE + ATT_ONES

    @pl.when(i == 0)
    def _():
        vct = vc_ref[0].astype(F32).T.astype(BF16)
        for j in range(ATT_KV_CH // LANE):
            vct_ref[pl.ds(j * vrows, LANE), :] = vct[j * LANE:(j + 1) * LANE, :]
            vct_ref[pl.ds(j * vrows + LANE, ATT_ONES), :] = jnp.ones((ATT_ONES, ctx_len), BF16)
        if latent:
            kpad_ref[pl.ds(0, w), :] = jnp.zeros((w, ATT_KV_CH), BF16)
            kpad_ref[pl.ds(seq + w, w), :] = jnp.zeros((w, ATT_KV_CH), BF16)
            kpad_ref[pl.ds(w, seq), :] = _rope(k_ref[0].astype(F32), cos_ref[...], sin_ref[...]).astype(BF16)
            for j in range(ATT_KV_CH // LANE):
                vtp_ref[pl.ds(j * vrows, LANE), pl.ds(0, w)] = jnp.zeros((LANE, w), BF16)
                vtp_ref[pl.ds(j * vrows, LANE), pl.ds(seq + w, w)] = jnp.zeros((LANE, w), BF16)
                vtp_ref[pl.ds(j * vrows + LANE, ATT_ONES), :] = jnp.ones((ATT_ONES, seq + 2 * w), BF16)
            for t in range(seq // w):
                vt = v_ref[0, pl.ds(t * w, w), :].astype(F32).T.astype(BF16)
                for j in range(ATT_KV_CH // LANE):
                    vtp_ref[pl.ds(j * vrows, LANE), pl.ds((t + 1) * w, w)] = vt[j * LANE:(j + 1) * LANE, :]

    if latent:
        base = pl.multiple_of(i * w, w)
        q = _rope(q_ref[0].astype(F32), cos_ref[pl.ds(base, w), :], sin_ref[pl.ds(base, w), :])
        kwin = kpad_ref[pl.ds(base, 3 * w), :]
        qi = lax.broadcasted_iota(jnp.int32, (w, w), 1)
        u = lax.broadcasted_iota(jnp.int32, (w, w), 0)
        before = (u - w >= qi - ATT_WINDOW) & (base + u - w >= 0)
        after = (u + w <= qi + ATT_WINDOW) & (base + u + w < seq)
        bias_lo = jnp.where(before, 0.0, -jnp.inf)
        bias_hi = jnp.where(after, 0.0, -jnp.inf)
        bias_lo = jnp.concatenate([bias_lo, bias_lo], axis=1)
        bias_hi = jnp.concatenate([bias_hi, bias_hi], axis=1)
    else:
        q = q_ref[0].astype(F32)
    qs = (q * (ATT_SCALE * LOG2E)).astype(BF16)
    kc = kc_ref[0]
    lane = lax.broadcasted_iota(jnp.int32, (w, LANE), 1)
    low_lanes = lane < ATT_HEAD_DIM
    low_rows = lax.broadcasted_iota(jnp.int32, (LANE, w), 0) < ATT_HEAD_DIM
    zero = jnp.zeros((w, LANE), BF16)
    n_cols = ATT_HEADS // 2

    def scores(c):
        kv = slice((c // ATT_REP) * LANE, (c // ATT_REP + 1) * LANE)
        qc = qs[:, c * LANE:(c + 1) * LANE]
        rhs = jnp.concatenate([jnp.where(low_lanes, qc, zero), jnp.where(low_lanes, zero, qc)], axis=0)
        s_ctx = lax.dot_general(kc[:, kv], rhs, nt, preferred_element_type=F32)
        s_lat = lax.dot_general(kwin[:, kv], rhs, nt, preferred_element_type=F32) if latent else None
        return s_ctx, s_lat

    def finish(c, s_ctx, s_lat):
        j = c // ATT_REP
        snk = sink_ref[0, :, pl.ds(c * 2 * w, 2 * w)]
        m = jnp.maximum(jnp.max(s_ctx, axis=0, keepdims=True), snk)
        if latent:
            s_lo = s_lat[:w] + bias_lo
            s_mid = s_lat[w:2 * w]
            s_hi = s_lat[2 * w:] + bias_hi
            m = jnp.maximum(m, jnp.maximum(jnp.maximum(jnp.max(s_lo, axis=0, keepdims=True),
                                                       jnp.max(s_mid, axis=0, keepdims=True)),
                                           jnp.max(s_hi, axis=0, keepdims=True)))
        p_ctx = jnp.exp2(s_ctx - m).astype(BF16)
        acc = jnp.dot(vct_ref[pl.ds(j * vrows, vrows), :], p_ctx, preferred_element_type=F32)
        if latent:
            p_lat = jnp.concatenate([jnp.exp2(s_lo - m), jnp.exp2(s_mid - m), jnp.exp2(s_hi - m)],
                                    axis=0).astype(BF16)
            acc = acc + jnp.dot(vtp_ref[pl.ds(j * vrows, vrows), pl.ds(base, 3 * w)], p_lat,
                                preferred_element_type=F32)
        den = acc[LANE:LANE + 1, :] + jnp.exp2(snk - m)
        acc = acc[:LANE, :] * (1.0 / den)
        return jnp.where(low_rows, acc[:, :w], acc[:, w:]).T

    outs = []
    ahead = 3
    pending = [scores(c) for c in range(ahead)]
    for c in range(n_cols):
        if c + ahead < n_cols:
            pending.append(scores(c + ahead))
        outs.append(finish(c, *pending.pop(0)))
    gd = gd_ref[0].astype(F32)
    o_ref[0] = (jnp.concatenate(outs, axis=-1) * gd).astype(o_ref.dtype)


def _attention(proj, proj_ctx, sink, cos, sin_signed, latent):
    bsz, seq, _ = proj.shape
    ctx_len = proj_ctx.shape[1]
    w = ATT_BLOCK
    qch = ATT_HEADS * ATT_HEAD_DIM
    kvc = ATT_KV_CH
    q_spec = pl.BlockSpec((1, w, qch), lambda b, i: (b, i, OD_Q // qch))
    gd_spec = pl.BlockSpec((1, w, qch), lambda b, i: (b, i, OD_GD // qch))
    kc_spec = pl.BlockSpec((1, ctx_len, kvc), lambda b, i: (b, 0, OD_K // kvc))
    vc_spec = pl.BlockSpec((1, ctx_len, kvc), lambda b, i: (b, 0, OD_VA // kvc))
    sink2 = jnp.repeat(sink[jnp.array(_att_head_order(), jnp.int32)] * LOG2E, w).reshape(1, 1, ATT_HEADS * w)
    sink_spec = pl.BlockSpec((1, 1, ATT_HEADS * w), lambda b, i: (0, 0, 0))
    vrows = (kvc // LANE) * (LANE + ATT_ONES)
    vct = pltpu.VMEM((vrows, ctx_len), BF16)
    if latent:
        in_specs = [q_spec, gd_spec,
                    pl.BlockSpec((1, seq, kvc), lambda b, i: (b, 0, OD_K // kvc)),
                    pl.BlockSpec((1, seq, kvc), lambda b, i: (b, 0, OD_VA // kvc)),
                    kc_spec, vc_spec, sink_spec,
                    pl.BlockSpec((seq, LANE), lambda b, i: (0, 0)),
                    pl.BlockSpec((seq, LANE), lambda b, i: (0, 0))]
        args = (proj, proj, proj, proj, proj_ctx, proj_ctx, sink2, cos, sin_signed)
        scratch = [pltpu.VMEM((seq + 2 * w, kvc), BF16), pltpu.VMEM((vrows, seq + 2 * w), BF16), vct]
    else:
        in_specs = [q_spec, gd_spec, kc_spec, vc_spec, sink_spec]
        args = (proj, proj, proj_ctx, proj_ctx, sink2)
        scratch = [vct]
    return pl.pallas_call(
        functools.partial(_attn_kernel, seq=seq, ctx_len=ctx_len, latent=latent),
        out_shape=jax.ShapeDtypeStruct((bsz, seq, qch), BF16),
        grid=(bsz, seq // w),
        in_specs=in_specs,
        out_specs=pl.BlockSpec((1, w, qch), lambda b, i: (b, i, 0)),
        scratch_shapes=scratch,
        compiler_params=_params("arbitrary", "arbitrary"),
        name="attention" if latent else "ctx_attention",
    )(*args)


def _outproj_kernel(*refs, even):
    if even:
        (y_ref, z_ref, cv_ref, gate_ref, nw_ref, cvg_ref, cvb_ref,
         x_ref, g_ref, lng_ref, lnb_ref, w_ref, o_ref) = refs
        t = y_ref[0].astype(F32) * z_ref[0].astype(F32)
        ya = t * lax.rsqrt(jnp.mean(t * t, -1, keepdims=True) + LN_EPS) * nw_ref[...]
        yb = (_silu(_layer_norm(cv_ref[0].astype(F32), cvg_ref[...], cvb_ref[...]))
              * gate_ref[0].astype(F32))
        ya = ya.astype(BF16)
        yb = yb.astype(BF16)
    else:
        ya_ref, yb_ref, x_ref, g_ref, lng_ref, lnb_ref, w_ref, o_ref = refs
        ya = ya_ref[0]
        yb = yb_ref[0]
    half = w_ref.shape[0] // 2
    y = (jnp.dot(ya, w_ref[pl.ds(0, half), :], preferred_element_type=F32)
         + jnp.dot(yb, w_ref[pl.ds(half, half), :], preferred_element_type=F32))
    r = DEEPNORM_ALPHA * x_ref[0] + g_ref[0] * y
    o_ref[0] = _layer_norm(r, lng_ref[...], lnb_ref[...])


def _outproj(mix_args, mix_specs, x, g, ln_g, ln_b, w_out, even, tm):
    bsz, seq, d = x.shape
    vec = pl.BlockSpec((1, d), lambda b, i: (0, 0))
    in_specs = list(mix_specs) + [
        pl.BlockSpec((1, tm, d), lambda b, i: (b, i, 0)),
        pl.BlockSpec((1, 1, d), lambda b, i: (b, 0, 0)),
        vec, vec,
        pl.BlockSpec(w_out.shape, lambda b, i: (0, 0))]
    return pl.pallas_call(
        functools.partial(_outproj_kernel, even=even),
        out_shape=jax.ShapeDtypeStruct((bsz, seq, d), F32),
        grid=(bsz, seq // tm),
        in_specs=in_specs,
        out_specs=pl.BlockSpec((1, tm, d), lambda b, i: (b, i, 0)),
        compiler_params=_params("arbitrary", "arbitrary"),
        name="outproj_even" if even else "outproj_odd",
    )(*mix_args, x, g, ln_g.reshape(1, d), ln_b.reshape(1, d), w_out)


def _outproj_even(y_ssd, cv, proj, norm_w, cv_ln_g, cv_ln_b, x, g, ln_g, ln_b, w_out):
    seq = x.shape[1]
    tm = min(seq, 512)
    ch = SSD_INNER
    blk = lambda col: pl.BlockSpec((1, tm, ch), lambda b, i: (b, i, col // ch))
    vec = pl.BlockSpec((1, ch), lambda b, i: (0, 0))
    specs = [blk(0), blk(EVO_Z), blk(0), blk(EVO_GATE), vec, vec, vec]
    args = (y_ssd, proj, cv, proj, norm_w.reshape(1, ch), cv_ln_g.reshape(1, ch), cv_ln_b.reshape(1, ch))
    return _outproj(args, specs, x, g, ln_g, ln_b, w_out, True, tm)


def _outproj_odd(yc, yd, x, g, ln_g, ln_b, w_out):
    seq = x.shape[1]
    tm = min(seq, 512)
    ch = MLP_CH
    blk = pl.BlockSpec((1, tm, ch), lambda b, i: (b, i, 0))
    return _outproj((yc, yd), [blk, blk], x, g, ln_g, ln_b, w_out, False, tm)


def _rope_tables(seq):
    t = jnp.arange(seq)
    row = (t // GRID_W).astype(F32)
    col = (t % GRID_W).astype(F32)
    n_freq = ATT_HEAD_DIM // 4
    inv = ROPE_BASE ** (-jnp.arange(n_freq, dtype=F32) / n_freq)
    ang = jnp.concatenate([row[:, None] * inv, col[:, None] * inv], -1)
    cos, sin = jnp.cos(ang), jnp.sin(ang)
    reps = LANE // ATT_HEAD_DIM
    return (jnp.tile(jnp.concatenate([cos, cos], -1), (1, reps)),
            jnp.tile(jnp.concatenate([-sin, sin], -1), (1, reps)))


def _even_weights(w_in):
    o_z, o_xbc, o_dt, o_glu, o_gate = 0, 1024, 2560, 2592, 4640
    main = jnp.concatenate([w_in[:, o_z:o_xbc], w_in[:, o_glu:o_gate], w_in[:, o_gate:],
                            w_in[:, o_xbc:o_dt]], axis=1)
    w_dt = w_in[:, o_dt:o_glu][:, jnp.array(_group_dt_order(), jnp.int32)]
    return main.astype(BF16), w_dt.astype(BF16)


def _odd_weights(w_in):
    o_q, o_k, o_gd = 3072, 4096, 4608
    perm = jnp.array(_att_col_perm(), jnp.int32)
    return jnp.concatenate([w_in[:, :o_q], w_in[:, o_q:o_k][:, perm], w_in[:, o_gd:][:, perm],
                            w_in[:, o_k:o_gd]], axis=1).astype(BF16)


def _odd_out_weights(w_out):
    perm = jnp.array(_att_col_perm(), jnp.int32)
    return jnp.concatenate([w_out[:MLP_CH], w_out[MLP_CH:][perm]], axis=0).astype(BF16)


def kernel(x, c, ctx, c_ctx, mod_w, mod_b, ln_g, ln_b, ev_w_in, ev_ssd_conv_w, ev_ssd_conv_b, ev_dt_bias, ev_a_log, ev_d_skip, ev_ssd_norm, ev_cv_w, ev_cv_b, ev_cv_ln_g, ev_cv_ln_b, ev_w_out, od_w_in, od_mlp_ln_g, od_mlp_ln_b, od_ws, od_bs, od_sink, od_w_out):
    bsz, seq, d = x.shape
    cos, sin_signed = _rope_tables(seq)
    rows = -(-(bsz + 1) // SUBLANE) * SUBLANE
    cond = jnp.concatenate([c, c_ctx[None, :], jnp.zeros((rows - bsz - 1, d), F32)], axis=0)
    mod = _modulation(cond, mod_w, mod_b)
    zero_state = jnp.zeros((bsz, 2, SSD_GROUPS, SSD_STATE, SSD_GROUP_CH), F32)

    for layer in range(DEPTH):
        last = layer == DEPTH - 1
        i = layer // 2
        m = mod[layer]
        sh_x, sc_x, g_x = (m[:bsz, None, k * d:(k + 1) * d] for k in range(3))
        sh_c, sc_c, g_c = (jnp.broadcast_to(m[bsz:bsz + 1, None, k * d:(k + 1) * d], (bsz, 1, d)) for k in range(3))
        if layer % 2 == 0:
            w_main, w_dt = _even_weights(ev_w_in[i])
            w_out = ev_w_out[i].astype(BF16)
            ssd_args = (ev_ssd_conv_w[i], ev_ssd_conv_b[i], ev_dt_bias[i], ev_a_log[i], ev_d_skip[i])
            p_c, dt_c, dtt_c = _inproj_ctx(ctx, sc_c, sh_c, w_main, w_dt)
            y_c, h_c = _ssd(p_c, dt_c, dtt_c, zero_state, *ssd_args)
            p_x, dt_x, dtt_x = _inproj(x, sc_x, sh_x, w_main, w_dt)
            y_x, _ = _ssd(p_x, dt_x, dtt_x, h_c, *ssd_args)
            cv_x = _cvconv(p_x, ev_cv_w[i], ev_cv_b[i])
            x = _outproj_even(y_x, cv_x, p_x, ev_ssd_norm[i], ev_cv_ln_g[i], ev_cv_ln_b[i],
                              x, g_x, ln_g[layer], ln_b[layer], w_out)
            if not last:
                cv_c = _cvconv(p_c, ev_cv_w[i], ev_cv_b[i])
                ctx = _outproj_even(y_c, cv_c, p_c, ev_ssd_norm[i], ev_cv_ln_g[i], ev_cv_ln_b[i],
                                    ctx, g_c, ln_g[layer], ln_b[layer], w_out)
        else:
            w_main = _odd_weights(od_w_in[i])
            w_out = _odd_out_weights(od_w_out[i])
            p_c = _inproj_ctx(ctx, sc_c, sh_c, w_main)[0]
            p_x = _inproj(x, sc_x, sh_x, w_main)[0]
            yd_x = _attention(p_x, p_c, od_sink[i], cos, sin_signed, True)
            yc_x = _gmlp(p_x, od_mlp_ln_g[i], od_mlp_ln_b[i], od_ws[i], od_bs[i])
            x = _outproj_odd(yc_x, yd_x, x, g_x, ln_g[layer], ln_b[layer], w_out)
            if not last:
                yd_c = _attention(p_c, p_c, od_sink[i], cos, sin_signed, False)
                yc_c = _gmlp(p_c, od_mlp_ln_g[i], od_mlp_ln_b[i], od_ws[i], od_bs[i])
                ctx = _outproj_odd(yc_c, yd_c, ctx, g_c, ln_g[layer], ln_b[layer], w_out)
    return x
```

```python
import functools
import math

import jax
import jax.numpy as jnp
from jax import lax
from jax.experimental import pallas as pl
from jax.experimental.pallas import tpu as pltpu

F32 = jnp.float32
BF16 = jnp.bfloat16

D_MODEL = 1024
DEPTH = 4
GRID_W = 64

SSD_HEADS = 16
SSD_HEAD_DIM = 64
SSD_INNER = SSD_HEADS * SSD_HEAD_DIM
SSD_GROUPS = 2
SSD_HPG = SSD_HEADS // SSD_GROUPS
SSD_STATE = 128
SSD_CHUNK = 128
SSD_CONV = 5
SSD_GROUP_CH = SSD_HPG * SSD_HEAD_DIM
CONV_CH = 1024
CONV_WIDTH = 31
MLP_CH = 1024
MLP_GROUPS = 8
MLP_GROUP_CH = MLP_CH // MLP_GROUPS
MLP_CHUNK = 128
ATT_HEADS = 16
ATT_KV_HEADS = 4
ATT_REP = ATT_HEADS // ATT_KV_HEADS
ATT_HEAD_DIM = 64
ATT_WINDOW = 128
ATT_BLOCK = 128
ATT_SCALE = ATT_HEAD_DIM ** -0.5
LOG2E = math.log2(math.e)
ROPE_BASE = 10000.0
ATT_KV_CH = ATT_KV_HEADS * ATT_HEAD_DIM
ATT_ONES = 16

DEEPNORM_ALPHA = (2 * DEPTH) ** 0.25
LN_EPS = 1e-5

PROJ_N = 5632
LANE = 128
SUBLANE = 8
VMEM_LIMIT = 56 * 1024 * 1024

EV_Z, EV_VAL, EV_GT, EV_GATE, EV_XS, EV_B, EV_C = 0, 1024, 2048, 3072, 4096, 5120, 5376
EVO_Z, EVO_GLU, EVO_GATE, EVO_XS, EVO_B, EVO_C = 0, 1024, 2048, 3072, 4096, 4352
OD_U, OD_V, OD_GC, OD_Q, OD_GD, OD_K, OD_VA = 0, 1024, 2048, 3072, 4096, 5120, 5376


def _silu(t):
    return t * jax.nn.sigmoid(t)


def _gelu_tanh(t):
    c = math.sqrt(2.0 / math.pi)
    return t * (0.5 * (1.0 + jnp.tanh(c * (t + 0.044715 * (t * t * t)))))


def _softplus(t):
    return jnp.maximum(t, 0.0) + jnp.log1p(jnp.exp(-jnp.abs(t)))


def _layer_norm(t, g, b):
    mu = jnp.mean(t, -1, keepdims=True)
    d = t - mu
    var = jnp.mean(d * d, -1, keepdims=True)
    return d * lax.rsqrt(var + LN_EPS) * g + b


def _params(*sem):
    return pltpu.CompilerParams(dimension_semantics=sem, vmem_limit_bytes=VMEM_LIMIT)


def _mod_kernel(c_ref, w_ref, b_ref, o_ref):
    s = _silu(c_ref[...]).astype(BF16)
    o_ref[0] = jnp.dot(s, w_ref[0].astype(BF16), preferred_element_type=F32) + b_ref[0]


def _modulation(cond, mod_w, mod_b):
    rows = cond.shape[0]
    d = D_MODEL
    return pl.pallas_call(
        _mod_kernel,
        out_shape=jax.ShapeDtypeStruct((DEPTH, rows, 3 * d), F32),
        grid=(DEPTH, 3),
        in_specs=[pl.BlockSpec((rows, d), lambda l, j: (0, 0)),
                  pl.BlockSpec((1, d, d), lambda l, j: (l, 0, j)),
                  pl.BlockSpec((1, 1, d), lambda l, j: (l, 0, j))],
        out_specs=pl.BlockSpec((1, rows, d), lambda l, j: (l, 0, j)),
        compiler_params=_params("arbitrary", "arbitrary"),
        name="modulation",
    )(cond, mod_w, mod_b.reshape(DEPTH, 1, 3 * d))


INPROJ_TM = 512
INPROJ_TN = 512


def _tile_plan(even):
    t = lambda col: col // INPROJ_TN
    if even:
        plan = [("silu", t(EV_Z) + i) for i in range(2)]
        plan += [("glu", t(EV_VAL) + i, t(EV_GT) + i) for i in range(2)]
        plan += [("silu", t(EV_GATE) + i) for i in range(2)]
        plan += [("id", t(EV_XS) + i) for i in range(3)]
    else:
        plan = [("gelu", t(OD_U) + i) for i in range(4)]
        plan += [("silu", t(OD_GC) + i) for i in range(2)]
        plan += [("id", t(OD_Q) + i) for i in range(2)]
        plan += [("silu", t(OD_GD) + i) for i in range(2)]
        plan += [("id", t(OD_K))]
    return plan


def _inproj_kernel(x_ref, sc_ref, sh_ref, w_ref, *rest, even, plan):
    if even:
        wdt_ref, wdtt_ref, o_ref, dt_ref, dtt_ref, h_ref = rest
    else:
        o_ref, h_ref = rest
    h_ref[...] = (x_ref[0] * (1.0 + sc_ref[0]) + sh_ref[0]).astype(BF16)
    if even:
        dt = jnp.dot(h_ref[...], wdt_ref[...], preferred_element_type=F32)
        dtt = lax.dot_general(wdtt_ref[...], h_ref[...], (((1,), (1,)), ((), ())),
                              preferred_element_type=F32)
        per = dt.shape[1] // SSD_GROUPS
        for gi in range(SSD_GROUPS):
            dt_ref[0, gi] = dt[:, gi * per:(gi + 1) * per]
            dtt_ref[0, gi] = dtt[gi * per:(gi + 1) * per, :]
    tile = lambda j: jnp.dot(h_ref[...], w_ref[:, pl.ds(j * INPROJ_TN, INPROJ_TN)], preferred_element_type=F32)
    for out_j, (kind, *src) in enumerate(plan):
        r = tile(src[0])
        if kind == "silu":
            r = _silu(r)
        elif kind == "gelu":
            r = _gelu_tanh(r)
        elif kind == "glu":
            r = r * jax.nn.sigmoid(tile(src[1]))
        o_ref[0, :, pl.ds(out_j * INPROJ_TN, INPROJ_TN)] = r.astype(BF16)


def _inproj(x, sc, sh, w, w_dt=None, plan=None):
    bsz, seq, d = x.shape
    n = w.shape[1]
    tm = min(seq, INPROJ_TM)
    even = w_dt is not None
    plan = tuple(_tile_plan(even)) if plan is None else plan
    n_out = len(plan) * INPROJ_TN
    resident = dict(pipeline_mode=pl.Buffered(1))
    in_specs = [pl.BlockSpec((1, tm, d), lambda b, i: (b, i, 0)),
                pl.BlockSpec((1, 1, d), lambda b, i: (b, 0, 0)),
                pl.BlockSpec((1, 1, d), lambda b, i: (b, 0, 0)),
                pl.BlockSpec((d, n), lambda b, i: (0, 0), **resident)]
    out_shape = [jax.ShapeDtypeStruct((bsz, seq, n_out), BF16)]
    out_specs = [pl.BlockSpec((1, tm, n_out), lambda b, i: (b, i, 0))]
    args = [x, sc, sh, w]
    if even:
        ndt = w_dt.shape[1]
        per = ndt // SSD_GROUPS
        in_specs += [pl.BlockSpec((d, ndt), lambda b, i: (0, 0)),
                     pl.BlockSpec((ndt, d), lambda b, i: (0, 0))]
        out_shape += [jax.ShapeDtypeStruct((bsz, SSD_GROUPS, seq, per), F32),
                      jax.ShapeDtypeStruct((bsz, SSD_GROUPS, per, seq), F32)]
        out_specs += [pl.BlockSpec((1, SSD_GROUPS, tm, per), lambda b, i: (b, 0, i, 0)),
                      pl.BlockSpec((1, SSD_GROUPS, per, tm), lambda b, i: (b, 0, 0, i))]
        args += [w_dt, w_dt.T]
    return pl.pallas_call(
        functools.partial(_inproj_kernel, even=even, plan=plan),
        out_shape=out_shape,
        grid=(bsz, seq // tm),
        in_specs=in_specs,
        out_specs=out_specs,
        scratch_shapes=[pltpu.VMEM((tm, d), BF16)],
        compiler_params=_params("arbitrary", "arbitrary"),
        name="inproj",
    )(*args)


def _inproj_ctx(ctx, sc, sh, w, w_dt=None, plan=None):
    bsz, clen, d = ctx.shape
    outs = _inproj(ctx.reshape(1, bsz * clen, d), sc[:1], sh[:1], w, w_dt, plan)
    proj = outs[0].reshape(bsz, clen, -1)
    if w_dt is None:
        return (proj,)
    per = outs[1].shape[-1]
    dt = outs[1].reshape(SSD_GROUPS, bsz, clen, per).transpose(1, 0, 2, 3)
    dtt = outs[2].reshape(SSD_GROUPS, per, bsz, clen).transpose(2, 0, 1, 3)
    return proj, dt, dtt


def _conv_silu(src_ref, w_ref, b_ref, pad_ref, dst_ref, seq):
    ch = src_ref.shape[-1]
    halo = SUBLANE
    pad_ref[pl.ds(0, halo), :] = jnp.zeros((halo, ch), F32)
    pad_ref[pl.ds(seq + halo, halo), :] = jnp.zeros((halo, ch), F32)
    pad_ref[pl.ds(halo, seq), :] = src_ref[0].astype(F32)
    w = w_ref[...]
    bias = b_ref[...]
    first = halo - SSD_CONV // 2

    def body(t, carry):
        base = pl.multiple_of(t * SSD_CHUNK, SSD_CHUNK)
        win = pad_ref[pl.ds(base, SSD_CHUNK + 2 * halo), :]
        acc = bias
        for k in range(SSD_CONV):
            acc = acc + w[k:k + 1, :] * win[first + k:first + k + SSD_CHUNK, :]
        dst_ref[pl.ds(base, SSD_CHUNK), :] = _silu(acc).astype(dst_ref.dtype)
        return carry

    lax.fori_loop(0, seq // SSD_CHUNK, body, 0)


def _pair_expand(vals, col0):
    q = vals.shape[0]
    lane = lax.broadcasted_iota(jnp.int32, (q, LANE), 1)
    parts = []
    for k in range(SSD_HPG // 2):
        a = jnp.broadcast_to(vals[:, col0 + 2 * k:col0 + 2 * k + 1], (q, LANE))
        b = jnp.broadcast_to(vals[:, col0 + 2 * k + 1:col0 + 2 * k + 2], (q, LANE))
        parts.append(jnp.where(lane < SSD_HEAD_DIM, a, b))
    return jnp.concatenate(parts, axis=-1)


def _split_dot(lhs, rhs):
    if lhs.dtype == F32:
        hi = lhs.astype(BF16)
        lo = (lhs - hi.astype(F32)).astype(BF16)
        return (jnp.dot(hi, rhs, preferred_element_type=F32)
                + jnp.dot(lo, rhs, preferred_element_type=F32))
    hi = rhs.astype(BF16)
    lo = (rhs - hi.astype(F32)).astype(BF16)
    return (jnp.dot(lhs, hi, preferred_element_type=F32)
            + jnp.dot(lhs, lo, preferred_element_type=F32))


def _ssd_kernel(xs_ref, bm_ref, cm_ref, dt_ref, dtt_ref,
                wx_ref, wb_ref, wc_ref, bx_ref, bb_ref, bc_ref,
                dtb_row_ref, dtb_col_ref, alog_row_ref, alog_col_ref, dskip_ref, *rest, seq, zero_init):
    h0_ref = None if zero_init else rest[0]
    (y_ref, hout_ref, padx_ref, padn_ref, xc_ref, bcs_ref, ccs_ref,
     yfwd_ref, ybwd_ref, state_ref) = rest[(0 if zero_init else 1):]
    q = SSD_CHUNK
    n_chunks = seq // q
    nh = 2 * SSD_HPG

    _conv_silu(xs_ref, wx_ref, bx_ref, padx_ref, xc_ref, seq)
    _conv_silu(bm_ref, wb_ref, bb_ref, padn_ref, bcs_ref, seq)
    _conv_silu(cm_ref, wc_ref, bc_ref, padn_ref, ccs_ref, seq)

    a_row = -jnp.exp(alog_row_ref[0])
    a_col = -jnp.exp(alog_col_ref[0])
    dtb_row = dtb_row_ref[0]
    dtb_col = dtb_col_ref[0]
    row_i = lax.broadcasted_iota(jnp.int32, (q, q), 0)
    col_i = lax.broadcasted_iota(jnp.int32, (q, q), 1)
    lower = row_i >= col_i
    upper = row_i <= col_i
    tri_lo = jnp.where(lower, 1.0, 0.0).astype(BF16)
    tri_up = jnp.where(upper, 1.0, 0.0).astype(BF16)
    lane = lax.broadcasted_iota(jnp.int32, (q, LANE), 1)
    head_lo = lane < SSD_HEAD_DIM

    def prologue(t, direction):
        rows = pl.ds(pl.multiple_of(t * q, q), q)
        dt = _softplus(dt_ref[0, 0, rows, :] + dtb_row)
        dtt = _softplus(dtt_ref[0, 0, :, rows] + dtb_col)
        da = dt * a_row
        dat = dtt * a_col
        if direction == 0:
            cum = _split_dot(tri_lo, da)
            cumt = _split_dot(dat, tri_up)
        else:
            cum = _split_dot(tri_up, da)
            cumt = _split_dot(dat, tri_lo)
        edge = q - 1 if direction == 0 else 0
        wt = dtt * jnp.exp(cumt[:, edge:edge + 1] - cumt)
        bc = bcs_ref[rows, :]
        cc = ccs_ref[rows, :]
        cb = lax.dot_general(cc, bc, (((1,), (1,)), ((), ())), preferred_element_type=F32)
        bctf = bc.astype(F32).T
        return cum, cumt - jnp.log(dtt), wt, cb, bctf

    def chunk(t, direction, pro):
        cum, rowt, wt, cb, bctf = pro
        rows = pl.ds(pl.multiple_of(t * q, q), q)
        mask = lower if direction == 0 else upper
        edge = q - 1 if direction == 0 else 0
        total = cum[edge:edge + 1, :]
        col0 = direction * SSD_HPG
        xb = xc_ref[rows, :]
        ccf = ccs_ref[rows, :].astype(F32)
        st = state_ref[direction]
        stb = st.astype(BF16)
        zero = jnp.zeros((q, LANE), BF16)
        ys, news = [], []
        for k in range(SSD_HPG // 2):
            xp = xb[:, k * LANE:(k + 1) * LANE]
            sp = stb[:, k * LANE:(k + 1) * LANE]
            rhs_x = jnp.concatenate([jnp.where(head_lo, xp, zero), jnp.where(head_lo, zero, xp)], axis=0)
            rhs_s = jnp.concatenate([jnp.where(head_lo, sp, zero), jnp.where(head_lo, zero, sp)], axis=0)
            ms, cs, ws = [], [], []
            for half in range(2):
                c = col0 + 2 * k + half
                bcol = jnp.broadcast_to(cum[:, c:c + 1], (q, q))
                seg = bcol - jnp.broadcast_to(rowt[c:c + 1, :], (q, q))
                ms.append((cb * jnp.exp(jnp.where(mask, seg, -jnp.inf))).astype(BF16))
                cs.append((ccf * jnp.exp(bcol)).astype(BF16))
                ws.append((bctf * jnp.broadcast_to(wt[c:c + 1, :], (q, q))).astype(BF16))
            ys.append(jnp.dot(jnp.concatenate(ms + cs, axis=1), jnp.concatenate([rhs_x, rhs_s], axis=0),
                              preferred_element_type=F32))
            news.append(jnp.dot(jnp.concatenate(ws, axis=1), rhs_x, preferred_element_type=F32))
        state_ref[direction] = st * _pair_expand(jnp.exp(total), col0) + jnp.concatenate(news, axis=-1)
        return rows, jnp.concatenate(ys, axis=-1)

    state_ref[...] = jnp.zeros(state_ref.shape, F32) if zero_init else h0_ref[0, :, 0]

    def both(i, carry):
        pro_f, pro_b = carry
        nxt_f = prologue(jnp.minimum(i + 1, n_chunks - 1), 0)
        nxt_b = prologue(jnp.maximum(n_chunks - 2 - i, 0), 1)
        rows, y = chunk(i, 0, pro_f)
        yfwd_ref[rows, :] = y
        rows, y = chunk(n_chunks - 1 - i, 1, pro_b)
        ybwd_ref[rows, :] = y
        return nxt_f, nxt_b

    lax.fori_loop(0, n_chunks, both, (prologue(0, 0), prologue(n_chunks - 1, 1)))
    hout_ref[0, :, 0] = state_ref[...]

    dskip = dskip_ref[...]

    def finish(t, carry):
        rows = pl.ds(pl.multiple_of(t * q, q), q)
        y = yfwd_ref[rows, :] + ybwd_ref[rows, :] + xc_ref[rows, :].astype(F32) * dskip
        y_ref[0, rows, :] = y.astype(y_ref.dtype)
        return carry

    lax.fori_loop(0, n_chunks, finish, 0)


def _group_dt_order():
    order = []
    for g in range(SSD_GROUPS):
        for direction in range(2):
            order += [direction * SSD_HEADS + g * SSD_HPG + r for r in range(SSD_HPG)]
    return order


def _ssd(proj, dt, dtt, h0, conv_w, conv_b, dt_bias, a_log, d_skip):
    bsz, seq, _ = proj.shape
    gc = SSD_GROUP_CH
    ns = SSD_STATE
    nh = 2 * SSD_HPG
    xs_blk, b_blk, c_blk = EVO_XS // gc, EVO_B // ns, EVO_C // ns
    cw_b0, cw_c0 = SSD_INNER // ns, (SSD_INNER + SSD_GROUPS * ns) // ns
    conv_b2 = conv_b.reshape(1, -1)
    d_skip_x = jnp.repeat(d_skip, SSD_HEAD_DIM).reshape(1, SSD_INNER)
    order = jnp.array(_group_dt_order(), jnp.int32)
    dt_bias_g = dt_bias.reshape(-1)[order].reshape(SSD_GROUPS, nh)
    a_log_g = a_log.reshape(-1)[order].reshape(SSD_GROUPS, nh)
    in_specs = [
        pl.BlockSpec((1, seq, gc), lambda b, g: (b, 0, xs_blk + g)),
        pl.BlockSpec((1, seq, ns), lambda b, g: (b, 0, b_blk + g)),
        pl.BlockSpec((1, seq, ns), lambda b, g: (b, 0, c_blk + g)),
        pl.BlockSpec((1, 1, seq, nh), lambda b, g: (b, g, 0, 0)),
        pl.BlockSpec((1, 1, nh, seq), lambda b, g: (b, g, 0, 0)),
        pl.BlockSpec((SSD_CONV, gc), lambda b, g: (0, g)),
        pl.BlockSpec((SSD_CONV, ns), lambda b, g: (0, cw_b0 + g)),
        pl.BlockSpec((SSD_CONV, ns), lambda b, g: (0, cw_c0 + g)),
        pl.BlockSpec((1, gc), lambda b, g: (0, g)),
        pl.BlockSpec((1, ns), lambda b, g: (0, cw_b0 + g)),
        pl.BlockSpec((1, ns), lambda b, g: (0, cw_c0 + g)),
        pl.BlockSpec((1, 1, nh), lambda b, g: (g, 0, 0)),
        pl.BlockSpec((1, nh, 1), lambda b, g: (g, 0, 0)),
        pl.BlockSpec((1, 1, nh), lambda b, g: (g, 0, 0)),
        pl.BlockSpec((1, nh, 1), lambda b, g: (g, 0, 0)),
        pl.BlockSpec((1, gc), lambda b, g: (0, g)),
    ]
    state_spec = pl.BlockSpec((1, 2, 1, ns, gc), lambda b, g: (b, 0, g, 0, 0))
    args = [proj, proj, proj, dt, dtt,
            conv_w, conv_w, conv_w, conv_b2, conv_b2, conv_b2,
            dt_bias_g.reshape(SSD_GROUPS, 1, nh), dt_bias_g.reshape(SSD_GROUPS, nh, 1),
            a_log_g.reshape(SSD_GROUPS, 1, nh), a_log_g.reshape(SSD_GROUPS, nh, 1), d_skip_x]
    if h0 is not None:
        in_specs.append(state_spec)
        args.append(h0)
    return pl.pallas_call(
        functools.partial(_ssd_kernel, seq=seq, zero_init=h0 is None),
        out_shape=[jax.ShapeDtypeStruct((bsz, seq, SSD_INNER), BF16),
                   jax.ShapeDtypeStruct((bsz, 2, SSD_GROUPS, ns, gc), F32)],
        grid=(bsz, SSD_GROUPS),
        in_specs=in_specs,
        out_specs=[pl.BlockSpec((1, seq, gc), lambda b, g: (b, 0, g)), state_spec],
        scratch_shapes=[pltpu.VMEM((seq + 2 * SUBLANE, gc), F32),
                        pltpu.VMEM((seq + 2 * SUBLANE, ns), F32),
                        pltpu.VMEM((seq, gc), BF16),
                        pltpu.VMEM((seq, ns), BF16),
                        pltpu.VMEM((seq, ns), BF16),
                        pltpu.VMEM((seq, gc), F32),
                        pltpu.VMEM((seq, gc), F32),
                        pltpu.VMEM((2, ns, gc), F32)],
        compiler_params=_params("arbitrary", "arbitrary"),
        name="ssd",
    )(*args)


CV_ROWS = 256
CV_HALO = 16
CV_TC = 128
CV_SPAN = CV_ROWS + 2 * CV_HALO - SUBLANE


def _cvconv_kernel(glu_ref, w_ref, b_ref, o_ref, pad_ref, sh_ref, *, seq):
    ch = glu_ref.shape[-1]
    pad_ref[pl.ds(0, CV_HALO), :] = jnp.zeros((CV_HALO, ch), F32)
    pad_ref[pl.ds(seq + CV_HALO, CV_HALO), :] = jnp.zeros((CV_HALO, ch), F32)
    pad_ref[pl.ds(CV_HALO, seq), :] = glu_ref[0].astype(F32)
    w = w_ref[...]
    bias = b_ref[...]
    first = CV_HALO - CONV_WIDTH // 2
    rows = min(CV_ROWS * CV_TC // ch, seq)
    span = rows + 2 * CV_HALO - SUBLANE

    def body(t, carry):
        base = pl.multiple_of(t * rows, rows)
        win = pad_ref[pl.ds(base, rows + 2 * CV_HALO), :]
        for s in range(SUBLANE):
            sh_ref[s, pl.ds(0, span), :] = win[s:s + span, :]
        acc = jnp.broadcast_to(bias, (rows, ch))
        for k in range(CONV_WIDTH):
            a, s = divmod(first + k, SUBLANE)
            acc = acc + w[k:k + 1, :] * sh_ref[s, pl.ds(a * SUBLANE, rows), :]
        o_ref[0, pl.ds(base, rows), :] = acc.astype(o_ref.dtype)
        return carry

    lax.fori_loop(0, seq // rows, body, 0)


def _cvconv(proj, cv_w, cv_b):
    bsz, seq, _ = proj.shape
    tc = CV_TC * max(1, 1024 // seq)
    return pl.pallas_call(
        functools.partial(_cvconv_kernel, seq=seq),
        out_shape=jax.ShapeDtypeStruct((bsz, seq, CONV_CH), BF16),
        grid=(bsz, CONV_CH // tc),
        in_specs=[pl.BlockSpec((1, seq, tc), lambda b, j: (b, 0, EVO_GLU // tc + j)),
                  pl.BlockSpec((CONV_WIDTH, tc), lambda b, j: (0, j)),
                  pl.BlockSpec((1, tc), lambda b, j: (0, j))],
        out_specs=pl.BlockSpec((1, seq, tc), lambda b, j: (b, 0, j)),
        scratch_shapes=[pltpu.VMEM((seq + 2 * CV_HALO, tc), F32),
                        pltpu.VMEM((SUBLANE, CV_SPAN, tc), F32)],
        compiler_params=_params("arbitrary", "arbitrary"),
        name="cvconv",
    )(proj, cv_w, cv_b.reshape(1, CONV_CH))


def _gmlp_kernel(u_ref, v_ref, gc_ref, lng_ref, lnb_ref, ws_ref, bias_ref, o_ref, *, tm):
    lng = lng_ref[...]
    lnb = lnb_ref[...]
    bias = bias_ref[...]
    for c in range(tm // MLP_CHUNK):
        rows = pl.ds(c * MLP_CHUNK, MLP_CHUNK)
        vn = _layer_norm(v_ref[0, rows, :].astype(F32), lng, lnb).astype(BF16)
        mixed = jnp.concatenate(
            [jnp.dot(ws_ref[gi], vn[:, gi * MLP_GROUP_CH:(gi + 1) * MLP_GROUP_CH], preferred_element_type=F32)
             for gi in range(MLP_GROUPS)], axis=-1)
        u = u_ref[0, rows, :].astype(F32)
        o_ref[0, rows, :] = (u * (mixed + bias) * gc_ref[0, rows, :].astype(F32)).astype(o_ref.dtype)


def _gmlp(proj, ln_g, ln_b, ws, bs):
    bsz, seq, _ = proj.shape
    tm = min(seq, 512)
    ch = MLP_CH
    bias = jnp.repeat(bs.T, MLP_GROUP_CH, axis=1)
    return pl.pallas_call(
        functools.partial(_gmlp_kernel, tm=tm),
        out_shape=jax.ShapeDtypeStruct((bsz, seq, ch), BF16),
        grid=(bsz, seq // tm),
        in_specs=[pl.BlockSpec((1, tm, ch), lambda b, i: (b, i, OD_U // ch)),
                  pl.BlockSpec((1, tm, ch), lambda b, i: (b, i, OD_V // ch)),
                  pl.BlockSpec((1, tm, ch), lambda b, i: (b, i, OD_GC // ch)),
                  pl.BlockSpec((1, ch), lambda b, i: (0, 0)),
                  pl.BlockSpec((1, ch), lambda b, i: (0, 0)),
                  pl.BlockSpec((MLP_GROUPS, MLP_CHUNK, MLP_CHUNK), lambda b, i: (0, 0, 0)),
                  pl.BlockSpec((MLP_CHUNK, ch), lambda b, i: (0, 0))],
        out_specs=pl.BlockSpec((1, tm, ch), lambda b, i: (b, i, 0)),
        compiler_params=_params("arbitrary", "arbitrary"),
        name="gmlp",
    )(proj, proj, proj, ln_g.reshape(1, ch), ln_b.reshape(1, ch), ws.astype(BF16), bias)


def _rope(t, cos, sin_signed):
    width = t.shape[1]
    reps = width // LANE
    half = ATT_HEAD_DIM // 2
    lane = lax.broadcasted_iota(jnp.int32, t.shape, 1)
    first_half = (lane % ATT_HEAD_DIM) < half
    swapped = jnp.where(first_half, pltpu.roll(t, width - half, 1), pltpu.roll(t, half, 1))
    c = jnp.concatenate([cos] * reps, axis=-1)
    s = jnp.concatenate([sin_signed] * reps, axis=-1)
    return t * c + swapped * s


def _att_head_order():
    order = []
    for c in range(ATT_HEADS // 2):
        j, r = divmod(c, ATT_REP)
        order += [(2 * j) * ATT_REP + r, (2 * j + 1) * ATT_REP + r]
    return order


def _att_col_perm():
    return [h * ATT_HEAD_DIM + d for h in _att_head_order() for d in range(ATT_HEAD_DIM)]


def _attn_kernel(*refs, seq, ctx_len, latent):
    if latent:
        (q_ref, gd_ref, k_ref, v_ref, kc_ref, vc_ref, sink_ref, cos_ref, sin_ref,
         o_ref, kpad_ref, vtp_ref, vct_ref) = refs
    else:
        q_ref, gd_ref, kc_ref, vc_ref, sink_ref, o_ref, vct_ref = refs
    i = pl.program_id(1)
    w = ATT_BLOCK
    nt = (((1,), (1,)), ((), ()))

    vrows = LANE + ATT_ONES

    @pl.when(i == 0)
    def _():
        vct = vc_ref[0].astype(F32).T.astype(BF16)
        for j in range(ATT_KV_CH // LANE):
            vct_ref[pl.ds(j * vrows, LANE), :] = vct[j * LANE:(j + 1) * LANE, :]
            vct_ref[pl.ds(j * vrows + LANE, ATT_ONES), :] = jnp.ones((ATT_ONES, ctx_len), BF16)
        if latent:
            kpad_ref[pl.ds(0, w), :] = jnp.zeros((w, ATT_KV_CH), BF16)
            kpad_ref[pl.ds(seq + w, w), :] = jnp.zeros((w, ATT_KV_CH), BF16)
            kpad_ref[pl.ds(w, seq), :] = _rope(k_ref[0].astype(F32), cos_ref[...], sin_ref[...]).astype(BF16)
            for j in range(ATT_KV_CH // LANE):
                vtp_ref[pl.ds(j * vrows, LANE), pl.ds(0, w)] = jnp.zeros((LANE, w), BF16)
                vtp_ref[pl.ds(j * vrows, LANE), pl.ds(seq + w, w)] = jnp.zeros((LANE, w), BF16)
                vtp_ref[pl.ds(j * vrows + LANE, ATT_ONES), :] = jnp.ones((ATT_ONES, seq + 2 * w), BF16)
            for t in range(seq // w):
                vt = v_ref[0, pl.ds(t * w, w), :].astype(F32).T.astype(BF16)
                for j in range(ATT_KV_CH // LANE):
                    vtp_ref[pl.ds(j * vrows, LANE), pl.ds((t + 1) * w, w)] = vt[j * LANE:(j + 1) * LANE, :]

    if latent:
        base = pl.multiple_of(i * w, w)
        q = _rope(q_ref[0].astype(F32), cos_ref[pl.ds(base, w), :], sin_ref[pl.ds(base, w), :])
        kwin = kpad_ref[pl.ds(base, 3 * w), :]
        qi = lax.broadcasted_iota(jnp.int32, (w, w), 1)
        u = lax.broadcasted_iota(jnp.int32, (w, w), 0)
        before = (u - w >= qi - ATT_WINDOW) & (base + u - w >= 0)
        after = (u + w <= qi + ATT_WINDOW) & (base + u + w < seq)
        bias_lo = jnp.where(before, 0.0, -jnp.inf)
        bias_hi = jnp.where(after, 0.0, -jnp.inf)
        bias_lo = jnp.concatenate([bias_lo, bias_lo], axis=1)
        bias_hi = jnp.concatenate([bias_hi, bias_hi], axis=1)
    else:
        q = q_ref[0].astype(F32)
    qs = (q * (ATT_SCALE * LOG2E)).astype(BF16)
    kc = kc_ref[0]
    lane = lax.broadcasted_iota(jnp.int32, (w, LANE), 1)
    low_lanes = lane < ATT_HEAD_DIM
    low_rows = lax.broadcasted_iota(jnp.int32, (LANE, w), 0) < ATT_HEAD_DIM
    zero = jnp.zeros((w, LANE), BF16)
    n_cols = ATT_HEADS // 2

    def scores(c):
        kv = slice((c // ATT_REP) * LANE, (c // ATT_REP + 1) * LANE)
        qc = qs[:, c * LANE:(c + 1) * LANE]
        rhs = jnp.concatenate([jnp.where(low_lanes, qc, zero), jnp.where(low_lanes, zero, qc)], axis=0)
        s_ctx = lax.dot_general(kc[:, kv], rhs, nt, preferred_element_type=F32)
        s_lat = lax.dot_general(kwin[:, kv], rhs, nt, preferred_element_type=F32) if latent else None
        return s_ctx, s_lat

    def finish(c, s_ctx, s_lat):
        j = c // ATT_REP
        snk = sink_ref[0, :, pl.ds(c * 2 * w, 2 * w)]
        m = jnp.maximum(jnp.max(s_ctx, axis=0, keepdims=True), snk)
        if latent:
            s_lo = s_lat[:w] + bias_lo
            s_mid = s_lat[w:2 * w]
            s_hi = s_lat[2 * w:] + bias_hi
            m = jnp.maximum(m, jnp.maximum(jnp.maximum(jnp.max(s_lo, axis=0, keepdims=True),
                                                       jnp.max(s_mid, axis=0, keepdims=True)),
                                           jnp.max(s_hi, axis=0, keepdims=True)))
        p_ctx = jnp.exp2(s_ctx - m).astype(BF16)
        acc = jnp.dot(vct_ref[pl.ds(j * vrows, vrows), :], p_ctx, preferred_element_type=F32)
        if latent:
            p_lat = jnp.concatenate([jnp.exp2(s_lo - m), jnp.exp2(s_mid - m), jnp.exp2(s_hi - m)],
                                    axis=0).astype(BF16)
            acc = acc + jnp.dot(vtp_ref[pl.ds(j * vrows, vrows), pl.ds(base, 3 * w)], p_lat,
                                preferred_element_type=F32)
        den = acc[LANE:LANE + 1, :] + jnp.exp2(snk - m)
        acc = acc[:LANE, :] * (1.0 / den)
        return jnp.where(low_rows, acc[:, :w], acc[:, w:]).T

    outs = []
    ahead = 3
    pending = [scores(c) for c in range(ahead)]
    for c in range(n_cols):
        if c + ahead < n_cols:
            pending.append(scores(c + ahead))
        outs.append(finish(c, *pending.pop(0)))
    gd = gd_ref[0].astype(F32)
    o_ref[0] = (jnp.concatenate(outs, axis=-1) * gd).astype(o_ref.dtype)


def _attention(proj, proj_ctx, sink, cos, sin_signed, latent, ctx_col=OD_K):
    bsz, seq, _ = proj.shape
    ctx_len = proj_ctx.shape[1]
    w = ATT_BLOCK
    qch = ATT_HEADS * ATT_HEAD_DIM
    kvc = ATT_KV_CH
    q_spec = pl.BlockSpec((1, w, qch), lambda b, i: (b, i, OD_Q // qch))
    gd_spec = pl.BlockSpec((1, w, qch), lambda b, i: (b, i, OD_GD // qch))
    kc_spec = pl.BlockSpec((1, ctx_len, kvc), lambda b, i: (b, 0, ctx_col // kvc))
    vc_spec = pl.BlockSpec((1, ctx_len, kvc), lambda b, i: (b, 0, ctx_col // kvc + 1))
    sink2 = jnp.repeat(sink[jnp.array(_att_head_order(), jnp.int32)] * LOG2E, w).reshape(1, 1, ATT_HEADS * w)
    sink_spec = pl.BlockSpec((1, 1, ATT_HEADS * w), lambda b, i: (0, 0, 0))
    vrows = (kvc // LANE) * (LANE + ATT_ONES)
    vct = pltpu.VMEM((vrows, ctx_len), BF16)
    if latent:
        in_specs = [q_spec, gd_spec,
                    pl.BlockSpec((1, seq, kvc), lambda b, i: (b, 0, OD_K // kvc)),
                    pl.BlockSpec((1, seq, kvc), lambda b, i: (b, 0, OD_VA // kvc)),
                    kc_spec, vc_spec, sink_spec,
                    pl.BlockSpec((seq, LANE), lambda b, i: (0, 0)),
                    pl.BlockSpec((seq, LANE), lambda b, i: (0, 0))]
        args = (proj, proj, proj, proj, proj_ctx, proj_ctx, sink2, cos, sin_signed)
        scratch = [pltpu.VMEM((seq + 2 * w, kvc), BF16), pltpu.VMEM((vrows, seq + 2 * w), BF16), vct]
    else:
        in_specs = [q_spec, gd_spec, kc_spec, vc_spec, sink_spec]
        args = (proj, proj, proj_ctx, proj_ctx, sink2)
        scratch = [vct]
    return pl.pallas_call(
        functools.partial(_attn_kernel, seq=seq, ctx_len=ctx_len, latent=latent),
        out_shape=jax.ShapeDtypeStruct((bsz, seq, qch), BF16),
        grid=(bsz, seq // w),
        in_specs=in_specs,
        out_specs=pl.BlockSpec((1, w, qch), lambda b, i: (b, i, 0)),
        scratch_shapes=scratch,
        compiler_params=_params("arbitrary", "arbitrary"),
        name="attention" if latent else "ctx_attention",
    )(*args)


OUTPROJ_TM = 1024
OUTPROJ_SUB = 512


def _outproj_kernel(*refs, even):
    if even:
        (y_ref, z_ref, cv_ref, gate_ref, nw_ref, cvg_ref, cvb_ref,
         x_ref, g_ref, lng_ref, lnb_ref, w_ref, o_ref) = refs
    else:
        ya_ref, yb_ref, x_ref, g_ref, lng_ref, lnb_ref, w_ref, o_ref = refs
    half = w_ref.shape[0] // 2
    tm = x_ref.shape[1]
    sub = min(OUTPROJ_SUB, tm)

    def project(k):
        rows = pl.ds(k * sub, sub)
        if even:
            t = y_ref[0, rows, :].astype(F32) * z_ref[0, rows, :].astype(F32)
            ya = (t * lax.rsqrt(jnp.mean(t * t, -1, keepdims=True) + LN_EPS) * nw_ref[...]).astype(BF16)
            yb = (_silu(_layer_norm(cv_ref[0, rows, :].astype(F32), cvg_ref[...], cvb_ref[...]))
                  * gate_ref[0, rows, :].astype(F32)).astype(BF16)
        else:
            ya = ya_ref[0, rows, :]
            yb = yb_ref[0, rows, :]
        return (jnp.dot(ya, w_ref[pl.ds(0, half), :], preferred_element_type=F32)
                + jnp.dot(yb, w_ref[pl.ds(half, half), :], preferred_element_type=F32))

    n_sub = tm // sub
    y = project(0)
    for k in range(n_sub):
        nxt = project(k + 1) if k + 1 < n_sub else None
        rows = pl.ds(k * sub, sub)
        r = DEEPNORM_ALPHA * x_ref[0, rows, :] + g_ref[0] * y
        o_ref[0, rows, :] = _layer_norm(r, lng_ref[...], lnb_ref[...])
        y = nxt


def _outproj(mix_args, mix_specs, x, g, ln_g, ln_b, w_out, even, tm):
    bsz, seq, d = x.shape
    vec = pl.BlockSpec((1, d), lambda b, i: (0, 0))
    in_specs = list(mix_specs) + [
        pl.BlockSpec((1, tm, d), lambda b, i: (b, i, 0)),
        pl.BlockSpec((1, 1, d), lambda b, i: (b, 0, 0)),
        vec, vec,
        pl.BlockSpec(w_out.shape, lambda b, i: (0, 0))]
    return pl.pallas_call(
        functools.partial(_outproj_kernel, even=even),
        out_shape=jax.ShapeDtypeStruct((bsz, seq, d), F32),
        grid=(bsz, seq // tm),
        in_specs=in_specs,
        out_specs=pl.BlockSpec((1, tm, d), lambda b, i: (b, i, 0)),
        compiler_params=_params("arbitrary", "arbitrary"),
        name="outproj_even" if even else "outproj_odd",
    )(*mix_args, x, g, ln_g.reshape(1, d), ln_b.reshape(1, d), w_out)


def _outproj_even(y_ssd, cv, proj, norm_w, cv_ln_g, cv_ln_b, x, g, ln_g, ln_b, w_out):
    seq = x.shape[1]
    tm = min(seq, OUTPROJ_TM)
    ch = SSD_INNER
    blk = lambda col: pl.BlockSpec((1, tm, ch), lambda b, i: (b, i, col // ch))
    vec = pl.BlockSpec((1, ch), lambda b, i: (0, 0))
    specs = [blk(0), blk(EVO_Z), blk(0), blk(EVO_GATE), vec, vec, vec]
    args = (y_ssd, proj, cv, proj, norm_w.reshape(1, ch), cv_ln_g.reshape(1, ch), cv_ln_b.reshape(1, ch))
    return _outproj(args, specs, x, g, ln_g, ln_b, w_out, True, tm)


def _outproj_odd(yc, yd, x, g, ln_g, ln_b, w_out):
    seq = x.shape[1]
    tm = min(seq, OUTPROJ_TM)
    ch = MLP_CH
    blk = pl.BlockSpec((1, tm, ch), lambda b, i: (b, i, 0))
    return _outproj((yc, yd), [blk, blk], x, g, ln_g, ln_b, w_out, False, tm)


def _rope_tables(seq):
    t = jnp.arange(seq)
    row = (t // GRID_W).astype(F32)
    col = (t % GRID_W).astype(F32)
    n_freq = ATT_HEAD_DIM // 4
    inv = ROPE_BASE ** (-jnp.arange(n_freq, dtype=F32) / n_freq)
    ang = jnp.concatenate([row[:, None] * inv, col[:, None] * inv], -1)
    cos, sin = jnp.cos(ang), jnp.sin(ang)
    reps = LANE // ATT_HEAD_DIM
    return (jnp.tile(jnp.concatenate([cos, cos], -1), (1, reps)),
            jnp.tile(jnp.concatenate([-sin, sin], -1), (1, reps)))


def _even_weights(w_in):
    o_z, o_xbc, o_dt, o_glu, o_gate = 0, 1024, 2560, 2592, 4640
    main = jnp.concatenate([w_in[:, o_z:o_xbc], w_in[:, o_glu:o_gate], w_in[:, o_gate:],
                            w_in[:, o_xbc:o_dt]], axis=1)
    w_dt = w_in[:, o_dt:o_glu][:, jnp.array(_group_dt_order(), jnp.int32)]
    return main.astype(BF16), w_dt.astype(BF16)


def _odd_weights(w_in):
    o_q, o_k, o_gd = 3072, 4096, 4608
    perm = jnp.array(_att_col_perm(), jnp.int32)
    return jnp.concatenate([w_in[:, :o_q], w_in[:, o_q:o_k][:, perm], w_in[:, o_gd:][:, perm],
                            w_in[:, o_k:o_gd]], axis=1).astype(BF16)


def _odd_out_weights(w_out):
    perm = jnp.array(_att_col_perm(), jnp.int32)
    return jnp.concatenate([w_out[:MLP_CH], w_out[MLP_CH:][perm]], axis=0).astype(BF16)


def kernel(x, c, ctx, c_ctx, mod_w, mod_b, ln_g, ln_b, ev_w_in, ev_ssd_conv_w, ev_ssd_conv_b, ev_dt_bias, ev_a_log, ev_d_skip, ev_ssd_norm, ev_cv_w, ev_cv_b, ev_cv_ln_g, ev_cv_ln_b, ev_w_out, od_w_in, od_mlp_ln_g, od_mlp_ln_b, od_ws, od_bs, od_sink, od_w_out):
    bsz, seq, d = x.shape
    cos, sin_signed = _rope_tables(seq)
    rows = -(-(bsz + 1) // SUBLANE) * SUBLANE
    cond = jnp.concatenate([c, c_ctx[None, :], jnp.zeros((rows - bsz - 1, d), F32)], axis=0)
    mod = _modulation(cond, mod_w, mod_b)

    for layer in range(DEPTH):
        last = layer == DEPTH - 1
        i = layer // 2
        m = mod[layer]
        sh_x, sc_x, g_x = (m[:bsz, None, k * d:(k + 1) * d] for k in range(3))
        sh_c, sc_c, g_c = (jnp.broadcast_to(m[bsz:bsz + 1, None, k * d:(k + 1) * d], (bsz, 1, d)) for k in range(3))
        if layer % 2 == 0:
            w_main, w_dt = _even_weights(ev_w_in[i])
            w_out = ev_w_out[i].astype(BF16)
            ssd_args = (ev_ssd_conv_w[i], ev_ssd_conv_b[i], ev_dt_bias[i], ev_a_log[i], ev_d_skip[i])
            p_c, dt_c, dtt_c = _inproj_ctx(ctx, sc_c, sh_c, w_main, w_dt)
            y_c, h_c = _ssd(p_c, dt_c, dtt_c, None, *ssd_args)
            p_x, dt_x, dtt_x = _inproj(x, sc_x, sh_x, w_main, w_dt)
            y_x, _ = _ssd(p_x, dt_x, dtt_x, h_c, *ssd_args)
            cv_x = _cvconv(p_x, ev_cv_w[i], ev_cv_b[i])
            x = _outproj_even(y_x, cv_x, p_x, ev_ssd_norm[i], ev_cv_ln_g[i], ev_cv_ln_b[i],
                              x, g_x, ln_g[layer], ln_b[layer], w_out)
            if not last:
                cv_c = _cvconv(p_c, ev_cv_w[i], ev_cv_b[i])
                ctx = _outproj_even(y_c, cv_c, p_c, ev_ssd_norm[i], ev_cv_ln_g[i], ev_cv_ln_b[i],
                                    ctx, g_c, ln_g[layer], ln_b[layer], w_out)
        else:
            w_main = _odd_weights(od_w_in[i])
            w_out = _odd_out_weights(od_w_out[i])
            if last:
                p_c = _inproj_ctx(ctx, sc_c, sh_c, w_main[:, OD_K:OD_K + INPROJ_TN], plan=(("id", 0),))[0]
                ctx_col = 0
            else:
                p_c = _inproj_ctx(ctx, sc_c, sh_c, w_main)[0]
                ctx_col = OD_K
            p_x = _inproj(x, sc_x, sh_x, w_main)[0]
            yd_x = _attention(p_x, p_c, od_sink[i], cos, sin_signed, True, ctx_col)
            yc_x = _gmlp(p_x, od_mlp_ln_g[i], od_mlp_ln_b[i], od_ws[i], od_bs[i])
            x = _outproj_odd(yc_x, yd_x, x, g_x, ln_g[layer], ln_b[layer], w_out)
            if not last:
                yd_c = _attention(p_c, p_c, od_sink[i], cos, sin_signed, False)
                yc_c = _gmlp(p_c, od_mlp_ln_g[i], od_mlp_ln_b[i], od_ws[i], od_bs[i])
                ctx = _outproj_odd(yc_c, yd_c, ctx, g_c, ln_g[layer], ln_b[layer], w_out)
    return x
```

```python
import functools
import math

import jax
import jax.numpy as jnp
from jax import lax
from jax.experimental import pallas as pl
from jax.experimental.pallas import tpu as pltpu

F32 = jnp.float32
BF16 = jnp.bfloat16

D_MODEL = 1024
DEPTH = 4
GRID_W = 64

SSD_HEADS = 16
SSD_HEAD_DIM = 64
SSD_INNER = SSD_HEADS * SSD_HEAD_DIM
SSD_GROUPS = 2
SSD_HPG = SSD_HEADS // SSD_GROUPS
SSD_STATE = 128
SSD_CHUNK = 128
SSD_CONV = 5
SSD_GROUP_CH = SSD_HPG * SSD_HEAD_DIM
CONV_CH = 1024
CONV_WIDTH = 31
MLP_CH = 1024
MLP_GROUPS = 8
MLP_GROUP_CH = MLP_CH // MLP_GROUPS
MLP_CHUNK = 128
ATT_HEADS = 16
ATT_KV_HEADS = 4
ATT_REP = ATT_HEADS // ATT_KV_HEADS
ATT_HEAD_DIM = 64
ATT_WINDOW = 128
ATT_BLOCK = 128
ATT_SCALE = ATT_HEAD_DIM ** -0.5
LOG2E = math.log2(math.e)
ROPE_BASE = 10000.0
ATT_KV_CH = ATT_KV_HEADS * ATT_HEAD_DIM
ATT_ONES = 16

DEEPNORM_ALPHA = (2 * DEPTH) ** 0.25
LN_EPS = 1e-5

PROJ_N = 5632
LANE = 128
SUBLANE = 8
VMEM_LIMIT = 56 * 1024 * 1024

EV_Z, EV_VAL, EV_GT, EV_GATE, EV_XS, EV_B, EV_C = 0, 1024, 2048, 3072, 4096, 5120, 5376
EVO_Z, EVO_GLU, EVO_GATE, EVO_XS, EVO_B, EVO_C = 0, 1024, 2048, 3072, 4096, 4352
OD_U, OD_V, OD_GC, OD_Q, OD_GD, OD_K, OD_VA = 0, 1024, 2048, 3072, 4096, 5120, 5376


def _silu(t):
    return t * jax.nn.sigmoid(t)


def _gelu_tanh(t):
    c = math.sqrt(2.0 / math.pi)
    return t * (0.5 * (1.0 + jnp.tanh(c * (t + 0.044715 * (t * t * t)))))


def _softplus(t):
    return jnp.maximum(t, 0.0) + jnp.log1p(jnp.exp(-jnp.abs(t)))


def _layer_norm(t, g, b):
    mu = jnp.mean(t, -1, keepdims=True)
    d = t - mu
    var = jnp.mean(d * d, -1, keepdims=True)
    return d * lax.rsqrt(var + LN_EPS) * g + b


def _params(*sem):
    return pltpu.CompilerParams(dimension_semantics=sem, vmem_limit_bytes=VMEM_LIMIT)


def _mod_kernel(c_ref, w_ref, b_ref, o_ref):
    s = _silu(c_ref[...]).astype(BF16)
    o_ref[0] = jnp.dot(s, w_ref[0].astype(BF16), preferred_element_type=F32) + b_ref[0]


def _modulation(cond, mod_w, mod_b):
    rows = cond.shape[0]
    d = D_MODEL
    return pl.pallas_call(
        _mod_kernel,
        out_shape=jax.ShapeDtypeStruct((DEPTH, rows, 3 * d), F32),
        grid=(DEPTH, 3),
        in_specs=[pl.BlockSpec((rows, d), lambda l, j: (0, 0)),
                  pl.BlockSpec((1, d, d), lambda l, j: (l, 0, j)),
                  pl.BlockSpec((1, 1, d), lambda l, j: (l, 0, j))],
        out_specs=pl.BlockSpec((1, rows, d), lambda l, j: (l, 0, j)),
        compiler_params=_params("arbitrary", "arbitrary"),
        name="modulation",
    )(cond, mod_w, mod_b.reshape(DEPTH, 1, 3 * d))


INPROJ_TM = 1024
INPROJ_SUB = 512
INPROJ_TN = 512


def _tile_plan(even):
    t = lambda col: col // INPROJ_TN
    if even:
        plan = [("silu", t(EV_Z) + i) for i in range(2)]
        plan += [("glu", t(EV_VAL) + i, t(EV_GT) + i) for i in range(2)]
        plan += [("silu", t(EV_GATE) + i) for i in range(2)]
        plan += [("id", t(EV_XS) + i) for i in range(3)]
    else:
        plan = [("gelu", t(OD_U) + i) for i in range(4)]
        plan += [("silu", t(OD_GC) + i) for i in range(2)]
        plan += [("id", t(OD_Q) + i) for i in range(2)]
        plan += [("silu", t(OD_GD) + i) for i in range(2)]
        plan += [("id", t(OD_K))]
    return plan


def _inproj_kernel(x_ref, sc_ref, sh_ref, w_ref, *rest, even, plan):
    if even:
        wdt_ref, wdtt_ref, o_ref, dt_ref, dtt_ref, h_ref = rest
    else:
        o_ref, h_ref = rest
    tm = x_ref.shape[1]
    sub = min(INPROJ_SUB, tm)

    def modulate(s):
        rows = pl.ds(s * sub, sub)
        h_ref[rows, :] = (x_ref[0, rows, :] * (1.0 + sc_ref[0]) + sh_ref[0]).astype(BF16)

    modulate(0)
    for s in range(tm // sub):
        if (s + 1) * sub < tm:
            modulate(s + 1)
        rows = pl.ds(s * sub, sub)
        if even:
            dt = jnp.dot(h_ref[rows, :], wdt_ref[...], preferred_element_type=F32)
            dtt = lax.dot_general(wdtt_ref[...], h_ref[rows, :], (((1,), (1,)), ((), ())),
                                  preferred_element_type=F32)
            per = dt.shape[1] // SSD_GROUPS
            for gi in range(SSD_GROUPS):
                dt_ref[0, gi, rows, :] = dt[:, gi * per:(gi + 1) * per]
                dtt_ref[0, gi, :, rows] = dtt[gi * per:(gi + 1) * per, :]
        tile = lambda j: jnp.dot(h_ref[rows, :], w_ref[:, pl.ds(j * INPROJ_TN, INPROJ_TN)],
                                 preferred_element_type=F32)
        for out_j, (kind, *src) in enumerate(plan):
            r = tile(src[0])
            if kind == "silu":
                r = _silu(r)
            elif kind == "gelu":
                r = _gelu_tanh(r)
            elif kind == "glu":
                r = r * jax.nn.sigmoid(tile(src[1]))
            o_ref[0, rows, pl.ds(out_j * INPROJ_TN, INPROJ_TN)] = r.astype(BF16)


def _inproj(x, sc, sh, w, w_dt=None, plan=None):
    bsz, seq, d = x.shape
    n = w.shape[1]
    tm = min(seq, INPROJ_TM)
    even = w_dt is not None
    plan = tuple(_tile_plan(even)) if plan is None else plan
    n_out = len(plan) * INPROJ_TN
    resident = dict(pipeline_mode=pl.Buffered(1))
    in_specs = [pl.BlockSpec((1, tm, d), lambda b, i: (b, i, 0)),
                pl.BlockSpec((1, 1, d), lambda b, i: (b, 0, 0)),
                pl.BlockSpec((1, 1, d), lambda b, i: (b, 0, 0)),
                pl.BlockSpec((d, n), lambda b, i: (0, 0), **resident)]
    out_shape = [jax.ShapeDtypeStruct((bsz, seq, n_out), BF16)]
    out_specs = [pl.BlockSpec((1, tm, n_out), lambda b, i: (b, i, 0))]
    args = [x, sc, sh, w]
    if even:
        ndt = w_dt.shape[1]
        per = ndt // SSD_GROUPS
        in_specs += [pl.BlockSpec((d, ndt), lambda b, i: (0, 0)),
                     pl.BlockSpec((ndt, d), lambda b, i: (0, 0))]
        out_shape += [jax.ShapeDtypeStruct((bsz, SSD_GROUPS, seq, per), F32),
                      jax.ShapeDtypeStruct((bsz, SSD_GROUPS, per, seq), F32)]
        out_specs += [pl.BlockSpec((1, SSD_GROUPS, tm, per), lambda b, i: (b, 0, i, 0)),
                      pl.BlockSpec((1, SSD_GROUPS, per, tm), lambda b, i: (b, 0, 0, i))]
        args += [w_dt, w_dt.T]
    return pl.pallas_call(
        functools.partial(_inproj_kernel, even=even, plan=plan),
        out_shape=out_shape,
        grid=(bsz, seq // tm),
        in_specs=in_specs,
        out_specs=out_specs,
        scratch_shapes=[pltpu.VMEM((tm, d), BF16)],
        compiler_params=_params("arbitrary", "arbitrary"),
        name="inproj",
    )(*args)


def _inproj_ctx(ctx, sc, sh, w, w_dt=None, plan=None):
    bsz, clen, d = ctx.shape
    outs = _inproj(ctx.reshape(1, bsz * clen, d), sc[:1], sh[:1], w, w_dt, plan)
    proj = outs[0].reshape(bsz, clen, -1)
    if w_dt is None:
        return (proj,)
    per = outs[1].shape[-1]
    dt = outs[1].reshape(SSD_GROUPS, bsz, clen, per).transpose(1, 0, 2, 3)
    dtt = outs[2].reshape(SSD_GROUPS, per, bsz, clen).transpose(2, 0, 1, 3)
    return proj, dt, dtt


def _conv_silu(src_ref, w_ref, b_ref, pad_ref, dst_ref, seq):
    ch = src_ref.shape[-1]
    halo = SUBLANE
    pad_ref[pl.ds(0, halo), :] = jnp.zeros((halo, ch), F32)
    pad_ref[pl.ds(seq + halo, halo), :] = jnp.zeros((halo, ch), F32)
    pad_ref[pl.ds(halo, seq), :] = src_ref[0].astype(F32)
    w = w_ref[...]
    bias = b_ref[...]
    first = halo - SSD_CONV // 2

    def body(t, carry):
        base = pl.multiple_of(t * SSD_CHUNK, SSD_CHUNK)
        win = pad_ref[pl.ds(base, SSD_CHUNK + 2 * halo), :]
        acc = bias
        for k in range(SSD_CONV):
            acc = acc + w[k:k + 1, :] * win[first + k:first + k + SSD_CHUNK, :]
        dst_ref[pl.ds(base, SSD_CHUNK), :] = _silu(acc).astype(dst_ref.dtype)
        return carry

    lax.fori_loop(0, seq // SSD_CHUNK, body, 0)


def _pair_expand(vals, col0):
    q = vals.shape[0]
    lane = lax.broadcasted_iota(jnp.int32, (q, LANE), 1)
    parts = []
    for k in range(SSD_HPG // 2):
        a = jnp.broadcast_to(vals[:, col0 + 2 * k:col0 + 2 * k + 1], (q, LANE))
        b = jnp.broadcast_to(vals[:, col0 + 2 * k + 1:col0 + 2 * k + 2], (q, LANE))
        parts.append(jnp.where(lane < SSD_HEAD_DIM, a, b))
    return jnp.concatenate(parts, axis=-1)


def _split_dot(lhs, rhs):
    if lhs.dtype == F32:
        hi = lhs.astype(BF16)
        lo = (lhs - hi.astype(F32)).astype(BF16)
        return (jnp.dot(hi, rhs, preferred_element_type=F32)
                + jnp.dot(lo, rhs, preferred_element_type=F32))
    hi = rhs.astype(BF16)
    lo = (rhs - hi.astype(F32)).astype(BF16)
    return (jnp.dot(lhs, hi, preferred_element_type=F32)
            + jnp.dot(lhs, lo, preferred_element_type=F32))


def _ssd_kernel(xs_ref, bm_ref, cm_ref, dt_ref, dtt_ref,
                wx_ref, wb_ref, wc_ref, bx_ref, bb_ref, bc_ref,
                dtb_row_ref, dtb_col_ref, alog_row_ref, alog_col_ref, dskip_ref, *rest, seq, zero_init):
    h0_ref = None if zero_init else rest[0]
    (y_ref, hout_ref, padx_ref, padn_ref, xc_ref, bcs_ref, ccs_ref,
     yfwd_ref, ybwd_ref, state_ref) = rest[(0 if zero_init else 1):]
    q = SSD_CHUNK
    n_chunks = seq // q
    nh = 2 * SSD_HPG

    _conv_silu(xs_ref, wx_ref, bx_ref, padx_ref, xc_ref, seq)
    _conv_silu(bm_ref, wb_ref, bb_ref, padn_ref, bcs_ref, seq)
    _conv_silu(cm_ref, wc_ref, bc_ref, padn_ref, ccs_ref, seq)

    a_row = -jnp.exp(alog_row_ref[0])
    a_col = -jnp.exp(alog_col_ref[0])
    dtb_row = dtb_row_ref[0]
    dtb_col = dtb_col_ref[0]
    row_i = lax.broadcasted_iota(jnp.int32, (q, q), 0)
    col_i = lax.broadcasted_iota(jnp.int32, (q, q), 1)
    lower = row_i >= col_i
    upper = row_i <= col_i
    tri_lo = jnp.where(lower, 1.0, 0.0).astype(BF16)
    tri_up = jnp.where(upper, 1.0, 0.0).astype(BF16)
    lane = lax.broadcasted_iota(jnp.int32, (q, LANE), 1)
    head_lo = lane < SSD_HEAD_DIM

    def prologue(t, direction):
        rows = pl.ds(pl.multiple_of(t * q, q), q)
        dt = _softplus(dt_ref[0, 0, rows, :] + dtb_row)
        dtt = _softplus(dtt_ref[0, 0, :, rows] + dtb_col)
        da = dt * a_row
        dat = dtt * a_col
        if direction == 0:
            cum = _split_dot(tri_lo, da)
            cumt = _split_dot(dat, tri_up)
        else:
            cum = _split_dot(tri_up, da)
            cumt = _split_dot(dat, tri_lo)
        edge = q - 1 if direction == 0 else 0
        wt = dtt * jnp.exp(cumt[:, edge:edge + 1] - cumt)
        bc = bcs_ref[rows, :]
        cc = ccs_ref[rows, :]
        cb = lax.dot_general(cc, bc, (((1,), (1,)), ((), ())), preferred_element_type=F32)
        bctf = bc.astype(F32).T
        return cum, cumt - jnp.log(dtt), wt, cb, bctf

    def chunk(t, direction, pro):
        cum, rowt, wt, cb, bctf = pro
        rows = pl.ds(pl.multiple_of(t * q, q), q)
        mask = lower if direction == 0 else upper
        edge = q - 1 if direction == 0 else 0
        total = cum[edge:edge + 1, :]
        col0 = direction * SSD_HPG
        xb = xc_ref[rows, :]
        ccf = ccs_ref[rows, :].astype(F32)
        st = state_ref[direction]
        stb = st.astype(BF16)
        zero = jnp.zeros((q, LANE), BF16)
        ys, news = [], []
        for k in range(SSD_HPG // 2):
            xp = xb[:, k * LANE:(k + 1) * LANE]
            sp = stb[:, k * LANE:(k + 1) * LANE]
            rhs_x = jnp.concatenate([jnp.where(head_lo, xp, zero), jnp.where(head_lo, zero, xp)], axis=0)
            rhs_s = jnp.concatenate([jnp.where(head_lo, sp, zero), jnp.where(head_lo, zero, sp)], axis=0)
            ms, cs, ws = [], [], []
            for half in range(2):
                c = col0 + 2 * k + half
                bcol = jnp.broadcast_to(cum[:, c:c + 1], (q, q))
                seg = bcol - jnp.broadcast_to(rowt[c:c + 1, :], (q, q))
                ms.append((cb * jnp.exp(jnp.where(mask, seg, -jnp.inf))).astype(BF16))
                cs.append((ccf * jnp.exp(bcol)).astype(BF16))
                ws.append((bctf * jnp.broadcast_to(wt[c:c + 1, :], (q, q))).astype(BF16))
            ys.append(jnp.dot(jnp.concatenate(ms + cs, axis=1), jnp.concatenate([rhs_x, rhs_s], axis=0),
                              preferred_element_type=F32))
            news.append(jnp.dot(jnp.concatenate(ws, axis=1), rhs_x, preferred_element_type=F32))
        state_ref[direction] = st * _pair_expand(jnp.exp(total), col0) + jnp.concatenate(news, axis=-1)
        return rows, jnp.concatenate(ys, axis=-1)

    state_ref[...] = jnp.zeros(state_ref.shape, F32) if zero_init else h0_ref[0, :, 0]

    def both(i, carry):
        pro_f, pro_b = carry
        nxt_f = prologue(jnp.minimum(i + 1, n_chunks - 1), 0)
        nxt_b = prologue(jnp.maximum(n_chunks - 2 - i, 0), 1)
        rows, y = chunk(i, 0, pro_f)
        yfwd_ref[rows, :] = y
        rows, y = chunk(n_chunks - 1 - i, 1, pro_b)
        ybwd_ref[rows, :] = y
        return nxt_f, nxt_b

    lax.fori_loop(0, n_chunks, both, (prologue(0, 0), prologue(n_chunks - 1, 1)))
    hout_ref[0, :, 0] = state_ref[...]

    dskip = dskip_ref[...]

    def finish(t, carry):
        rows = pl.ds(pl.multiple_of(t * q, q), q)
        y = yfwd_ref[rows, :] + ybwd_ref[rows, :] + xc_ref[rows, :].astype(F32) * dskip
        y_ref[0, rows, :] = y.astype(y_ref.dtype)
        return carry

    lax.fori_loop(0, n_chunks, finish, 0)


def _group_dt_order():
    order = []
    for g in range(SSD_GROUPS):
        for direction in range(2):
            order += [direction * SSD_HEADS + g * SSD_HPG + r for r in range(SSD_HPG)]
    return order


def _ssd(proj, dt, dtt, h0, conv_w, conv_b, dt_bias, a_log, d_skip):
    bsz, seq, _ = proj.shape
    gc = SSD_GROUP_CH
    ns = SSD_STATE
    nh = 2 * SSD_HPG
    xs_blk, b_blk, c_blk = EVO_XS // gc, EVO_B // ns, EVO_C // ns
    cw_b0, cw_c0 = SSD_INNER // ns, (SSD_INNER + SSD_GROUPS * ns) // ns
    conv_b2 = conv_b.reshape(1, -1)
    d_skip_x = jnp.repeat(d_skip, SSD_HEAD_DIM).reshape(1, SSD_INNER)
    order = jnp.array(_group_dt_order(), jnp.int32)
    dt_bias_g = dt_bias.reshape(-1)[order].reshape(SSD_GROUPS, nh)
    a_log_g = a_log.reshape(-1)[order].reshape(SSD_GROUPS, nh)
    in_specs = [
        pl.BlockSpec((1, seq, gc), lambda b, g: (b, 0, xs_blk + g)),
        pl.BlockSpec((1, seq, ns), lambda b, g: (b, 0, b_blk + g)),
        pl.BlockSpec((1, seq, ns), lambda b, g: (b, 0, c_blk + g)),
        pl.BlockSpec((1, 1, seq, nh), lambda b, g: (b, g, 0, 0)),
        pl.BlockSpec((1, 1, nh, seq), lambda b, g: (b, g, 0, 0)),
        pl.BlockSpec((SSD_CONV, gc), lambda b, g: (0, g)),
        pl.BlockSpec((SSD_CONV, ns), lambda b, g: (0, cw_b0 + g)),
        pl.BlockSpec((SSD_CONV, ns), lambda b, g: (0, cw_c0 + g)),
        pl.BlockSpec((1, gc), lambda b, g: (0, g)),
        pl.BlockSpec((1, ns), lambda b, g: (0, cw_b0 + g)),
        pl.BlockSpec((1, ns), lambda b, g: (0, cw_c0 + g)),
        pl.BlockSpec((1, 1, nh), lambda b, g: (g, 0, 0)),
        pl.BlockSpec((1, nh, 1), lambda b, g: (g, 0, 0)),
        pl.BlockSpec((1, 1, nh), lambda b, g: (g, 0, 0)),
        pl.BlockSpec((1, nh, 1), lambda b, g: (g, 0, 0)),
        pl.BlockSpec((1, gc), lambda b, g: (0, g)),
    ]
    state_spec = pl.BlockSpec((1, 2, 1, ns, gc), lambda b, g: (b, 0, g, 0, 0))
    args = [proj, proj, proj, dt, dtt,
            conv_w, conv_w, conv_w, conv_b2, conv_b2, conv_b2,
            dt_bias_g.reshape(SSD_GROUPS, 1, nh), dt_bias_g.reshape(SSD_GROUPS, nh, 1),
            a_log_g.reshape(SSD_GROUPS, 1, nh), a_log_g.reshape(SSD_GROUPS, nh, 1), d_skip_x]
    if h0 is not None:
        in_specs.append(state_spec)
        args.append(h0)
    return pl.pallas_call(
        functools.partial(_ssd_kernel, seq=seq, zero_init=h0 is None),
        out_shape=[jax.ShapeDtypeStruct((bsz, seq, SSD_INNER), BF16),
                   jax.ShapeDtypeStruct((bsz, 2, SSD_GROUPS, ns, gc), F32)],
        grid=(bsz, SSD_GROUPS),
        in_specs=in_specs,
        out_specs=[pl.BlockSpec((1, seq, gc), lambda b, g: (b, 0, g)), state_spec],
        scratch_shapes=[pltpu.VMEM((seq + 2 * SUBLANE, gc), F32),
                        pltpu.VMEM((seq + 2 * SUBLANE, ns), F32),
                        pltpu.VMEM((seq, gc), BF16),
                        pltpu.VMEM((seq, ns), BF16),
                        pltpu.VMEM((seq, ns), BF16),
                        pltpu.VMEM((seq, gc), F32),
                        pltpu.VMEM((seq, gc), F32),
                        pltpu.VMEM((2, ns, gc), F32)],
        compiler_params=_params("arbitrary", "arbitrary"),
        name="ssd",
    )(*args)


CV_ROWS = 256
CV_HALO = 16
CV_TC = 128
CV_SPAN = CV_ROWS + 2 * CV_HALO - SUBLANE


def _cvconv_kernel(glu_ref, w_ref, b_ref, o_ref, pad_ref, sh_ref, *, seq):
    ch = glu_ref.shape[-1]
    pad_ref[pl.ds(0, CV_HALO), :] = jnp.zeros((CV_HALO, ch), F32)
    pad_ref[pl.ds(seq + CV_HALO, CV_HALO), :] = jnp.zeros((CV_HALO, ch), F32)
    pad_ref[pl.ds(CV_HALO, seq), :] = glu_ref[0].astype(F32)
    w = w_ref[...]
    bias = b_ref[...]
    first = CV_HALO - CONV_WIDTH // 2
    rows = min(CV_ROWS * CV_TC // ch, seq)
    span = rows + 2 * CV_HALO - SUBLANE

    def body(t, carry):
        base = pl.multiple_of(t * rows, rows)
        win = pad_ref[pl.ds(base, rows + 2 * CV_HALO), :]
        for s in range(SUBLANE):
            sh_ref[s, pl.ds(0, span), :] = win[s:s + span, :]
        acc = jnp.broadcast_to(bias, (rows, ch))
        for k in range(CONV_WIDTH):
            a, s = divmod(first + k, SUBLANE)
            acc = acc + w[k:k + 1, :] * sh_ref[s, pl.ds(a * SUBLANE, rows), :]
        o_ref[0, pl.ds(base, rows), :] = acc.astype(o_ref.dtype)
        return carry

    lax.fori_loop(0, seq // rows, body, 0)


def _cvconv(proj, cv_w, cv_b):
    bsz, seq, _ = proj.shape
    tc = CV_TC * max(1, 1024 // seq)
    return pl.pallas_call(
        functools.partial(_cvconv_kernel, seq=seq),
        out_shape=jax.ShapeDtypeStruct((bsz, seq, CONV_CH), BF16),
        grid=(bsz, CONV_CH // tc),
        in_specs=[pl.BlockSpec((1, seq, tc), lambda b, j: (b, 0, EVO_GLU // tc + j)),
                  pl.BlockSpec((CONV_WIDTH, tc), lambda b, j: (0, j)),
                  pl.BlockSpec((1, tc), lambda b, j: (0, j))],
        out_specs=pl.BlockSpec((1, seq, tc), lambda b, j: (b, 0, j)),
        scratch_shapes=[pltpu.VMEM((seq + 2 * CV_HALO, tc), F32),
                        pltpu.VMEM((SUBLANE, CV_SPAN, tc), F32)],
        compiler_params=_params("arbitrary", "arbitrary"),
        name="cvconv",
    )(proj, cv_w, cv_b.reshape(1, CONV_CH))


def _gmlp_rows(u_ref, v_ref, gc_ref, lng_ref, lnb_ref, ws_ref, bias_ref, row0, n_rows):
    lng = lng_ref[...]
    lnb = lnb_ref[...]
    bias = bias_ref[...]
    out = []
    for c in range(n_rows // MLP_CHUNK):
        rows = pl.ds(row0 + c * MLP_CHUNK, MLP_CHUNK)
        vn = _layer_norm(v_ref[0, rows, :].astype(F32), lng, lnb).astype(BF16)
        mixed = jnp.concatenate(
            [jnp.dot(ws_ref[gi], vn[:, gi * MLP_GROUP_CH:(gi + 1) * MLP_GROUP_CH], preferred_element_type=F32)
             for gi in range(MLP_GROUPS)], axis=-1)
        u = u_ref[0, rows, :].astype(F32)
        out.append((u * (mixed + bias) * gc_ref[0, rows, :].astype(F32)).astype(BF16))
    return jnp.concatenate(out, axis=0)


def _rope(t, cos, sin_signed):
    width = t.shape[1]
    reps = width // LANE
    half = ATT_HEAD_DIM // 2
    lane = lax.broadcasted_iota(jnp.int32, t.shape, 1)
    first_half = (lane % ATT_HEAD_DIM) < half
    swapped = jnp.where(first_half, pltpu.roll(t, width - half, 1), pltpu.roll(t, half, 1))
    c = jnp.concatenate([cos] * reps, axis=-1)
    s = jnp.concatenate([sin_signed] * reps, axis=-1)
    return t * c + swapped * s


def _att_head_order():
    order = []
    for c in range(ATT_HEADS // 2):
        j, r = divmod(c, ATT_REP)
        order += [(2 * j) * ATT_REP + r, (2 * j + 1) * ATT_REP + r]
    return order


def _att_col_perm():
    return [h * ATT_HEAD_DIM + d for h in _att_head_order() for d in range(ATT_HEAD_DIM)]


def _attn_kernel(*refs, seq, ctx_len, latent):
    if latent:
        (q_ref, gd_ref, k_ref, v_ref, kc_ref, vc_ref, sink_ref, cos_ref, sin_ref,
         o_ref, kpad_ref, vtp_ref, vct_ref) = refs
    else:
        q_ref, gd_ref, kc_ref, vc_ref, sink_ref, o_ref, vct_ref = refs
    i = pl.program_id(1)
    w = ATT_BLOCK
    nt = (((1,), (1,)), ((), ()))

    vrows = LANE + ATT_ONES

    @pl.when(i == 0)
    def _():
        vct = vc_ref[0].astype(F32).T.astype(BF16)
        for j in range(ATT_KV_CH // LANE):
            vct_ref[pl.ds(j * vrows, LANE), :] = vct[j * LANE:(j + 1) * LANE, :]
            vct_ref[pl.ds(j * vrows + LANE, ATT_ONES), :] = jnp.ones((ATT_ONES, ctx_len), BF16)
        if latent:
            kpad_ref[pl.ds(0, w), :] = jnp.zeros((w, ATT_KV_CH), BF16)
            kpad_ref[pl.ds(seq + w, w), :] = jnp.zeros((w, ATT_KV_CH), BF16)
            kpad_ref[pl.ds(w, seq), :] = _rope(k_ref[0].astype(F32), cos_ref[...], sin_ref[...]).astype(BF16)
            for j in range(ATT_KV_CH // LANE):
                vtp_ref[pl.ds(j * vrows, LANE), pl.ds(0, w)] = jnp.zeros((LANE, w), BF16)
                vtp_ref[pl.ds(j * vrows, LANE), pl.ds(seq + w, w)] = jnp.zeros((LANE, w), BF16)
                vtp_ref[pl.ds(j * vrows + LANE, ATT_ONES), :] = jnp.ones((ATT_ONES, seq + 2 * w), BF16)
            for t in range(seq // w):
                vt = v_ref[0, pl.ds(t * w, w), :].astype(F32).T.astype(BF16)
                for j in range(ATT_KV_CH // LANE):
                    vtp_ref[pl.ds(j * vrows, LANE), pl.ds((t + 1) * w, w)] = vt[j * LANE:(j + 1) * LANE, :]

    if latent:
        base = pl.multiple_of(i * w, w)
        q = _rope(q_ref[0].astype(F32), cos_ref[pl.ds(base, w), :], sin_ref[pl.ds(base, w), :])
        kwin = kpad_ref[pl.ds(base, 3 * w), :]
        qi = lax.broadcasted_iota(jnp.int32, (w, w), 1)
        u = lax.broadcasted_iota(jnp.int32, (w, w), 0)
        before = (u - w >= qi - ATT_WINDOW) & (base + u - w >= 0)
        after = (u + w <= qi + ATT_WINDOW) & (base + u + w < seq)
        bias_lo = jnp.where(before, 0.0, -jnp.inf)
        bias_hi = jnp.where(after, 0.0, -jnp.inf)
        bias_lo = jnp.concatenate([bias_lo, bias_lo], axis=1)
        bias_hi = jnp.concatenate([bias_hi, bias_hi], axis=1)
    else:
        q = q_ref[0].astype(F32)
    qs = (q * (ATT_SCALE * LOG2E)).astype(BF16)
    kc = kc_ref[0]
    lane = lax.broadcasted_iota(jnp.int32, (w, LANE), 1)
    low_lanes = lane < ATT_HEAD_DIM
    low_rows = lax.broadcasted_iota(jnp.int32, (LANE, w), 0) < ATT_HEAD_DIM
    zero = jnp.zeros((w, LANE), BF16)
    n_cols = ATT_HEADS // 2

    def scores(c):
        kv = slice((c // ATT_REP) * LANE, (c // ATT_REP + 1) * LANE)
        qc = qs[:, c * LANE:(c + 1) * LANE]
        rhs = jnp.concatenate([jnp.where(low_lanes, qc, zero), jnp.where(low_lanes, zero, qc)], axis=0)
        s_ctx = lax.dot_general(kc[:, kv], rhs, nt, preferred_element_type=F32)
        s_lat = lax.dot_general(kwin[:, kv], rhs, nt, preferred_element_type=F32) if latent else None
        return s_ctx, s_lat

    def finish(c, s_ctx, s_lat):
        j = c // ATT_REP
        snk = sink_ref[0, :, pl.ds(c * 2 * w, 2 * w)]
        m = jnp.maximum(jnp.max(s_ctx, axis=0, keepdims=True), snk)
        if latent:
            s_lo = s_lat[:w] + bias_lo
            s_mid = s_lat[w:2 * w]
            s_hi = s_lat[2 * w:] + bias_hi
            m = jnp.maximum(m, jnp.maximum(jnp.maximum(jnp.max(s_lo, axis=0, keepdims=True),
                                                       jnp.max(s_mid, axis=0, keepdims=True)),
                                           jnp.max(s_hi, axis=0, keepdims=True)))
        p_ctx = jnp.exp2(s_ctx - m).astype(BF16)
        acc = jnp.dot(vct_ref[pl.ds(j * vrows, vrows), :], p_ctx, preferred_element_type=F32)
        if latent:
            p_lat = jnp.concatenate([jnp.exp2(s_lo - m), jnp.exp2(s_mid - m), jnp.exp2(s_hi - m)],
                                    axis=0).astype(BF16)
            acc = acc + jnp.dot(vtp_ref[pl.ds(j * vrows, vrows), pl.ds(base, 3 * w)], p_lat,
                                preferred_element_type=F32)
        den = acc[LANE:LANE + 1, :] + jnp.exp2(snk - m)
        acc = acc[:LANE, :] * (1.0 / den)
        return jnp.where(low_rows, acc[:, :w], acc[:, w:]).T

    outs = []
    ahead = 3
    pending = [scores(c) for c in range(ahead)]
    for c in range(n_cols):
        if c + ahead < n_cols:
            pending.append(scores(c + ahead))
        outs.append(finish(c, *pending.pop(0)))
    gd = gd_ref[0].astype(F32)
    o_ref[0] = (jnp.concatenate(outs, axis=-1) * gd).astype(o_ref.dtype)


def _attention(proj, proj_ctx, sink, cos, sin_signed, latent, ctx_col=OD_K):
    bsz, seq, _ = proj.shape
    ctx_len = proj_ctx.shape[1]
    w = ATT_BLOCK
    qch = ATT_HEADS * ATT_HEAD_DIM
    kvc = ATT_KV_CH
    q_spec = pl.BlockSpec((1, w, qch), lambda b, i: (b, i, OD_Q // qch))
    gd_spec = pl.BlockSpec((1, w, qch), lambda b, i: (b, i, OD_GD // qch))
    kc_spec = pl.BlockSpec((1, ctx_len, kvc), lambda b, i: (b, 0, ctx_col // kvc))
    vc_spec = pl.BlockSpec((1, ctx_len, kvc), lambda b, i: (b, 0, ctx_col // kvc + 1))
    sink2 = jnp.repeat(sink[jnp.array(_att_head_order(), jnp.int32)] * LOG2E, w).reshape(1, 1, ATT_HEADS * w)
    sink_spec = pl.BlockSpec((1, 1, ATT_HEADS * w), lambda b, i: (0, 0, 0))
    vrows = (kvc // LANE) * (LANE + ATT_ONES)
    vct = pltpu.VMEM((vrows, ctx_len), BF16)
    if latent:
        in_specs = [q_spec, gd_spec,
                    pl.BlockSpec((1, seq, kvc), lambda b, i: (b, 0, OD_K // kvc)),
                    pl.BlockSpec((1, seq, kvc), lambda b, i: (b, 0, OD_VA // kvc)),
                    kc_spec, vc_spec, sink_spec,
                    pl.BlockSpec((seq, LANE), lambda b, i: (0, 0)),
                    pl.BlockSpec((seq, LANE), lambda b, i: (0, 0))]
        args = (proj, proj, proj, proj, proj_ctx, proj_ctx, sink2, cos, sin_signed)
        scratch = [pltpu.VMEM((seq + 2 * w, kvc), BF16), pltpu.VMEM((vrows, seq + 2 * w), BF16), vct]
    else:
        in_specs = [q_spec, gd_spec, kc_spec, vc_spec, sink_spec]
        args = (proj, proj, proj_ctx, proj_ctx, sink2)
        scratch = [vct]
    return pl.pallas_call(
        functools.partial(_attn_kernel, seq=seq, ctx_len=ctx_len, latent=latent),
        out_shape=jax.ShapeDtypeStruct((bsz, seq, qch), BF16),
        grid=(bsz, seq // w),
        in_specs=in_specs,
        out_specs=pl.BlockSpec((1, w, qch), lambda b, i: (b, i, 0)),
        scratch_shapes=scratch,
        compiler_params=_params("arbitrary", "arbitrary"),
        name="attention" if latent else "ctx_attention",
    )(*args)


OUTPROJ_TM = 1024
OUTPROJ_SUB = 512


def _outproj_kernel(*refs, even):
    if even:
        (y_ref, z_ref, cv_ref, gate_ref, nw_ref, cvg_ref, cvb_ref,
         x_ref, g_ref, lng_ref, lnb_ref, w_ref, o_ref) = refs
    else:
        (u_ref, v_ref, gc_ref, mlng_ref, mlnb_ref, ws_ref, bias_ref, yb_ref,
         x_ref, g_ref, lng_ref, lnb_ref, w_ref, o_ref) = refs
    half = w_ref.shape[0] // 2
    tm = x_ref.shape[1]
    sub = min(OUTPROJ_SUB, tm)

    def project(k):
        rows = pl.ds(k * sub, sub)
        if even:
            t = y_ref[0, rows, :].astype(F32) * z_ref[0, rows, :].astype(F32)
            ya = (t * lax.rsqrt(jnp.mean(t * t, -1, keepdims=True) + LN_EPS) * nw_ref[...]).astype(BF16)
            yb = (_silu(_layer_norm(cv_ref[0, rows, :].astype(F32), cvg_ref[...], cvb_ref[...]))
                  * gate_ref[0, rows, :].astype(F32)).astype(BF16)
        else:
            ya = _gmlp_rows(u_ref, v_ref, gc_ref, mlng_ref, mlnb_ref, ws_ref, bias_ref, k * sub, sub)
            yb = yb_ref[0, rows, :]
        return (jnp.dot(ya, w_ref[pl.ds(0, half), :], preferred_element_type=F32)
                + jnp.dot(yb, w_ref[pl.ds(half, half), :], preferred_element_type=F32))

    n_sub = tm // sub
    y = project(0)
    for k in range(n_sub):
        nxt = project(k + 1) if k + 1 < n_sub else None
        rows = pl.ds(k * sub, sub)
        r = DEEPNORM_ALPHA * x_ref[0, rows, :] + g_ref[0] * y
        o_ref[0, rows, :] = _layer_norm(r, lng_ref[...], lnb_ref[...])
        y = nxt


def _outproj(mix_args, mix_specs, x, g, ln_g, ln_b, w_out, even, tm):
    bsz, seq, d = x.shape
    vec = pl.BlockSpec((1, d), lambda b, i: (0, 0))
    in_specs = list(mix_specs) + [
        pl.BlockSpec((1, tm, d), lambda b, i: (b, i, 0)),
        pl.BlockSpec((1, 1, d), lambda b, i: (b, 0, 0)),
        vec, vec,
        pl.BlockSpec(w_out.shape, lambda b, i: (0, 0))]
    return pl.pallas_call(
        functools.partial(_outproj_kernel, even=even),
        out_shape=jax.ShapeDtypeStruct((bsz, seq, d), F32),
        grid=(bsz, seq // tm),
        in_specs=in_specs,
        out_specs=pl.BlockSpec((1, tm, d), lambda b, i: (b, i, 0)),
        compiler_params=_params("arbitrary", "arbitrary"),
        name="outproj_even" if even else "outproj_odd",
    )(*mix_args, x, g, ln_g.reshape(1, d), ln_b.reshape(1, d), w_out)


def _outproj_even(y_ssd, cv, proj, norm_w, cv_ln_g, cv_ln_b, x, g, ln_g, ln_b, w_out):
    seq = x.shape[1]
    tm = min(seq, OUTPROJ_TM)
    ch = SSD_INNER
    blk = lambda col: pl.BlockSpec((1, tm, ch), lambda b, i: (b, i, col // ch))
    vec = pl.BlockSpec((1, ch), lambda b, i: (0, 0))
    specs = [blk(0), blk(EVO_Z), blk(0), blk(EVO_GATE), vec, vec, vec]
    args = (y_ssd, proj, cv, proj, norm_w.reshape(1, ch), cv_ln_g.reshape(1, ch), cv_ln_b.reshape(1, ch))
    return _outproj(args, specs, x, g, ln_g, ln_b, w_out, True, tm)


def _outproj_odd(proj, yd, mlp_ln_g, mlp_ln_b, ws, bs, x, g, ln_g, ln_b, w_out):
    seq = x.shape[1]
    tm = min(seq, OUTPROJ_TM)
    ch = MLP_CH
    blk = lambda col: pl.BlockSpec((1, tm, ch), lambda b, i: (b, i, col // ch))
    vec = pl.BlockSpec((1, ch), lambda b, i: (0, 0))
    bias = jnp.repeat(bs.T, MLP_GROUP_CH, axis=1)
    specs = [blk(OD_U), blk(OD_V), blk(OD_GC), vec, vec,
             pl.BlockSpec((MLP_GROUPS, MLP_CHUNK, MLP_CHUNK), lambda b, i: (0, 0, 0)),
             pl.BlockSpec((MLP_CHUNK, ch), lambda b, i: (0, 0)), blk(0)]
    args = (proj, proj, proj, mlp_ln_g.reshape(1, ch), mlp_ln_b.reshape(1, ch), ws.astype(BF16), bias, yd)
    return _outproj(args, specs, x, g, ln_g, ln_b, w_out, False, tm)


def _rope_tables(seq):
    t = jnp.arange(seq)
    row = (t // GRID_W).astype(F32)
    col = (t % GRID_W).astype(F32)
    n_freq = ATT_HEAD_DIM // 4
    inv = ROPE_BASE ** (-jnp.arange(n_freq, dtype=F32) / n_freq)
    ang = jnp.concatenate([row[:, None] * inv, col[:, None] * inv], -1)
    cos, sin = jnp.cos(ang), jnp.sin(ang)
    reps = LANE // ATT_HEAD_DIM
    return (jnp.tile(jnp.concatenate([cos, cos], -1), (1, reps)),
            jnp.tile(jnp.concatenate([-sin, sin], -1), (1, reps)))


def _even_weights(w_in):
    o_z, o_xbc, o_dt, o_glu, o_gate = 0, 1024, 2560, 2592, 4640
    main = jnp.concatenate([w_in[:, o_z:o_xbc], w_in[:, o_glu:o_gate], w_in[:, o_gate:],
                            w_in[:, o_xbc:o_dt]], axis=1)
    w_dt = w_in[:, o_dt:o_glu][:, jnp.array(_group_dt_order(), jnp.int32)]
    return main.astype(BF16), w_dt.astype(BF16)


def _odd_weights(w_in):
    o_q, o_k, o_gd = 3072, 4096, 4608
    perm = jnp.array(_att_col_perm(), jnp.int32)
    return jnp.concatenate([w_in[:, :o_q], w_in[:, o_q:o_k][:, perm], w_in[:, o_gd:][:, perm],
                            w_in[:, o_k:o_gd]], axis=1).astype(BF16)


def _odd_out_weights(w_out):
    perm = jnp.array(_att_col_perm(), jnp.int32)
    return jnp.concatenate([w_out[:MLP_CH], w_out[MLP_CH:][perm]], axis=0).astype(BF16)


def kernel(x, c, ctx, c_ctx, mod_w, mod_b, ln_g, ln_b, ev_w_in, ev_ssd_conv_w, ev_ssd_conv_b, ev_dt_bias, ev_a_log, ev_d_skip, ev_ssd_norm, ev_cv_w, ev_cv_b, ev_cv_ln_g, ev_cv_ln_b, ev_w_out, od_w_in, od_mlp_ln_g, od_mlp_ln_b, od_ws, od_bs, od_sink, od_w_out):
    bsz, seq, d = x.shape
    cos, sin_signed = _rope_tables(seq)
    rows = -(-(bsz + 1) // SUBLANE) * SUBLANE
    cond = jnp.concatenate([c, c_ctx[None, :], jnp.zeros((rows - bsz - 1, d), F32)], axis=0)
    mod = _modulation(cond, mod_w, mod_b)

    for layer in range(DEPTH):
        last = layer == DEPTH - 1
        i = layer // 2
        m = mod[layer]
        sh_x, sc_x, g_x = (m[:bsz, None, k * d:(k + 1) * d] for k in range(3))
        sh_c, sc_c, g_c = (jnp.broadcast_to(m[bsz:bsz + 1, None, k * d:(k + 1) * d], (bsz, 1, d)) for k in range(3))
        if layer % 2 == 0:
            w_main, w_dt = _even_weights(ev_w_in[i])
            w_out = ev_w_out[i].astype(BF16)
            ssd_args = (ev_ssd_conv_w[i], ev_ssd_conv_b[i], ev_dt_bias[i], ev_a_log[i], ev_d_skip[i])
            p_c, dt_c, dtt_c = _inproj_ctx(ctx, sc_c, sh_c, w_main, w_dt)
            y_c, h_c = _ssd(p_c, dt_c, dtt_c, None, *ssd_args)
            p_x, dt_x, dtt_x = _inproj(x, sc_x, sh_x, w_main, w_dt)
            y_x, _ = _ssd(p_x, dt_x, dtt_x, h_c, *ssd_args)
            cv_x = _cvconv(p_x, ev_cv_w[i], ev_cv_b[i])
            x = _outproj_even(y_x, cv_x, p_x, ev_ssd_norm[i], ev_cv_ln_g[i], ev_cv_ln_b[i],
                              x, g_x, ln_g[layer], ln_b[layer], w_out)
            if not last:
                cv_c = _cvconv(p_c, ev_cv_w[i], ev_cv_b[i])
                ctx = _outproj_even(y_c, cv_c, p_c, ev_ssd_norm[i], ev_cv_ln_g[i], ev_cv_ln_b[i],
                                    ctx, g_c, ln_g[layer], ln_b[layer], w_out)
        else:
            w_main = _odd_weights(od_w_in[i])
            w_out = _odd_out_weights(od_w_out[i])
            if last:
                p_c = _inproj_ctx(ctx, sc_c, sh_c, w_main[:, OD_K:OD_K + INPROJ_TN], plan=(("id", 0),))[0]
                ctx_col = 0
            else:
                p_c = _inproj_ctx(ctx, sc_c, sh_c, w_main)[0]
                ctx_col = OD_K
            p_x = _inproj(x, sc_x, sh_x, w_main)[0]
            yd_x = _attention(p_x, p_c, od_sink[i], cos, sin_signed, True, ctx_col)
            mlp_args = (od_mlp_ln_g[i], od_mlp_ln_b[i], od_ws[i], od_bs[i])
            x = _outproj_odd(p_x, yd_x, *mlp_args, x, g_x, ln_g[layer], ln_b[layer], w_out)
            if not last:
                yd_c = _attention(p_c, p_c, od_sink[i], cos, sin_signed, False)
                ctx = _outproj_odd(p_c, yd_c, *mlp_args, ctx, g_c, ln_g[layer], ln_b[layer], w_out)
    return x
```

```python
import functools
import math

import jax
import jax.numpy as jnp
from jax import lax
from jax.experimental import pallas as pl
from jax.experimental.pallas import tpu as pltpu

F32 = jnp.float32
BF16 = jnp.bfloat16

D_MODEL = 1024
DEPTH = 4
GRID_W = 64

SSD_HEADS = 16
SSD_HEAD_DIM = 64
SSD_INNER = SSD_HEADS * SSD_HEAD_DIM
SSD_GROUPS = 2
SSD_HPG = SSD_HEADS // SSD_GROUPS
SSD_STATE = 128
SSD_CHUNK = 128
SSD_CONV = 5
SSD_GROUP_CH = SSD_HPG * SSD_HEAD_DIM
CONV_CH = 1024
CONV_WIDTH = 31
MLP_CH = 1024
MLP_GROUPS = 8
MLP_GROUP_CH = MLP_CH // MLP_GROUPS
MLP_CHUNK = 128
ATT_HEADS = 16
ATT_KV_HEADS = 4
ATT_REP = ATT_HEADS // ATT_KV_HEADS
ATT_HEAD_DIM = 64
ATT_WINDOW = 128
ATT_BLOCK = 128
ATT_SCALE = ATT_HEAD_DIM ** -0.5
LOG2E = math.log2(math.e)
ROPE_BASE = 10000.0
ATT_KV_CH = ATT_KV_HEADS * ATT_HEAD_DIM
ATT_ONES = 16

DEEPNORM_ALPHA = (2 * DEPTH) ** 0.25
LN_EPS = 1e-5

PROJ_N = 5632
LANE = 128
SUBLANE = 8
VMEM_LIMIT = 56 * 1024 * 1024

EV_Z, EV_VAL, EV_GT, EV_GATE, EV_XS, EV_B, EV_C = 0, 1024, 2048, 3072, 4096, 5120, 5376
EVO_Z, EVO_GLU, EVO_GATE, EVO_XS, EVO_B, EVO_C = 0, 1024, 2048, 3072, 4096, 4352
OD_U, OD_V, OD_GC, OD_Q, OD_GD, OD_K, OD_VA = 0, 1024, 2048, 3072, 4096, 5120, 5376


def _silu(t):
    return t * jax.nn.sigmoid(t)


def _gelu_tanh(t):
    c = math.sqrt(2.0 / math.pi)
    return t * (0.5 * (1.0 + jnp.tanh(c * (t + 0.044715 * (t * t * t)))))


def _softplus(t):
    return jnp.maximum(t, 0.0) + jnp.log1p(jnp.exp(-jnp.abs(t)))


def _layer_norm(t, g, b):
    mu = jnp.mean(t, -1, keepdims=True)
    d = t - mu
    var = jnp.mean(d * d, -1, keepdims=True)
    return d * lax.rsqrt(var + LN_EPS) * g + b


def _params(*sem):
    return pltpu.CompilerParams(dimension_semantics=sem, vmem_limit_bytes=VMEM_LIMIT)


def _mod_kernel(c_ref, w_ref, b_ref, o_ref):
    s = _silu(c_ref[...]).astype(BF16)
    o_ref[0] = jnp.dot(s, w_ref[0].astype(BF16), preferred_element_type=F32) + b_ref[0]


def _modulation(cond, mod_w, mod_b):
    rows = cond.shape[0]
    d = D_MODEL
    return pl.pallas_call(
        _mod_kernel,
        out_shape=jax.ShapeDtypeStruct((DEPTH, rows, 3 * d), F32),
        grid=(DEPTH, 3),
        in_specs=[pl.BlockSpec((rows, d), lambda l, j: (0, 0)),
                  pl.BlockSpec((1, d, d), lambda l, j: (l, 0, j)),
                  pl.BlockSpec((1, 1, d), lambda l, j: (l, 0, j))],
        out_specs=pl.BlockSpec((1, rows, d), lambda l, j: (l, 0, j)),
        compiler_params=_params("arbitrary", "arbitrary"),
        name="modulation",
    )(cond, mod_w, mod_b.reshape(DEPTH, 1, 3 * d))


INPROJ_TM = 1024
INPROJ_SUB = 512
INPROJ_TN = 512


def _tile_plan(even):
    t = lambda col: col // INPROJ_TN
    if even:
        plan = [("silu", t(EV_Z) + i) for i in range(2)]
        plan += [("glu", t(EV_VAL) + i, t(EV_GT) + i) for i in range(2)]
        plan += [("silu", t(EV_GATE) + i) for i in range(2)]
        plan += [("id", t(EV_XS) + i) for i in range(3)]
    else:
        plan = [("gelu", t(OD_U) + i) for i in range(4)]
        plan += [("silu", t(OD_GC) + i) for i in range(2)]
        plan += [("id", t(OD_Q) + i) for i in range(2)]
        plan += [("silu", t(OD_GD) + i) for i in range(2)]
        plan += [("id", t(OD_K))]
    return plan


def _inproj_kernel(x_ref, sc_ref, sh_ref, w_ref, *rest, even, plan):
    if even:
        wdt_ref, wdtt_ref, o_ref, dt_ref, dtt_ref, h_ref = rest
    else:
        o_ref, h_ref = rest
    tm = x_ref.shape[1]
    sub = min(INPROJ_SUB, tm)

    def modulate(s):
        rows = pl.ds(s * sub, sub)
        h_ref[rows, :] = (x_ref[0, rows, :] * (1.0 + sc_ref[0]) + sh_ref[0]).astype(BF16)

    modulate(0)
    for s in range(tm // sub):
        if (s + 1) * sub < tm:
            modulate(s + 1)
        rows = pl.ds(s * sub, sub)
        if even:
            dt = jnp.dot(h_ref[rows, :], wdt_ref[...], preferred_element_type=F32)
            dtt = lax.dot_general(wdtt_ref[...], h_ref[rows, :], (((1,), (1,)), ((), ())),
                                  preferred_element_type=F32)
            per = dt.shape[1] // SSD_GROUPS
            for gi in range(SSD_GROUPS):
                dt_ref[0, gi, rows, :] = dt[:, gi * per:(gi + 1) * per]
                dtt_ref[0, gi, :, rows] = dtt[gi * per:(gi + 1) * per, :]
        tile = lambda j: jnp.dot(h_ref[rows, :], w_ref[:, pl.ds(j * INPROJ_TN, INPROJ_TN)],
                                 preferred_element_type=F32)
        for out_j, (kind, *src) in enumerate(plan):
            r = tile(src[0])
            if kind == "silu":
                r = _silu(r)
            elif kind == "gelu":
                r = _gelu_tanh(r)
            elif kind == "glu":
                r = r * jax.nn.sigmoid(tile(src[1]))
            o_ref[0, rows, pl.ds(out_j * INPROJ_TN, INPROJ_TN)] = r.astype(BF16)


def _inproj(x, sc, sh, w, w_dt=None, plan=None):
    bsz, seq, d = x.shape
    n = w.shape[1]
    tm = min(seq, INPROJ_TM)
    even = w_dt is not None
    plan = tuple(_tile_plan(even)) if plan is None else plan
    n_out = len(plan) * INPROJ_TN
    resident = dict(pipeline_mode=pl.Buffered(1))
    in_specs = [pl.BlockSpec((1, tm, d), lambda b, i: (b, i, 0)),
                pl.BlockSpec((1, 1, d), lambda b, i: (b, 0, 0)),
                pl.BlockSpec((1, 1, d), lambda b, i: (b, 0, 0)),
                pl.BlockSpec((d, n), lambda b, i: (0, 0), **resident)]
    out_shape = [jax.ShapeDtypeStruct((bsz, seq, n_out), BF16)]
    out_specs = [pl.BlockSpec((1, tm, n_out), lambda b, i: (b, i, 0))]
    args = [x, sc, sh, w]
    if even:
        ndt = w_dt.shape[1]
        per = ndt // SSD_GROUPS
        in_specs += [pl.BlockSpec((d, ndt), lambda b, i: (0, 0)),
                     pl.BlockSpec((ndt, d), lambda b, i: (0, 0))]
        out_shape += [jax.ShapeDtypeStruct((bsz, SSD_GROUPS, seq, per), F32),
                      jax.ShapeDtypeStruct((bsz, SSD_GROUPS, per, seq), F32)]
        out_specs += [pl.BlockSpec((1, SSD_GROUPS, tm, per), lambda b, i: (b, 0, i, 0)),
                      pl.BlockSpec((1, SSD_GROUPS, per, tm), lambda b, i: (b, 0, 0, i))]
        args += [w_dt, w_dt.T]
    return pl.pallas_call(
        functools.partial(_inproj_kernel, even=even, plan=plan),
        out_shape=out_shape,
        grid=(bsz, seq // tm),
        in_specs=in_specs,
        out_specs=out_specs,
        scratch_shapes=[pltpu.VMEM((tm, d), BF16)],
        compiler_params=_params("arbitrary", "arbitrary"),
        name="inproj",
    )(*args)


def _inproj_ctx(ctx, sc, sh, w, w_dt=None, plan=None):
    bsz, clen, d = ctx.shape
    outs = _inproj(ctx.reshape(1, bsz * clen, d), sc[:1], sh[:1], w, w_dt, plan)
    proj = outs[0].reshape(bsz, clen, -1)
    if w_dt is None:
        return (proj,)
    per = outs[1].shape[-1]
    dt = outs[1].reshape(SSD_GROUPS, bsz, clen, per).transpose(1, 0, 2, 3)
    dtt = outs[2].reshape(SSD_GROUPS, per, bsz, clen).transpose(2, 0, 1, 3)
    return proj, dt, dtt


def _conv_silu(src_ref, w_ref, b_ref, pad_ref, dst_ref, seq):
    ch = src_ref.shape[-1]
    halo = SUBLANE
    pad_ref[pl.ds(0, halo), :] = jnp.zeros((halo, ch), F32)
    pad_ref[pl.ds(seq + halo, halo), :] = jnp.zeros((halo, ch), F32)
    pad_ref[pl.ds(halo, seq), :] = src_ref[0].astype(F32)
    w = w_ref[...]
    bias = b_ref[...]
    first = halo - SSD_CONV // 2

    def body(t, carry):
        base = pl.multiple_of(t * SSD_CHUNK, SSD_CHUNK)
        win = pad_ref[pl.ds(base, SSD_CHUNK + 2 * halo), :]
        acc = bias
        for k in range(SSD_CONV):
            acc = acc + w[k:k + 1, :] * win[first + k:first + k + SSD_CHUNK, :]
        dst_ref[pl.ds(base, SSD_CHUNK), :] = _silu(acc).astype(dst_ref.dtype)
        return carry

    lax.fori_loop(0, seq // SSD_CHUNK, body, 0)


def _pair_expand(vals, col0):
    q = vals.shape[0]
    lane = lax.broadcasted_iota(jnp.int32, (q, LANE), 1)
    parts = []
    for k in range(SSD_HPG // 2):
        a = jnp.broadcast_to(vals[:, col0 + 2 * k:col0 + 2 * k + 1], (q, LANE))
        b = jnp.broadcast_to(vals[:, col0 + 2 * k + 1:col0 + 2 * k + 2], (q, LANE))
        parts.append(jnp.where(lane < SSD_HEAD_DIM, a, b))
    return jnp.concatenate(parts, axis=-1)


def _split_dot(lhs, rhs):
    if lhs.dtype == F32:
        hi = lhs.astype(BF16)
        lo = (lhs - hi.astype(F32)).astype(BF16)
        return (jnp.dot(hi, rhs, preferred_element_type=F32)
                + jnp.dot(lo, rhs, preferred_element_type=F32))
    hi = rhs.astype(BF16)
    lo = (rhs - hi.astype(F32)).astype(BF16)
    return (jnp.dot(lhs, hi, preferred_element_type=F32)
            + jnp.dot(lhs, lo, preferred_element_type=F32))


def _ssd_kernel(xs_ref, bm_ref, cm_ref, dt_ref, dtt_ref,
                wx_ref, wb_ref, wc_ref, bx_ref, bb_ref, bc_ref,
                dtb_row_ref, dtb_col_ref, alog_row_ref, alog_col_ref, dskip_ref, *rest, seq, zero_init):
    h0_ref = None if zero_init else rest[0]
    (y_ref, hout_ref, padx_ref, padn_ref, xc_ref, bcs_ref, ccs_ref,
     yfwd_ref, ybwd_ref, state_ref) = rest[(0 if zero_init else 1):]
    q = SSD_CHUNK
    n_chunks = seq // q
    nh = 2 * SSD_HPG

    _conv_silu(xs_ref, wx_ref, bx_ref, padx_ref, xc_ref, seq)
    _conv_silu(bm_ref, wb_ref, bb_ref, padn_ref, bcs_ref, seq)
    _conv_silu(cm_ref, wc_ref, bc_ref, padn_ref, ccs_ref, seq)

    a_row = -jnp.exp(alog_row_ref[0])
    a_col = -jnp.exp(alog_col_ref[0])
    dtb_row = dtb_row_ref[0]
    dtb_col = dtb_col_ref[0]
    row_i = lax.broadcasted_iota(jnp.int32, (q, q), 0)
    col_i = lax.broadcasted_iota(jnp.int32, (q, q), 1)
    lower = row_i >= col_i
    upper = row_i <= col_i
    tri_lo = jnp.where(lower, 1.0, 0.0).astype(BF16)
    tri_up = jnp.where(upper, 1.0, 0.0).astype(BF16)
    lane = lax.broadcasted_iota(jnp.int32, (q, LANE), 1)
    head_lo = lane < SSD_HEAD_DIM

    def prologue(t, direction):
        rows = pl.ds(pl.multiple_of(t * q, q), q)
        dt = _softplus(dt_ref[0, 0, rows, :] + dtb_row)
        dtt = _softplus(dtt_ref[0, 0, :, rows] + dtb_col)
        da = dt * (a_row * LOG2E)
        dat = dtt * (a_col * LOG2E)
        if direction == 0:
            cum = _split_dot(tri_lo, da)
            cumt = _split_dot(dat, tri_up)
        else:
            cum = _split_dot(tri_up, da)
            cumt = _split_dot(dat, tri_lo)
        edge = q - 1 if direction == 0 else 0
        wt = dtt * jnp.exp2(cumt[:, edge:edge + 1] - cumt)
        bc = bcs_ref[rows, :]
        cc = ccs_ref[rows, :]
        cb = lax.dot_general(cc, bc, (((1,), (1,)), ((), ())), preferred_element_type=F32)
        bctf = bc.astype(F32).T
        return cum, cumt - jnp.log2(dtt), wt, cb, bctf

    def chunk(t, direction, pro):
        cum, rowt, wt, cb, bctf = pro
        rows = pl.ds(pl.multiple_of(t * q, q), q)
        mask = lower if direction == 0 else upper
        edge = q - 1 if direction == 0 else 0
        total = cum[edge:edge + 1, :]
        col0 = direction * SSD_HPG
        xb = xc_ref[rows, :]
        ccf = ccs_ref[rows, :].astype(F32)
        st = state_ref[direction]
        stb = st.astype(BF16)
        zero = jnp.zeros((q, LANE), BF16)
        ys, news = [], []
        for k in range(SSD_HPG // 2):
            xp = xb[:, k * LANE:(k + 1) * LANE]
            sp = stb[:, k * LANE:(k + 1) * LANE]
            rhs_x = jnp.concatenate([jnp.where(head_lo, xp, zero), jnp.where(head_lo, zero, xp)], axis=0)
            rhs_s = jnp.concatenate([jnp.where(head_lo, sp, zero), jnp.where(head_lo, zero, sp)], axis=0)
            ms, cs, ws = [], [], []
            for half in range(2):
                c = col0 + 2 * k + half
                bcol = jnp.broadcast_to(cum[:, c:c + 1], (q, q))
                seg = bcol - jnp.broadcast_to(rowt[c:c + 1, :], (q, q))
                ms.append((cb * jnp.exp2(jnp.where(mask, seg, -jnp.inf))).astype(BF16))
                cs.append((ccf * jnp.exp2(bcol)).astype(BF16))
                ws.append((bctf * jnp.broadcast_to(wt[c:c + 1, :], (q, q))).astype(BF16))
            ys.append(jnp.dot(jnp.concatenate(ms + cs, axis=1), jnp.concatenate([rhs_x, rhs_s], axis=0),
                              preferred_element_type=F32))
            news.append(jnp.dot(jnp.concatenate(ws, axis=1), rhs_x, preferred_element_type=F32))
        state_ref[direction] = st * _pair_expand(jnp.exp2(total), col0) + jnp.concatenate(news, axis=-1)
        return rows, jnp.concatenate(ys, axis=-1)

    state_ref[...] = jnp.zeros(state_ref.shape, F32) if zero_init else h0_ref[0, :, 0]

    def both(i, carry):
        pro_f, pro_b = carry
        nxt_f = prologue(jnp.minimum(i + 1, n_chunks - 1), 0)
        nxt_b = prologue(jnp.maximum(n_chunks - 2 - i, 0), 1)
        rows, y = chunk(i, 0, pro_f)
        yfwd_ref[rows, :] = y
        rows, y = chunk(n_chunks - 1 - i, 1, pro_b)
        ybwd_ref[rows, :] = y
        return nxt_f, nxt_b

    lax.fori_loop(0, n_chunks, both, (prologue(0, 0), prologue(n_chunks - 1, 1)))
    hout_ref[0, :, 0] = state_ref[...]

    dskip = dskip_ref[...]

    def finish(t, carry):
        rows = pl.ds(pl.multiple_of(t * q, q), q)
        y = yfwd_ref[rows, :] + ybwd_ref[rows, :] + xc_ref[rows, :].astype(F32) * dskip
        y_ref[0, rows, :] = y.astype(y_ref.dtype)
        return carry

    lax.fori_loop(0, n_chunks, finish, 0)


def _group_dt_order():
    order = []
    for g in range(SSD_GROUPS):
        for direction in range(2):
            order += [direction * SSD_HEADS + g * SSD_HPG + r for r in range(SSD_HPG)]
    return order


def _ssd(proj, dt, dtt, h0, conv_w, conv_b, dt_bias, a_log, d_skip):
    bsz, seq, _ = proj.shape
    gc = SSD_GROUP_CH
    ns = SSD_STATE
    nh = 2 * SSD_HPG
    xs_blk, b_blk, c_blk = EVO_XS // gc, EVO_B // ns, EVO_C // ns
    cw_b0, cw_c0 = SSD_INNER // ns, (SSD_INNER + SSD_GROUPS * ns) // ns
    conv_b2 = conv_b.reshape(1, -1)
    d_skip_x = jnp.repeat(d_skip, SSD_HEAD_DIM).reshape(1, SSD_INNER)
    order = jnp.array(_group_dt_order(), jnp.int32)
    dt_bias_g = dt_bias.reshape(-1)[order].reshape(SSD_GROUPS, nh)
    a_log_g = a_log.reshape(-1)[order].reshape(SSD_GROUPS, nh)
    in_specs = [
        pl.BlockSpec((1, seq, gc), lambda b, g: (b, 0, xs_blk + g)),
        pl.BlockSpec((1, seq, ns), lambda b, g: (b, 0, b_blk + g)),
        pl.BlockSpec((1, seq, ns), lambda b, g: (b, 0, c_blk + g)),
        pl.BlockSpec((1, 1, seq, nh), lambda b, g: (b, g, 0, 0)),
        pl.BlockSpec((1, 1, nh, seq), lambda b, g: (b, g, 0, 0)),
        pl.BlockSpec((SSD_CONV, gc), lambda b, g: (0, g)),
        pl.BlockSpec((SSD_CONV, ns), lambda b, g: (0, cw_b0 + g)),
        pl.BlockSpec((SSD_CONV, ns), lambda b, g: (0, cw_c0 + g)),
        pl.BlockSpec((1, gc), lambda b, g: (0, g)),
        pl.BlockSpec((1, ns), lambda b, g: (0, cw_b0 + g)),
        pl.BlockSpec((1, ns), lambda b, g: (0, cw_c0 + g)),
        pl.BlockSpec((1, 1, nh), lambda b, g: (g, 0, 0)),
        pl.BlockSpec((1, nh, 1), lambda b, g: (g, 0, 0)),
        pl.BlockSpec((1, 1, nh), lambda b, g: (g, 0, 0)),
        pl.BlockSpec((1, nh, 1), lambda b, g: (g, 0, 0)),
        pl.BlockSpec((1, gc), lambda b, g: (0, g)),
    ]
    state_spec = pl.BlockSpec((1, 2, 1, ns, gc), lambda b, g: (b, 0, g, 0, 0))
    args = [proj, proj, proj, dt, dtt,
            conv_w, conv_w, conv_w, conv_b2, conv_b2, conv_b2,
            dt_bias_g.reshape(SSD_GROUPS, 1, nh), dt_bias_g.reshape(SSD_GROUPS, nh, 1),
            a_log_g.reshape(SSD_GROUPS, 1, nh), a_log_g.reshape(SSD_GROUPS, nh, 1), d_skip_x]
    if h0 is not None:
        in_specs.append(state_spec)
        args.append(h0)
    return pl.pallas_call(
        functools.partial(_ssd_kernel, seq=seq, zero_init=h0 is None),
        out_shape=[jax.ShapeDtypeStruct((bsz, seq, SSD_INNER), BF16),
                   jax.ShapeDtypeStruct((bsz, 2, SSD_GROUPS, ns, gc), F32)],
        grid=(bsz, SSD_GROUPS),
        in_specs=in_specs,
        out_specs=[pl.BlockSpec((1, seq, gc), lambda b, g: (b, 0, g)), state_spec],
        scratch_shapes=[pltpu.VMEM((seq + 2 * SUBLANE, gc), F32),
                        pltpu.VMEM((seq + 2 * SUBLANE, ns), F32),
                        pltpu.VMEM((seq, gc), BF16),
                        pltpu.VMEM((seq, ns), BF16),
                        pltpu.VMEM((seq, ns), BF16),
                        pltpu.VMEM((seq, gc), F32),
                        pltpu.VMEM((seq, gc), F32),
                        pltpu.VMEM((2, ns, gc), F32)],
        compiler_params=_params("arbitrary", "arbitrary"),
        name="ssd",
    )(*args)


CV_ROWS = 256
CV_HALO = 16
CV_TC = 128
CV_SPAN = CV_ROWS + 2 * CV_HALO - SUBLANE


def _cvconv_kernel(glu_ref, w_ref, b_ref, o_ref, pad_ref, sh_ref, *, seq):
    ch = glu_ref.shape[-1]
    pad_ref[pl.ds(0, CV_HALO), :] = jnp.zeros((CV_HALO, ch), F32)
    pad_ref[pl.ds(seq + CV_HALO, CV_HALO), :] = jnp.zeros((CV_HALO, ch), F32)
    pad_ref[pl.ds(CV_HALO, seq), :] = glu_ref[0].astype(F32)
    w = w_ref[...]
    bias = b_ref[...]
    first = CV_HALO - CONV_WIDTH // 2
    rows = min(CV_ROWS * CV_TC // ch, seq)
    span = rows + 2 * CV_HALO - SUBLANE

    def body(t, carry):
        base = pl.multiple_of(t * rows, rows)
        win = pad_ref[pl.ds(base, rows + 2 * CV_HALO), :]
        for s in range(SUBLANE):
            sh_ref[s, pl.ds(0, span), :] = win[s:s + span, :]
        acc = jnp.broadcast_to(bias, (rows, ch))
        for k in range(CONV_WIDTH):
            a, s = divmod(first + k, SUBLANE)
            acc = acc + w[k:k + 1, :] * sh_ref[s, pl.ds(a * SUBLANE, rows), :]
        o_ref[0, pl.ds(base, rows), :] = acc.astype(o_ref.dtype)
        return carry

    lax.fori_loop(0, seq // rows, body, 0)


def _cvconv(proj, cv_w, cv_b):
    bsz, seq, _ = proj.shape
    tc = CV_TC * max(1, 1024 // seq)
    return pl.pallas_call(
        functools.partial(_cvconv_kernel, seq=seq),
        out_shape=jax.ShapeDtypeStruct((bsz, seq, CONV_CH), BF16),
        grid=(bsz, CONV_CH // tc),
        in_specs=[pl.BlockSpec((1, seq, tc), lambda b, j: (b, 0, EVO_GLU // tc + j)),
                  pl.BlockSpec((CONV_WIDTH, tc), lambda b, j: (0, j)),
                  pl.BlockSpec((1, tc), lambda b, j: (0, j))],
        out_specs=pl.BlockSpec((1, seq, tc), lambda b, j: (b, 0, j)),
        scratch_shapes=[pltpu.VMEM((seq + 2 * CV_HALO, tc), F32),
                        pltpu.VMEM((SUBLANE, CV_SPAN, tc), F32)],
        compiler_params=_params("arbitrary", "arbitrary"),
        name="cvconv",
    )(proj, cv_w, cv_b.reshape(1, CONV_CH))


def _gmlp_rows(u_ref, v_ref, gc_ref, lng_ref, lnb_ref, ws_ref, bias_ref, row0, n_rows):
    lng = lng_ref[...]
    lnb = lnb_ref[...]
    bias = bias_ref[...]
    out = []
    for c in range(n_rows // MLP_CHUNK):
        rows = pl.ds(row0 + c * MLP_CHUNK, MLP_CHUNK)
        vn = _layer_norm(v_ref[0, rows, :].astype(F32), lng, lnb).astype(BF16)
        mixed = jnp.concatenate(
            [jnp.dot(ws_ref[gi], vn[:, gi * MLP_GROUP_CH:(gi + 1) * MLP_GROUP_CH], preferred_element_type=F32)
             for gi in range(MLP_GROUPS)], axis=-1)
        u = u_ref[0, rows, :].astype(F32)
        out.append((u * (mixed + bias) * gc_ref[0, rows, :].astype(F32)).astype(BF16))
    return jnp.concatenate(out, axis=0)


def _rope(t, cos, sin_signed):
    width = t.shape[1]
    reps = width // LANE
    half = ATT_HEAD_DIM // 2
    lane = lax.broadcasted_iota(jnp.int32, t.shape, 1)
    first_half = (lane % ATT_HEAD_DIM) < half
    swapped = jnp.where(first_half, pltpu.roll(t, width - half, 1), pltpu.roll(t, half, 1))
    c = jnp.concatenate([cos] * reps, axis=-1)
    s = jnp.concatenate([sin_signed] * reps, axis=-1)
    return t * c + swapped * s


def _att_head_order():
    order = []
    for c in range(ATT_HEADS // 2):
        j, r = divmod(c, ATT_REP)
        order += [(2 * j) * ATT_REP + r, (2 * j + 1) * ATT_REP + r]
    return order


def _att_reorder(t, axis):
    shape = t.shape
    kvp = ATT_KV_HEADS // 2
    t = jnp.moveaxis(t, axis, 0).reshape((kvp, 2, ATT_REP, ATT_HEAD_DIM) + shape[:axis] + shape[axis + 1:])
    t = jnp.swapaxes(t, 1, 2).reshape((shape[axis],) + shape[:axis] + shape[axis + 1:])
    return jnp.moveaxis(t, 0, axis)


def _attn_kernel(*refs, seq, ctx_len, latent):
    if latent:
        (q_ref, gd_ref, k_ref, v_ref, kc_ref, vc_ref, sink_ref, cos_ref, sin_ref,
         o_ref, kpad_ref, vtp_ref, vct_ref) = refs
    else:
        q_ref, gd_ref, kc_ref, vc_ref, sink_ref, o_ref, vct_ref = refs
    i = pl.program_id(1)
    w = ATT_BLOCK
    nt = (((1,), (1,)), ((), ()))

    vrows = LANE + ATT_ONES

    @pl.when(i == 0)
    def _():
        vct = vc_ref[0].astype(F32).T.astype(BF16)
        for j in range(ATT_KV_CH // LANE):
            vct_ref[pl.ds(j * vrows, LANE), :] = vct[j * LANE:(j + 1) * LANE, :]
            vct_ref[pl.ds(j * vrows + LANE, ATT_ONES), :] = jnp.ones((ATT_ONES, ctx_len), BF16)
        if latent:
            kpad_ref[pl.ds(0, w), :] = jnp.zeros((w, ATT_KV_CH), BF16)
            kpad_ref[pl.ds(seq + w, w), :] = jnp.zeros((w, ATT_KV_CH), BF16)
            kpad_ref[pl.ds(w, seq), :] = _rope(k_ref[0].astype(F32), cos_ref[...], sin_ref[...]).astype(BF16)
            for j in range(ATT_KV_CH // LANE):
                vtp_ref[pl.ds(j * vrows, LANE), pl.ds(0, w)] = jnp.zeros((LANE, w), BF16)
                vtp_ref[pl.ds(j * vrows, LANE), pl.ds(seq + w, w)] = jnp.zeros((LANE, w), BF16)
                vtp_ref[pl.ds(j * vrows + LANE, ATT_ONES), :] = jnp.ones((ATT_ONES, seq + 2 * w), BF16)
            for t in range(seq // w):
                vt = v_ref[0, pl.ds(t * w, w), :].astype(F32).T.astype(BF16)
                for j in range(ATT_KV_CH // LANE):
                    vtp_ref[pl.ds(j * vrows, LANE), pl.ds((t + 1) * w, w)] = vt[j * LANE:(j + 1) * LANE, :]

    if latent:
        base = pl.multiple_of(i * w, w)
        q = _rope(q_ref[0].astype(F32), cos_ref[pl.ds(base, w), :], sin_ref[pl.ds(base, w), :])
        kwin = kpad_ref[pl.ds(base, 3 * w), :]
        qi = lax.broadcasted_iota(jnp.int32, (w, w), 1)
        u = lax.broadcasted_iota(jnp.int32, (w, w), 0)
        before = (u - w >= qi - ATT_WINDOW) & (base + u - w >= 0)
        after = (u + w <= qi + ATT_WINDOW) & (base + u + w < seq)
        bias_lo = jnp.where(before, 0.0, -jnp.inf)
        bias_hi = jnp.where(after, 0.0, -jnp.inf)
        bias_lo = jnp.concatenate([bias_lo, bias_lo], axis=1)
        bias_hi = jnp.concatenate([bias_hi, bias_hi], axis=1)
    else:
        q = q_ref[0].astype(F32)
    qs = (q * (ATT_SCALE * LOG2E)).astype(BF16)
    kc = kc_ref[0]
    lane = lax.broadcasted_iota(jnp.int32, (w, LANE), 1)
    low_lanes = lane < ATT_HEAD_DIM
    low_rows = lax.broadcasted_iota(jnp.int32, (LANE, w), 0) < ATT_HEAD_DIM
    zero = jnp.zeros((w, LANE), BF16)
    n_cols = ATT_HEADS // 2

    def scores(c):
        kv = slice((c // ATT_REP) * LANE, (c // ATT_REP + 1) * LANE)
        qc = qs[:, c * LANE:(c + 1) * LANE]
        rhs = jnp.concatenate([jnp.where(low_lanes, qc, zero), jnp.where(low_lanes, zero, qc)], axis=0)
        s_ctx = lax.dot_general(kc[:, kv], rhs, nt, preferred_element_type=F32)
        s_lat = lax.dot_general(kwin[:, kv], rhs, nt, preferred_element_type=F32) if latent else None
        return s_ctx, s_lat

    def finish(c, s_ctx, s_lat):
        j = c // ATT_REP
        snk = sink_ref[0, :, pl.ds(c * 2 * w, 2 * w)]
        m = jnp.maximum(jnp.max(s_ctx, axis=0, keepdims=True), snk)
        if latent:
            s_lo = s_lat[:w] + bias_lo
            s_mid = s_lat[w:2 * w]
            s_hi = s_lat[2 * w:] + bias_hi
            m = jnp.maximum(m, jnp.maximum(jnp.maximum(jnp.max(s_lo, axis=0, keepdims=True),
                                                       jnp.max(s_mid, axis=0, keepdims=True)),
                                           jnp.max(s_hi, axis=0, keepdims=True)))
        p_ctx = jnp.exp2(s_ctx - m).astype(BF16)
        acc = jnp.dot(vct_ref[pl.ds(j * vrows, vrows), :], p_ctx, preferred_element_type=F32)
        if latent:
            p_lat = jnp.concatenate([jnp.exp2(s_lo - m), jnp.exp2(s_mid - m), jnp.exp2(s_hi - m)],
                                    axis=0).astype(BF16)
            acc = acc + jnp.dot(vtp_ref[pl.ds(j * vrows, vrows), pl.ds(base, 3 * w)], p_lat,
                                preferred_element_type=F32)
        den = acc[LANE:LANE + 1, :] + jnp.exp2(snk - m)
        acc = acc[:LANE, :] * (1.0 / den)
        return jnp.where(low_rows, acc[:, :w], acc[:, w:]).T

    outs = []
    ahead = 3
    pending = [scores(c) for c in range(ahead)]
    for c in range(n_cols):
        if c + ahead < n_cols:
            pending.append(scores(c + ahead))
        outs.append(finish(c, *pending.pop(0)))
    gd = gd_ref[0].astype(F32)
    o_ref[0] = (jnp.concatenate(outs, axis=-1) * gd).astype(o_ref.dtype)


def _attention(proj, proj_ctx, sink, cos, sin_signed, latent, ctx_col=OD_K):
    bsz, seq, _ = proj.shape
    ctx_len = proj_ctx.shape[1]
    w = ATT_BLOCK
    qch = ATT_HEADS * ATT_HEAD_DIM
    kvc = ATT_KV_CH
    q_spec = pl.BlockSpec((1, w, qch), lambda b, i: (b, i, OD_Q // qch))
    gd_spec = pl.BlockSpec((1, w, qch), lambda b, i: (b, i, OD_GD // qch))
    kc_spec = pl.BlockSpec((1, ctx_len, kvc), lambda b, i: (b, 0, ctx_col // kvc))
    vc_spec = pl.BlockSpec((1, ctx_len, kvc), lambda b, i: (b, 0, ctx_col // kvc + 1))
    sink2 = jnp.repeat(sink[jnp.array(_att_head_order(), jnp.int32)] * LOG2E, w).reshape(1, 1, ATT_HEADS * w)
    sink_spec = pl.BlockSpec((1, 1, ATT_HEADS * w), lambda b, i: (0, 0, 0))
    vrows = (kvc // LANE) * (LANE + ATT_ONES)
    vct = pltpu.VMEM((vrows, ctx_len), BF16)
    if latent:
        in_specs = [q_spec, gd_spec,
                    pl.BlockSpec((1, seq, kvc), lambda b, i: (b, 0, OD_K // kvc)),
                    pl.BlockSpec((1, seq, kvc), lambda b, i: (b, 0, OD_VA // kvc)),
                    kc_spec, vc_spec, sink_spec,
                    pl.BlockSpec((seq, LANE), lambda b, i: (0, 0)),
                    pl.BlockSpec((seq, LANE), lambda b, i: (0, 0))]
        args = (proj, proj, proj, proj, proj_ctx, proj_ctx, sink2, cos, sin_signed)
        scratch = [pltpu.VMEM((seq + 2 * w, kvc), BF16), pltpu.VMEM((vrows, seq + 2 * w), BF16), vct]
    else:
        in_specs = [q_spec, gd_spec, kc_spec, vc_spec, sink_spec]
        args = (proj, proj, proj_ctx, proj_ctx, sink2)
        scratch = [vct]
    return pl.pallas_call(
        functools.partial(_attn_kernel, seq=seq, ctx_len=ctx_len, latent=latent),
        out_shape=jax.ShapeDtypeStruct((bsz, seq, qch), BF16),
        grid=(bsz, seq // w),
        in_specs=in_specs,
        out_specs=pl.BlockSpec((1, w, qch), lambda b, i: (b, i, 0)),
        scratch_shapes=scratch,
        compiler_params=_params("arbitrary", "arbitrary"),
        name="attention" if latent else "ctx_attention",
    )(*args)


OUTPROJ_TM = 1024
OUTPROJ_SUB = 512


def _outproj_kernel(*refs, even):
    if even:
        (y_ref, z_ref, cv_ref, gate_ref, nw_ref, cvg_ref, cvb_ref,
         x_ref, g_ref, lng_ref, lnb_ref, w_ref, o_ref) = refs
    else:
        (u_ref, v_ref, gc_ref, mlng_ref, mlnb_ref, ws_ref, bias_ref, yb_ref,
         x_ref, g_ref, lng_ref, lnb_ref, w_ref, o_ref) = refs
    half = w_ref.shape[0] // 2
    tm = x_ref.shape[1]
    sub = min(OUTPROJ_SUB, tm)

    def project(k):
        rows = pl.ds(k * sub, sub)
        if even:
            t = y_ref[0, rows, :].astype(F32) * z_ref[0, rows, :].astype(F32)
            ya = (t * lax.rsqrt(jnp.mean(t * t, -1, keepdims=True) + LN_EPS) * nw_ref[...]).astype(BF16)
            yb = (_silu(_layer_norm(cv_ref[0, rows, :].astype(F32), cvg_ref[...], cvb_ref[...]))
                  * gate_ref[0, rows, :].astype(F32)).astype(BF16)
        else:
            ya = _gmlp_rows(u_ref, v_ref, gc_ref, mlng_ref, mlnb_ref, ws_ref, bias_ref, k * sub, sub)
            yb = yb_ref[0, rows, :]
        return (jnp.dot(ya, w_ref[pl.ds(0, half), :], preferred_element_type=F32)
                + jnp.dot(yb, w_ref[pl.ds(half, half), :], preferred_element_type=F32))

    n_sub = tm // sub
    y = project(0)
    for k in range(n_sub):
        nxt = project(k + 1) if k + 1 < n_sub else None
        rows = pl.ds(k * sub, sub)
        r = DEEPNORM_ALPHA * x_ref[0, rows, :] + g_ref[0] * y
        o_ref[0, rows, :] = _layer_norm(r, lng_ref[...], lnb_ref[...])
        y = nxt


def _outproj(mix_args, mix_specs, x, g, ln_g, ln_b, w_out, even, tm):
    bsz, seq, d = x.shape
    vec = pl.BlockSpec((1, d), lambda b, i: (0, 0))
    in_specs = list(mix_specs) + [
        pl.BlockSpec((1, tm, d), lambda b, i: (b, i, 0)),
        pl.BlockSpec((1, 1, d), lambda b, i: (b, 0, 0)),
        vec, vec,
        pl.BlockSpec(w_out.shape, lambda b, i: (0, 0))]
    return pl.pallas_call(
        functools.partial(_outproj_kernel, even=even),
        out_shape=jax.ShapeDtypeStruct((bsz, seq, d), F32),
        grid=(bsz, seq // tm),
        in_specs=in_specs,
        out_specs=pl.BlockSpec((1, tm, d), lambda b, i: (b, i, 0)),
        compiler_params=_params("arbitrary", "arbitrary"),
        name="outproj_even" if even else "outproj_odd",
    )(*mix_args, x, g, ln_g.reshape(1, d), ln_b.reshape(1, d), w_out)


def _outproj_even(y_ssd, cv, proj, norm_w, cv_ln_g, cv_ln_b, x, g, ln_g, ln_b, w_out):
    seq = x.shape[1]
    tm = min(seq, OUTPROJ_TM)
    ch = SSD_INNER
    blk = lambda col: pl.BlockSpec((1, tm, ch), lambda b, i: (b, i, col // ch))
    vec = pl.BlockSpec((1, ch), lambda b, i: (0, 0))
    specs = [blk(0), blk(EVO_Z), blk(0), blk(EVO_GATE), vec, vec, vec]
    args = (y_ssd, proj, cv, proj, norm_w.reshape(1, ch), cv_ln_g.reshape(1, ch), cv_ln_b.reshape(1, ch))
    return _outproj(args, specs, x, g, ln_g, ln_b, w_out, True, tm)


def _outproj_odd(proj, yd, mlp_ln_g, mlp_ln_b, ws, bs, x, g, ln_g, ln_b, w_out):
    seq = x.shape[1]
    tm = min(seq, OUTPROJ_TM)
    ch = MLP_CH
    blk = lambda col: pl.BlockSpec((1, tm, ch), lambda b, i: (b, i, col // ch))
    vec = pl.BlockSpec((1, ch), lambda b, i: (0, 0))
    bias = jnp.repeat(bs.T, MLP_GROUP_CH, axis=1)
    specs = [blk(OD_U), blk(OD_V), blk(OD_GC), vec, vec,
             pl.BlockSpec((MLP_GROUPS, MLP_CHUNK, MLP_CHUNK), lambda b, i: (0, 0, 0)),
             pl.BlockSpec((MLP_CHUNK, ch), lambda b, i: (0, 0)), blk(0)]
    args = (proj, proj, proj, mlp_ln_g.reshape(1, ch), mlp_ln_b.reshape(1, ch), ws.astype(BF16), bias, yd)
    return _outproj(args, specs, x, g, ln_g, ln_b, w_out, False, tm)


def _rope_tables(seq):
    t = jnp.arange(seq)
    row = (t // GRID_W).astype(F32)
    col = (t % GRID_W).astype(F32)
    n_freq = ATT_HEAD_DIM // 4
    inv = ROPE_BASE ** (-jnp.arange(n_freq, dtype=F32) / n_freq)
    ang = jnp.concatenate([row[:, None] * inv, col[:, None] * inv], -1)
    cos, sin = jnp.cos(ang), jnp.sin(ang)
    reps = LANE // ATT_HEAD_DIM
    return (jnp.tile(jnp.concatenate([cos, cos], -1), (1, reps)),
            jnp.tile(jnp.concatenate([-sin, sin], -1), (1, reps)))


def _even_weights(w_in):
    o_z, o_xbc, o_dt, o_glu, o_gate = 0, 1024, 2560, 2592, 4640
    main = jnp.concatenate([w_in[:, o_z:o_xbc], w_in[:, o_glu:o_gate], w_in[:, o_gate:],
                            w_in[:, o_xbc:o_dt]], axis=1)
    w_dt = w_in[:, o_dt:o_glu][:, jnp.array(_group_dt_order(), jnp.int32)]
    return main.astype(BF16), w_dt.astype(BF16)


def _odd_weights(w_in):
    o_q, o_k, o_gd = 3072, 4096, 4608
    return jnp.concatenate([w_in[:, :o_q], _att_reorder(w_in[:, o_q:o_k], 1), _att_reorder(w_in[:, o_gd:], 1),
                            w_in[:, o_k:o_gd]], axis=1).astype(BF16)


def _odd_out_weights(w_out):
    return jnp.concatenate([w_out[:MLP_CH], _att_reorder(w_out[MLP_CH:], 0)], axis=0).astype(BF16)


def kernel(x, c, ctx, c_ctx, mod_w, mod_b, ln_g, ln_b, ev_w_in, ev_ssd_conv_w, ev_ssd_conv_b, ev_dt_bias, ev_a_log, ev_d_skip, ev_ssd_norm, ev_cv_w, ev_cv_b, ev_cv_ln_g, ev_cv_ln_b, ev_w_out, od_w_in, od_mlp_ln_g, od_mlp_ln_b, od_ws, od_bs, od_sink, od_w_out):
    bsz, seq, d = x.shape
    cos, sin_signed = _rope_tables(seq)
    rows = -(-(bsz + 1) // SUBLANE) * SUBLANE
    cond = jnp.concatenate([c, c_ctx[None, :], jnp.zeros((rows - bsz - 1, d), F32)], axis=0)
    mod = _modulation(cond, mod_w, mod_b)

    for layer in range(DEPTH):
        last = layer == DEPTH - 1
        i = layer // 2
        m = mod[layer]
        sh_x, sc_x, g_x = (m[:bsz, None, k * d:(k + 1) * d] for k in range(3))
        sh_c, sc_c, g_c = (jnp.broadcast_to(m[bsz:bsz + 1, None, k * d:(k + 1) * d], (bsz, 1, d)) for k in range(3))
        if layer % 2 == 0:
            w_main, w_dt = _even_weights(ev_w_in[i])
            w_out = ev_w_out[i].astype(BF16)
            ssd_args = (ev_ssd_conv_w[i], ev_ssd_conv_b[i], ev_dt_bias[i], ev_a_log[i], ev_d_skip[i])
            p_c, dt_c, dtt_c = _inproj_ctx(ctx, sc_c, sh_c, w_main, w_dt)
            y_c, h_c = _ssd(p_c, dt_c, dtt_c, None, *ssd_args)
            p_x, dt_x, dtt_x = _inproj(x, sc_x, sh_x, w_main, w_dt)
            y_x, _ = _ssd(p_x, dt_x, dtt_x, h_c, *ssd_args)
            cv_x = _cvconv(p_x, ev_cv_w[i], ev_cv_b[i])
            x = _outproj_even(y_x, cv_x, p_x, ev_ssd_norm[i], ev_cv_ln_g[i], ev_cv_ln_b[i],
                              x, g_x, ln_g[layer], ln_b[layer], w_out)
            if not last:
                cv_c = _cvconv(p_c, ev_cv_w[i], ev_cv_b[i])
                ctx = _outproj_even(y_c, cv_c, p_c, ev_ssd_norm[i], ev_cv_ln_g[i], ev_cv_ln_b[i],
                                    ctx, g_c, ln_g[layer], ln_b[layer], w_out)
        else:
            w_main = _odd_weights(od_w_in[i])
            w_out = _odd_out_weights(od_w_out[i])
            if last:
                p_c = _inproj_ctx(ctx, sc_c, sh_c, w_main[:, OD_K:OD_K + INPROJ_TN], plan=(("id", 0),))[0]
                ctx_col = 0
            else:
                p_c = _inproj_ctx(ctx, sc_c, sh_c, w_main)[0]
                ctx_col = OD_K
            p_x = _inproj(x, sc_x, sh_x, w_main)[0]
            yd_x = _attention(p_x, p_c, od_sink[i], cos, sin_signed, True, ctx_col)
            mlp_args = (od_mlp_ln_g[i], od_mlp_ln_b[i], od_ws[i], od_bs[i])
            x = _outproj_odd(p_x, yd_x, *mlp_args, x, g_x, ln_g[layer], ln_b[layer], w_out)
            if not last:
                yd_c = _attention(p_c, p_c, od_sink[i], cos, sin_signed, False)
                ctx = _outproj_odd(p_c, yd_c, *mlp_args, ctx, g_c, ln_g[layer], ln_b[layer], w_out)
    return x
```

```python
import functools
import math

import jax
import jax.numpy as jnp
from jax import lax
from jax.experimental import pallas as pl
from jax.experimental.pallas import tpu as pltpu

F32 = jnp.float32
BF16 = jnp.bfloat16

D_MODEL = 1024
DEPTH = 4
GRID_W = 64

SSD_HEADS = 16
SSD_HEAD_DIM = 64
SSD_INNER = SSD_HEADS * SSD_HEAD_DIM
SSD_GROUPS = 2
SSD_HPG = SSD_HEADS // SSD_GROUPS
SSD_STATE = 128
SSD_CHUNK = 128
SSD_CONV = 5
SSD_GROUP_CH = SSD_HPG * SSD_HEAD_DIM
CONV_CH = 1024
CONV_WIDTH = 31
MLP_CH = 1024
MLP_GROUPS = 8
MLP_GROUP_CH = MLP_CH // MLP_GROUPS
MLP_CHUNK = 128
ATT_HEADS = 16
ATT_KV_HEADS = 4
ATT_REP = ATT_HEADS // ATT_KV_HEADS
ATT_HEAD_DIM = 64
ATT_WINDOW = 128
ATT_BLOCK = 128
ATT_SCALE = ATT_HEAD_DIM ** -0.5
LOG2E = math.log2(math.e)
ROPE_BASE = 10000.0
ATT_KV_CH = ATT_KV_HEADS * ATT_HEAD_DIM
ATT_ONES = 16

DEEPNORM_ALPHA = (2 * DEPTH) ** 0.25
LN_EPS = 1e-5

LANE = 128
SUBLANE = 8
VMEM_LIMIT = 56 * 1024 * 1024


def _offsets(*widths):
    return tuple(sum(widths[:i]) for i in range(len(widths)))


SSD_BC = SSD_GROUPS * SSD_STATE
EV_Z, EV_VAL, EV_GT, EV_GATE, EV_XS, EV_B, EV_C = _offsets(SSD_INNER, CONV_CH, CONV_CH, CONV_CH, SSD_INNER, SSD_BC, SSD_BC)
EVO_Z, EVO_GLU, EVO_GATE, EVO_XS, EVO_B, EVO_C = _offsets(SSD_INNER, CONV_CH, CONV_CH, SSD_INNER, SSD_BC, SSD_BC)
OD_U, OD_V, OD_GC, OD_Q, OD_GD, OD_K, OD_VA = _offsets(MLP_CH, MLP_CH, MLP_CH, ATT_HEADS * ATT_HEAD_DIM,
                                                       ATT_HEADS * ATT_HEAD_DIM, ATT_KV_CH, ATT_KV_CH)


def _silu(t):
    return t * jax.nn.sigmoid(t)


def _gelu_tanh(t):
    c = -2.0 * math.sqrt(2.0 / math.pi) * LOG2E
    return t / (1.0 + jnp.exp2(t * (c + (c * 0.044715) * (t * t))))


def _softplus(t):
    return jnp.maximum(t, 0.0) + jnp.log1p(jnp.exp(-jnp.abs(t)))


def _layer_norm(t, g, b):
    mu = jnp.mean(t, -1, keepdims=True)
    d = t - mu
    var = jnp.mean(d * d, -1, keepdims=True)
    return d * lax.rsqrt(var + LN_EPS) * g + b


def _params(*sem):
    return pltpu.CompilerParams(dimension_semantics=sem, vmem_limit_bytes=VMEM_LIMIT)


def _mod_kernel(c_ref, w_ref, b_ref, o_ref):
    s = _silu(c_ref[...]).astype(BF16)
    o_ref[0] = jnp.dot(s, w_ref[0].astype(BF16), preferred_element_type=F32) + b_ref[0]


def _modulation(cond, mod_w, mod_b):
    rows = cond.shape[0]
    d = D_MODEL
    return pl.pallas_call(
        _mod_kernel,
        out_shape=jax.ShapeDtypeStruct((DEPTH, rows, 3 * d), F32),
        grid=(DEPTH, 3),
        in_specs=[pl.BlockSpec((rows, d), lambda l, j: (0, 0)),
                  pl.BlockSpec((1, d, d), lambda l, j: (l, 0, j)),
                  pl.BlockSpec((1, 1, d), lambda l, j: (l, 0, j))],
        out_specs=pl.BlockSpec((1, rows, d), lambda l, j: (l, 0, j)),
        compiler_params=_params("arbitrary", "arbitrary"),
        name="modulation",
    )(cond, mod_w, mod_b.reshape(DEPTH, 1, 3 * d))


INPROJ_TM = 1024
INPROJ_SUB = 512
INPROJ_TN = 512


def _tile_plan(even):
    t = lambda col: col // INPROJ_TN
    if even:
        plan = [("silu", t(EV_Z) + i) for i in range(2)]
        plan += [("glu", t(EV_VAL) + i, t(EV_GT) + i) for i in range(2)]
        plan += [("silu", t(EV_GATE) + i) for i in range(2)]
        plan += [("id", t(EV_XS) + i) for i in range(3)]
    else:
        plan = [("gelu", t(OD_U) + i) for i in range(4)]
        plan += [("silu", t(OD_GC) + i) for i in range(2)]
        plan += [("id", t(OD_Q) + i) for i in range(2)]
        plan += [("silu", t(OD_GD) + i) for i in range(2)]
        plan += [("id", t(OD_K))]
    return plan


def _inproj_kernel(x_ref, sc_ref, sh_ref, w_ref, *rest, even, plan):
    if even:
        wdt_ref, wdtt_ref, o_ref, dt_ref, dtt_ref, h_ref = rest
    else:
        o_ref, h_ref = rest
    tm = x_ref.shape[1]
    sub = min(INPROJ_SUB, tm)

    def modulate(s):
        rows = pl.ds(s * sub, sub)
        h_ref[rows, :] = (x_ref[0, rows, :] * (1.0 + sc_ref[0]) + sh_ref[0]).astype(BF16)

    modulate(0)
    for s in range(tm // sub):
        if (s + 1) * sub < tm:
            modulate(s + 1)
        rows = pl.ds(s * sub, sub)
        if even:
            dt = jnp.dot(h_ref[rows, :], wdt_ref[...], preferred_element_type=F32)
            dtt = lax.dot_general(wdtt_ref[...], h_ref[rows, :], (((1,), (1,)), ((), ())),
                                  preferred_element_type=F32)
            per = dt.shape[1] // SSD_GROUPS
            for gi in range(SSD_GROUPS):
                dt_ref[0, gi, rows, :] = dt[:, gi * per:(gi + 1) * per]
                dtt_ref[0, gi, :, rows] = dtt[gi * per:(gi + 1) * per, :]
        tile = lambda j: jnp.dot(h_ref[rows, :], w_ref[:, pl.ds(j * INPROJ_TN, INPROJ_TN)],
                                 preferred_element_type=F32)
        for out_j, (kind, *src) in enumerate(plan):
            r = tile(src[0])
            if kind == "silu":
                r = _silu(r)
            elif kind == "gelu":
                r = _gelu_tanh(r)
            elif kind == "glu":
                r = r * jax.nn.sigmoid(tile(src[1]))
            o_ref[0, rows, pl.ds(out_j * INPROJ_TN, INPROJ_TN)] = r.astype(BF16)


def _inproj(x, sc, sh, w, w_dt=None, plan=None):
    bsz, seq, d = x.shape
    n = w.shape[1]
    tm = min(seq, INPROJ_TM)
    even = w_dt is not None
    plan = tuple(_tile_plan(even)) if plan is None else plan
    n_out = len(plan) * INPROJ_TN
    resident = dict(pipeline_mode=pl.Buffered(1))
    in_specs = [pl.BlockSpec((1, tm, d), lambda b, i: (b, i, 0)),
                pl.BlockSpec((1, 1, d), lambda b, i: (b, 0, 0)),
                pl.BlockSpec((1, 1, d), lambda b, i: (b, 0, 0)),
                pl.BlockSpec((d, n), lambda b, i: (0, 0), **resident)]
    out_shape = [jax.ShapeDtypeStruct((bsz, seq, n_out), BF16)]
    out_specs = [pl.BlockSpec((1, tm, n_out), lambda b, i: (b, i, 0))]
    args = [x, sc, sh, w]
    if even:
        ndt = w_dt.shape[1]
        per = ndt // SSD_GROUPS
        in_specs += [pl.BlockSpec((d, ndt), lambda b, i: (0, 0)),
                     pl.BlockSpec((ndt, d), lambda b, i: (0, 0))]
        out_shape += [jax.ShapeDtypeStruct((bsz, SSD_GROUPS, seq, per), F32),
                      jax.ShapeDtypeStruct((bsz, SSD_GROUPS, per, seq), F32)]
        out_specs += [pl.BlockSpec((1, SSD_GROUPS, tm, per), lambda b, i: (b, 0, i, 0)),
                      pl.BlockSpec((1, SSD_GROUPS, per, tm), lambda b, i: (b, 0, 0, i))]
        args += [w_dt, w_dt.T]
    return pl.pallas_call(
        functools.partial(_inproj_kernel, even=even, plan=plan),
        out_shape=out_shape,
        grid=(bsz, seq // tm),
        in_specs=in_specs,
        out_specs=out_specs,
        scratch_shapes=[pltpu.VMEM((tm, d), BF16)],
        compiler_params=_params("arbitrary", "arbitrary"),
        name="inproj",
    )(*args)


def _inproj_ctx(ctx, sc, sh, w, w_dt=None, plan=None):
    bsz, clen, d = ctx.shape
    outs = _inproj(ctx.reshape(1, bsz * clen, d), sc[:1], sh[:1], w, w_dt, plan)
    proj = outs[0].reshape(bsz, clen, -1)
    if w_dt is None:
        return (proj,)
    per = outs[1].shape[-1]
    dt = outs[1].reshape(SSD_GROUPS, bsz, clen, per).transpose(1, 0, 2, 3)
    dtt = outs[2].reshape(SSD_GROUPS, per, bsz, clen).transpose(2, 0, 1, 3)
    return proj, dt, dtt


def _conv_silu(src_ref, w_ref, b_ref, pad_ref, dst_ref, seq):
    ch = src_ref.shape[-1]
    halo = SUBLANE
    pad_ref[pl.ds(0, halo), :] = jnp.zeros((halo, ch), F32)
    pad_ref[pl.ds(seq + halo, halo), :] = jnp.zeros((halo, ch), F32)
    pad_ref[pl.ds(halo, seq), :] = src_ref[0].astype(F32)
    w = w_ref[...]
    bias = b_ref[...]
    first = halo - SSD_CONV // 2

    def body(t, carry):
        base = pl.multiple_of(t * SSD_CHUNK, SSD_CHUNK)
        win = pad_ref[pl.ds(base, SSD_CHUNK + 2 * halo), :]
        acc = bias
        for k in range(SSD_CONV):
            acc = acc + w[k:k + 1, :] * win[first + k:first + k + SSD_CHUNK, :]
        dst_ref[pl.ds(base, SSD_CHUNK), :] = _silu(acc).astype(dst_ref.dtype)
        return carry

    lax.fori_loop(0, seq // SSD_CHUNK, body, 0)


def _pair_expand(vals, col0):
    q = vals.shape[0]
    lane = lax.broadcasted_iota(jnp.int32, (q, LANE), 1)
    parts = []
    for k in range(SSD_HPG // 2):
        a = jnp.broadcast_to(vals[:, col0 + 2 * k:col0 + 2 * k + 1], (q, LANE))
        b = jnp.broadcast_to(vals[:, col0 + 2 * k + 1:col0 + 2 * k + 2], (q, LANE))
        parts.append(jnp.where(lane < SSD_HEAD_DIM, a, b))
    return jnp.concatenate(parts, axis=-1)


def _split_dot(lhs, rhs):
    if lhs.dtype == F32:
        hi = lhs.astype(BF16)
        lo = (lhs - hi.astype(F32)).astype(BF16)
        return (jnp.dot(hi, rhs, preferred_element_type=F32)
                + jnp.dot(lo, rhs, preferred_element_type=F32))
    hi = rhs.astype(BF16)
    lo = (rhs - hi.astype(F32)).astype(BF16)
    return (jnp.dot(lhs, hi, preferred_element_type=F32)
            + jnp.dot(lhs, lo, preferred_element_type=F32))


def _ssd_kernel(xs_ref, bm_ref, cm_ref, dt_ref, dtt_ref,
                wx_ref, wb_ref, wc_ref, bx_ref, bb_ref, bc_ref,
                dtb_row_ref, dtb_col_ref, alog_row_ref, alog_col_ref, dskip_ref, *rest, seq, zero_init):
    h0_ref = None if zero_init else rest[0]
    (y_ref, hout_ref, padx_ref, padn_ref, xc_ref, bcs_ref, ccs_ref,
     yfwd_ref, ybwd_ref, state_ref) = rest[(0 if zero_init else 1):]
    q = SSD_CHUNK
    n_chunks = seq // q
    nh = 2 * SSD_HPG

    _conv_silu(xs_ref, wx_ref, bx_ref, padx_ref, xc_ref, seq)
    _conv_silu(bm_ref, wb_ref, bb_ref, padn_ref, bcs_ref, seq)
    _conv_silu(cm_ref, wc_ref, bc_ref, padn_ref, ccs_ref, seq)

    a_row = -jnp.exp(alog_row_ref[0])
    a_col = -jnp.exp(alog_col_ref[0])
    dtb_row = dtb_row_ref[0]
    dtb_col = dtb_col_ref[0]
    row_i = lax.broadcasted_iota(jnp.int32, (q, q), 0)
    col_i = lax.broadcasted_iota(jnp.int32, (q, q), 1)
    lower = row_i >= col_i
    upper = row_i <= col_i
    tri_lo = jnp.where(lower, 1.0, 0.0).astype(BF16)
    tri_up = jnp.where(upper, 1.0, 0.0).astype(BF16)
    lane = lax.broadcasted_iota(jnp.int32, (q, LANE), 1)
    head_lo = lane < SSD_HEAD_DIM

    def prologue(t, direction):
        rows = pl.ds(pl.multiple_of(t * q, q), q)
        dt = _softplus(dt_ref[0, 0, rows, :] + dtb_row)
        dtt = _softplus(dtt_ref[0, 0, :, rows] + dtb_col)
        da = dt * (a_row * LOG2E)
        dat = dtt * (a_col * LOG2E)
        if direction == 0:
            cum = _split_dot(tri_lo, da)
            cumt = _split_dot(dat, tri_up)
        else:
            cum = _split_dot(tri_up, da)
            cumt = _split_dot(dat, tri_lo)
        edge = q - 1 if direction == 0 else 0
        wt = dtt * jnp.exp2(cumt[:, edge:edge + 1] - cumt)
        bc = bcs_ref[rows, :]
        cc = ccs_ref[rows, :]
        cb = lax.dot_general(cc, bc, (((1,), (1,)), ((), ())), preferred_element_type=F32)
        bctf = bc.astype(F32).T
        return cum, cumt - jnp.log2(dtt), wt, cb, bctf

    def chunk(t, direction, pro):
        cum, rowt, wt, cb, bctf = pro
        rows = pl.ds(pl.multiple_of(t * q, q), q)
        mask = lower if direction == 0 else upper
        edge = q - 1 if direction == 0 else 0
        total = cum[edge:edge + 1, :]
        col0 = direction * SSD_HPG
        xb = xc_ref[rows, :]
        ccf = ccs_ref[rows, :].astype(F32)
        st = state_ref[direction]
        stb = st.astype(BF16)
        zero = jnp.zeros((q, LANE), BF16)
        ys, news = [], []
        for k in range(SSD_HPG // 2):
            xp = xb[:, k * LANE:(k + 1) * LANE]
            sp = stb[:, k * LANE:(k + 1) * LANE]
            rhs_x = jnp.concatenate([jnp.where(head_lo, xp, zero), jnp.where(head_lo, zero, xp)], axis=0)
            rhs_s = jnp.concatenate([jnp.where(head_lo, sp, zero), jnp.where(head_lo, zero, sp)], axis=0)
            ms, cs, ws = [], [], []
            for half in range(2):
                c = col0 + 2 * k + half
                bcol = jnp.broadcast_to(cum[:, c:c + 1], (q, q))
                seg = bcol - jnp.broadcast_to(rowt[c:c + 1, :], (q, q))
                ms.append((cb * jnp.exp2(jnp.where(mask, seg, -jnp.inf))).astype(BF16))
                cs.append((ccf * jnp.exp2(bcol)).astype(BF16))
                ws.append((bctf * jnp.broadcast_to(wt[c:c + 1, :], (q, q))).astype(BF16))
            ys.append(jnp.dot(jnp.concatenate(ms + cs, axis=1), jnp.concatenate([rhs_x, rhs_s], axis=0),
                              preferred_element_type=F32))
            news.append(jnp.dot(jnp.concatenate(ws, axis=1), rhs_x, preferred_element_type=F32))
        state_ref[direction] = st * _pair_expand(jnp.exp2(total), col0) + jnp.concatenate(news, axis=-1)
        return rows, jnp.concatenate(ys, axis=-1)

    state_ref[...] = jnp.zeros(state_ref.shape, F32) if zero_init else h0_ref[0, :, 0]

    def both(i, carry):
        pro_f, pro_b = carry
        nxt_f = prologue(jnp.minimum(i + 1, n_chunks - 1), 0)
        nxt_b = prologue(jnp.maximum(n_chunks - 2 - i, 0), 1)
        rows, y = chunk(i, 0, pro_f)
        yfwd_ref[rows, :] = y
        rows, y = chunk(n_chunks - 1 - i, 1, pro_b)
        ybwd_ref[rows, :] = y
        return nxt_f, nxt_b

    lax.fori_loop(0, n_chunks, both, (prologue(0, 0), prologue(n_chunks - 1, 1)))
    hout_ref[0, :, 0] = state_ref[...]

    dskip = dskip_ref[...]

    def finish(t, carry):
        rows = pl.ds(pl.multiple_of(t * q, q), q)
        y = yfwd_ref[rows, :] + ybwd_ref[rows, :] + xc_ref[rows, :].astype(F32) * dskip
        y_ref[0, rows, :] = y.astype(y_ref.dtype)
        return carry

    lax.fori_loop(0, n_chunks, finish, 0)


def _group_dt_order():
    order = []
    for g in range(SSD_GROUPS):
        for direction in range(2):
            order += [direction * SSD_HEADS + g * SSD_HPG + r for r in range(SSD_HPG)]
    return order


def _ssd(proj, dt, dtt, h0, conv_w, conv_b, dt_bias, a_log, d_skip):
    bsz, seq, _ = proj.shape
    gc = SSD_GROUP_CH
    ns = SSD_STATE
    nh = 2 * SSD_HPG
    xs_blk, b_blk, c_blk = EVO_XS // gc, EVO_B // ns, EVO_C // ns
    cw_b0, cw_c0 = SSD_INNER // ns, (SSD_INNER + SSD_GROUPS * ns) // ns
    conv_b2 = conv_b.reshape(1, -1)
    d_skip_x = jnp.repeat(d_skip, SSD_HEAD_DIM).reshape(1, SSD_INNER)
    order = jnp.array(_group_dt_order(), jnp.int32)
    dt_bias_g = dt_bias.reshape(-1)[order].reshape(SSD_GROUPS, nh)
    a_log_g = a_log.reshape(-1)[order].reshape(SSD_GROUPS, nh)
    in_specs = [
        pl.BlockSpec((1, seq, gc), lambda b, g: (b, 0, xs_blk + g)),
        pl.BlockSpec((1, seq, ns), lambda b, g: (b, 0, b_blk + g)),
        pl.BlockSpec((1, seq, ns), lambda b, g: (b, 0, c_blk + g)),
        pl.BlockSpec((1, 1, seq, nh), lambda b, g: (b, g, 0, 0)),
        pl.BlockSpec((1, 1, nh, seq), lambda b, g: (b, g, 0, 0)),
        pl.BlockSpec((SSD_CONV, gc), lambda b, g: (0, g)),
        pl.BlockSpec((SSD_CONV, ns), lambda b, g: (0, cw_b0 + g)),
        pl.BlockSpec((SSD_CONV, ns), lambda b, g: (0, cw_c0 + g)),
        pl.BlockSpec((1, gc), lambda b, g: (0, g)),
        pl.BlockSpec((1, ns), lambda b, g: (0, cw_b0 + g)),
        pl.BlockSpec((1, ns), lambda b, g: (0, cw_c0 + g)),
        pl.BlockSpec((1, 1, nh), lambda b, g: (g, 0, 0)),
        pl.BlockSpec((1, nh, 1), lambda b, g: (g, 0, 0)),
        pl.BlockSpec((1, 1, nh), lambda b, g: (g, 0, 0)),
        pl.BlockSpec((1, nh, 1), lambda b, g: (g, 0, 0)),
        pl.BlockSpec((1, gc), lambda b, g: (0, g)),
    ]
    state_spec = pl.BlockSpec((1, 2, 1, ns, gc), lambda b, g: (b, 0, g, 0, 0))
    args = [proj, proj, proj, dt, dtt,
            conv_w, conv_w, conv_w, conv_b2, conv_b2, conv_b2,
            dt_bias_g.reshape(SSD_GROUPS, 1, nh), dt_bias_g.reshape(SSD_GROUPS, nh, 1),
            a_log_g.reshape(SSD_GROUPS, 1, nh), a_log_g.reshape(SSD_GROUPS, nh, 1), d_skip_x]
    if h0 is not None:
        in_specs.append(state_spec)
        args.append(h0)
    return pl.pallas_call(
        functools.partial(_ssd_kernel, seq=seq, zero_init=h0 is None),
        out_shape=[jax.ShapeDtypeStruct((bsz, seq, SSD_INNER), BF16),
                   jax.ShapeDtypeStruct((bsz, 2, SSD_GROUPS, ns, gc), F32)],
        grid=(bsz, SSD_GROUPS),
        in_specs=in_specs,
        out_specs=[pl.BlockSpec((1, seq, gc), lambda b, g: (b, 0, g)), state_spec],
        scratch_shapes=[pltpu.VMEM((seq + 2 * SUBLANE, gc), F32),
                        pltpu.VMEM((seq + 2 * SUBLANE, ns), F32),
                        pltpu.VMEM((seq, gc), BF16),
                        pltpu.VMEM((seq, ns), BF16),
                        pltpu.VMEM((seq, ns), BF16),
                        pltpu.VMEM((seq, gc), F32),
                        pltpu.VMEM((seq, gc), F32),
                        pltpu.VMEM((2, ns, gc), F32)],
        compiler_params=_params("arbitrary", "arbitrary"),
        name="ssd",
    )(*args)


CV_ROWS = 256
CV_HALO = 16
CV_TC = 128
CV_SPAN = CV_ROWS + 2 * CV_HALO - SUBLANE


def _cvconv_kernel(glu_ref, w_ref, b_ref, o_ref, pad_ref, sh_ref, *, seq):
    ch = glu_ref.shape[-1]
    pad_ref[pl.ds(0, CV_HALO), :] = jnp.zeros((CV_HALO, ch), F32)
    pad_ref[pl.ds(seq + CV_HALO, CV_HALO), :] = jnp.zeros((CV_HALO, ch), F32)
    pad_ref[pl.ds(CV_HALO, seq), :] = glu_ref[0].astype(F32)
    w = w_ref[...]
    bias = b_ref[...]
    first = CV_HALO - CONV_WIDTH // 2
    rows = min(CV_ROWS * CV_TC // ch, seq)
    span = rows + 2 * CV_HALO - SUBLANE

    def body(t, carry):
        base = pl.multiple_of(t * rows, rows)
        win = pad_ref[pl.ds(base, rows + 2 * CV_HALO), :]
        for s in range(SUBLANE):
            sh_ref[s, pl.ds(0, span), :] = win[s:s + span, :]
        acc = jnp.broadcast_to(bias, (rows, ch))
        for k in range(CONV_WIDTH):
            a, s = divmod(first + k, SUBLANE)
            acc = acc + w[k:k + 1, :] * sh_ref[s, pl.ds(a * SUBLANE, rows), :]
        o_ref[0, pl.ds(base, rows), :] = acc.astype(o_ref.dtype)
        return carry

    lax.fori_loop(0, seq // rows, body, 0)


def _cvconv(proj, cv_w, cv_b):
    bsz, seq, _ = proj.shape
    tc = CV_TC * max(1, 1024 // seq)
    return pl.pallas_call(
        functools.partial(_cvconv_kernel, seq=seq),
        out_shape=jax.ShapeDtypeStruct((bsz, seq, CONV_CH), BF16),
        grid=(bsz, CONV_CH // tc),
        in_specs=[pl.BlockSpec((1, seq, tc), lambda b, j: (b, 0, EVO_GLU // tc + j)),
                  pl.BlockSpec((CONV_WIDTH, tc), lambda b, j: (0, j)),
                  pl.BlockSpec((1, tc), lambda b, j: (0, j))],
        out_specs=pl.BlockSpec((1, seq, tc), lambda b, j: (b, 0, j)),
        scratch_shapes=[pltpu.VMEM((seq + 2 * CV_HALO, tc), F32),
                        pltpu.VMEM((SUBLANE, CV_SPAN, tc), F32)],
        compiler_params=_params("arbitrary", "arbitrary"),
        name="cvconv",
    )(proj, cv_w, cv_b.reshape(1, CONV_CH))


def _gmlp_rows(u_ref, v_ref, gc_ref, lng_ref, lnb_ref, ws_ref, bias_ref, row0, n_rows):
    lng = lng_ref[...]
    lnb = lnb_ref[...]
    bias = bias_ref[...]
    out = []
    for c in range(n_rows // MLP_CHUNK):
        rows = pl.ds(row0 + c * MLP_CHUNK, MLP_CHUNK)
        vn = _layer_norm(v_ref[0, rows, :].astype(F32), lng, lnb).astype(BF16)
        mixed = jnp.concatenate(
            [jnp.dot(ws_ref[gi], vn[:, gi * MLP_GROUP_CH:(gi + 1) * MLP_GROUP_CH], preferred_element_type=F32)
             for gi in range(MLP_GROUPS)], axis=-1)
        u = u_ref[0, rows, :].astype(F32)
        out.append((u * (mixed + bias) * gc_ref[0, rows, :].astype(F32)).astype(BF16))
    return jnp.concatenate(out, axis=0)


def _rope(t, cos, sin_signed):
    width = t.shape[1]
    reps = width // LANE
    half = ATT_HEAD_DIM // 2
    lane = lax.broadcasted_iota(jnp.int32, t.shape, 1)
    first_half = (lane % ATT_HEAD_DIM) < half
    swapped = jnp.where(first_half, pltpu.roll(t, width - half, 1), pltpu.roll(t, half, 1))
    c = jnp.concatenate([cos] * reps, axis=-1)
    s = jnp.concatenate([sin_signed] * reps, axis=-1)
    return t * c + swapped * s


def _att_head_order():
    order = []
    for c in range(ATT_HEADS // 2):
        j, r = divmod(c, ATT_REP)
        order += [(2 * j) * ATT_REP + r, (2 * j + 1) * ATT_REP + r]
    return order


def _att_reorder(t, axis):
    shape = t.shape
    kvp = ATT_KV_HEADS // 2
    t = jnp.moveaxis(t, axis, 0).reshape((kvp, 2, ATT_REP, ATT_HEAD_DIM) + shape[:axis] + shape[axis + 1:])
    t = jnp.swapaxes(t, 1, 2).reshape((shape[axis],) + shape[:axis] + shape[axis + 1:])
    return jnp.moveaxis(t, 0, axis)


def _attn_kernel(*refs, seq, ctx_len, latent):
    if latent:
        (q_ref, gd_ref, k_ref, v_ref, kc_ref, vc_ref, sink_ref, cos_ref, sin_ref,
         o_ref, kpad_ref, vtp_ref, vct_ref) = refs
    else:
        q_ref, gd_ref, kc_ref, vc_ref, sink_ref, o_ref, vct_ref = refs
    i = pl.program_id(1)
    w = ATT_BLOCK
    nt = (((1,), (1,)), ((), ()))

    vrows = LANE + ATT_ONES

    @pl.when(i == 0)
    def _():
        vct = vc_ref[0].astype(F32).T.astype(BF16)
        for j in range(ATT_KV_CH // LANE):
            vct_ref[pl.ds(j * vrows, LANE), :] = vct[j * LANE:(j + 1) * LANE, :]
            vct_ref[pl.ds(j * vrows + LANE, ATT_ONES), :] = jnp.ones((ATT_ONES, ctx_len), BF16)
        if latent:
            kpad_ref[pl.ds(0, w), :] = jnp.zeros((w, ATT_KV_CH), BF16)
            kpad_ref[pl.ds(seq + w, w), :] = jnp.zeros((w, ATT_KV_CH), BF16)
            kpad_ref[pl.ds(w, seq), :] = _rope(k_ref[0].astype(F32), cos_ref[...], sin_ref[...]).astype(BF16)
            for j in range(ATT_KV_CH // LANE):
                vtp_ref[pl.ds(j * vrows, LANE), pl.ds(0, w)] = jnp.zeros((LANE, w), BF16)
                vtp_ref[pl.ds(j * vrows, LANE), pl.ds(seq + w, w)] = jnp.zeros((LANE, w), BF16)
                vtp_ref[pl.ds(j * vrows + LANE, ATT_ONES), :] = jnp.ones((ATT_ONES, seq + 2 * w), BF16)
            for t in range(seq // w):
                vt = v_ref[0, pl.ds(t * w, w), :].astype(F32).T.astype(BF16)
                for j in range(ATT_KV_CH // LANE):
                    vtp_ref[pl.ds(j * vrows, LANE), pl.ds((t + 1) * w, w)] = vt[j * LANE:(j + 1) * LANE, :]

    if latent:
        base = pl.multiple_of(i * w, w)
        q = _rope(q_ref[0].astype(F32), cos_ref[pl.ds(base, w), :], sin_ref[pl.ds(base, w), :])
        kwin = kpad_ref[pl.ds(base, 3 * w), :]
        qi = lax.broadcasted_iota(jnp.int32, (w, w), 1)
        u = lax.broadcasted_iota(jnp.int32, (w, w), 0)
        before = (u - w >= qi - ATT_WINDOW) & (base + u - w >= 0)
        after = (u + w <= qi + ATT_WINDOW) & (base + u + w < seq)
        bias_lo = jnp.where(before, 0.0, -jnp.inf)
        bias_hi = jnp.where(after, 0.0, -jnp.inf)
        bias_lo = jnp.concatenate([bias_lo, bias_lo], axis=1)
        bias_hi = jnp.concatenate([bias_hi, bias_hi], axis=1)
    else:
        q = q_ref[0].astype(F32)
    qs = (q * (ATT_SCALE * LOG2E)).astype(BF16)
    kc = kc_ref[0]
    lane = lax.broadcasted_iota(jnp.int32, (w, LANE), 1)
    low_lanes = lane < ATT_HEAD_DIM
    low_rows = lax.broadcasted_iota(jnp.int32, (LANE, w), 0) < ATT_HEAD_DIM
    zero = jnp.zeros((w, LANE), BF16)
    n_cols = ATT_HEADS // 2

    def scores(c):
        kv = slice((c // ATT_REP) * LANE, (c // ATT_REP + 1) * LANE)
        qc = qs[:, c * LANE:(c + 1) * LANE]
        rhs = jnp.concatenate([jnp.where(low_lanes, qc, zero), jnp.where(low_lanes, zero, qc)], axis=0)
        s_ctx = lax.dot_general(kc[:, kv], rhs, nt, preferred_element_type=F32)
        s_lat = lax.dot_general(kwin[:, kv], rhs, nt, preferred_element_type=F32) if latent else None
        return s_ctx, s_lat

    def finish(c, s_ctx, s_lat):
        j = c // ATT_REP
        snk = sink_ref[0, :, pl.ds(c * 2 * w, 2 * w)]
        m = jnp.maximum(jnp.max(s_ctx, axis=0, keepdims=True), snk)
        if latent:
            s_lo = s_lat[:w] + bias_lo
            s_mid = s_lat[w:2 * w]
            s_hi = s_lat[2 * w:] + bias_hi
            m = jnp.maximum(m, jnp.maximum(jnp.maximum(jnp.max(s_lo, axis=0, keepdims=True),
                                                       jnp.max(s_mid, axis=0, keepdims=True)),
                                           jnp.max(s_hi, axis=0, keepdims=True)))
        probs = [jnp.exp2(s_ctx - m)]
        vals = [vct_ref[pl.ds(j * vrows, vrows), :]]
        if latent:
            probs += [jnp.exp2(s_lo - m), jnp.exp2(s_mid - m), jnp.exp2(s_hi - m)]
            vals.append(vtp_ref[pl.ds(j * vrows, vrows), pl.ds(base, 3 * w)])
        acc = jnp.dot(jnp.concatenate(vals, axis=1), jnp.concatenate(probs, axis=0).astype(BF16),
                      preferred_element_type=F32)
        den = acc[LANE:LANE + 1, :] + jnp.exp2(snk - m)
        acc = acc[:LANE, :] * (1.0 / den)
        return jnp.where(low_rows, acc[:, :w], acc[:, w:]).T

    outs = []
    ahead = 3
    pending = [scores(c) for c in range(ahead)]
    for c in range(n_cols):
        if c + ahead < n_cols:
            pending.append(scores(c + ahead))
        outs.append(finish(c, *pending.pop(0)))
    gd = gd_ref[0].astype(F32)
    o_ref[0] = (jnp.concatenate(outs, axis=-1) * gd).astype(o_ref.dtype)


def _attention(proj, proj_ctx, sink, cos, sin_signed, latent, ctx_col=OD_K):
    bsz, seq, _ = proj.shape
    ctx_len = proj_ctx.shape[1]
    w = ATT_BLOCK
    qch = ATT_HEADS * ATT_HEAD_DIM
    kvc = ATT_KV_CH
    q_spec = pl.BlockSpec((1, w, qch), lambda b, i: (b, i, OD_Q // qch))
    gd_spec = pl.BlockSpec((1, w, qch), lambda b, i: (b, i, OD_GD // qch))
    kc_spec = pl.BlockSpec((1, ctx_len, kvc), lambda b, i: (b, 0, ctx_col // kvc))
    vc_spec = pl.BlockSpec((1, ctx_len, kvc), lambda b, i: (b, 0, ctx_col // kvc + 1))
    sink2 = jnp.repeat(sink[jnp.array(_att_head_order(), jnp.int32)] * LOG2E, w).reshape(1, 1, ATT_HEADS * w)
    sink_spec = pl.BlockSpec((1, 1, ATT_HEADS * w), lambda b, i: (0, 0, 0))
    vrows = (kvc // LANE) * (LANE + ATT_ONES)
    vct = pltpu.VMEM((vrows, ctx_len), BF16)
    if latent:
        in_specs = [q_spec, gd_spec,
                    pl.BlockSpec((1, seq, kvc), lambda b, i: (b, 0, OD_K // kvc)),
                    pl.BlockSpec((1, seq, kvc), lambda b, i: (b, 0, OD_VA // kvc)),
                    kc_spec, vc_spec, sink_spec,
                    pl.BlockSpec((seq, LANE), lambda b, i: (0, 0)),
                    pl.BlockSpec((seq, LANE), lambda b, i: (0, 0))]
        args = (proj, proj, proj, proj, proj_ctx, proj_ctx, sink2, cos, sin_signed)
        scratch = [pltpu.VMEM((seq + 2 * w, kvc), BF16), pltpu.VMEM((vrows, seq + 2 * w), BF16), vct]
    else:
        in_specs = [q_spec, gd_spec, kc_spec, vc_spec, sink_spec]
        args = (proj, proj, proj_ctx, proj_ctx, sink2)
        scratch = [vct]
    return pl.pallas_call(
        functools.partial(_attn_kernel, seq=seq, ctx_len=ctx_len, latent=latent),
        out_shape=jax.ShapeDtypeStruct((bsz, seq, qch), BF16),
        grid=(bsz, seq // w),
        in_specs=in_specs,
        out_specs=pl.BlockSpec((1, w, qch), lambda b, i: (b, i, 0)),
        scratch_shapes=scratch,
        compiler_params=_params("arbitrary", "arbitrary"),
        name="attention" if latent else "ctx_attention",
    )(*args)


OUTPROJ_TM = 1024
OUTPROJ_SUB = 512


def _outproj_kernel(*refs, even):
    if even:
        (y_ref, z_ref, cv_ref, gate_ref, nw_ref, cvg_ref, cvb_ref,
         x_ref, g_ref, lng_ref, lnb_ref, w_ref, o_ref) = refs
    else:
        (u_ref, v_ref, gc_ref, mlng_ref, mlnb_ref, ws_ref, bias_ref, yb_ref,
         x_ref, g_ref, lng_ref, lnb_ref, w_ref, o_ref) = refs
    half = w_ref.shape[0] // 2
    tm = x_ref.shape[1]
    sub = min(OUTPROJ_SUB, tm)

    def project(k):
        rows = pl.ds(k * sub, sub)
        if even:
            t = y_ref[0, rows, :].astype(F32) * z_ref[0, rows, :].astype(F32)
            ya = (t * lax.rsqrt(jnp.mean(t * t, -1, keepdims=True) + LN_EPS) * nw_ref[...]).astype(BF16)
            yb = (_silu(_layer_norm(cv_ref[0, rows, :].astype(F32), cvg_ref[...], cvb_ref[...]))
                  * gate_ref[0, rows, :].astype(F32)).astype(BF16)
        else:
            ya = _gmlp_rows(u_ref, v_ref, gc_ref, mlng_ref, mlnb_ref, ws_ref, bias_ref, k * sub, sub)
            yb = yb_ref[0, rows, :]
        return (jnp.dot(ya, w_ref[pl.ds(0, half), :], preferred_element_type=F32)
                + jnp.dot(yb, w_ref[pl.ds(half, half), :], preferred_element_type=F32))

    n_sub = tm // sub
    y = project(0)
    for k in range(n_sub):
        nxt = project(k + 1) if k + 1 < n_sub else None
        rows = pl.ds(k * sub, sub)
        r = DEEPNORM_ALPHA * x_ref[0, rows, :] + g_ref[0] * y
        o_ref[0, rows, :] = _layer_norm(r, lng_ref[...], lnb_ref[...])
        y = nxt


def _outproj(mix_args, mix_specs, x, g, ln_g, ln_b, w_out, even, tm):
    bsz, seq, d = x.shape
    vec = pl.BlockSpec((1, d), lambda b, i: (0, 0))
    in_specs = list(mix_specs) + [
        pl.BlockSpec((1, tm, d), lambda b, i: (b, i, 0)),
        pl.BlockSpec((1, 1, d), lambda b, i: (b, 0, 0)),
        vec, vec,
        pl.BlockSpec(w_out.shape, lambda b, i: (0, 0))]
    return pl.pallas_call(
        functools.partial(_outproj_kernel, even=even),
        out_shape=jax.ShapeDtypeStruct((bsz, seq, d), F32),
        grid=(bsz, seq // tm),
        in_specs=in_specs,
        out_specs=pl.BlockSpec((1, tm, d), lambda b, i: (b, i, 0)),
        compiler_params=_params("arbitrary", "arbitrary"),
        name="outproj_even" if even else "outproj_odd",
    )(*mix_args, x, g, ln_g.reshape(1, d), ln_b.reshape(1, d), w_out)


def _outproj_even(y_ssd, cv, proj, norm_w, cv_ln_g, cv_ln_b, x, g, ln_g, ln_b, w_out):
    seq = x.shape[1]
    tm = min(seq, OUTPROJ_TM)
    ch = SSD_INNER
    blk = lambda col: pl.BlockSpec((1, tm, ch), lambda b, i: (b, i, col // ch))
    vec = pl.BlockSpec((1, ch), lambda b, i: (0, 0))
    specs = [blk(0), blk(EVO_Z), blk(0), blk(EVO_GATE), vec, vec, vec]
    args = (y_ssd, proj, cv, proj, norm_w.reshape(1, ch), cv_ln_g.reshape(1, ch), cv_ln_b.reshape(1, ch))
    return _outproj(args, specs, x, g, ln_g, ln_b, w_out, True, tm)


def _outproj_odd(proj, yd, mlp_ln_g, mlp_ln_b, ws, bs, x, g, ln_g, ln_b, w_out):
    seq = x.shape[1]
    tm = min(seq, OUTPROJ_TM)
    ch = MLP_CH
    blk = lambda col: pl.BlockSpec((1, tm, ch), lambda b, i: (b, i, col // ch))
    vec = pl.BlockSpec((1, ch), lambda b, i: (0, 0))
    bias = jnp.repeat(bs.T, MLP_GROUP_CH, axis=1)
    specs = [blk(OD_U), blk(OD_V), blk(OD_GC), vec, vec,
             pl.BlockSpec((MLP_GROUPS, MLP_CHUNK, MLP_CHUNK), lambda b, i: (0, 0, 0)),
             pl.BlockSpec((MLP_CHUNK, ch), lambda b, i: (0, 0)), blk(0)]
    args = (proj, proj, proj, mlp_ln_g.reshape(1, ch), mlp_ln_b.reshape(1, ch), ws.astype(BF16), bias, yd)
    return _outproj(args, specs, x, g, ln_g, ln_b, w_out, False, tm)


def _rope_tables(seq):
    t = jnp.arange(seq)
    row = (t // GRID_W).astype(F32)
    col = (t % GRID_W).astype(F32)
    n_freq = ATT_HEAD_DIM // 4
    inv = ROPE_BASE ** (-jnp.arange(n_freq, dtype=F32) / n_freq)
    ang = jnp.concatenate([row[:, None] * inv, col[:, None] * inv], -1)
    cos, sin = jnp.cos(ang), jnp.sin(ang)
    reps = LANE // ATT_HEAD_DIM
    return (jnp.tile(jnp.concatenate([cos, cos], -1), (1, reps)),
            jnp.tile(jnp.concatenate([-sin, sin], -1), (1, reps)))


def _even_weights(w_in):
    o_z, o_xbc, o_dt, o_glu, o_gate = _offsets(SSD_INNER, SSD_INNER + 2 * SSD_BC, 2 * SSD_HEADS, 2 * CONV_CH, CONV_CH)
    main = jnp.concatenate([w_in[:, o_z:o_xbc], w_in[:, o_glu:o_gate], w_in[:, o_gate:],
                            w_in[:, o_xbc:o_dt]], axis=1)
    w_dt = w_in[:, o_dt:o_glu][:, jnp.array(_group_dt_order(), jnp.int32)]
    return main.astype(BF16), w_dt.astype(BF16)


def _odd_weights(w_in):
    o_q = 3 * MLP_CH
    o_k = o_q + ATT_HEADS * ATT_HEAD_DIM
    o_gd = o_k + 2 * ATT_KV_CH
    return jnp.concatenate([w_in[:, :o_q], _att_reorder(w_in[:, o_q:o_k], 1), _att_reorder(w_in[:, o_gd:], 1),
                            w_in[:, o_k:o_gd]], axis=1).astype(BF16)


def _odd_out_weights(w_out):
    return jnp.concatenate([w_out[:MLP_CH], _att_reorder(w_out[MLP_CH:], 0)], axis=0).astype(BF16)


def kernel(x, c, ctx, c_ctx, mod_w, mod_b, ln_g, ln_b, ev_w_in, ev_ssd_conv_w, ev_ssd_conv_b, ev_dt_bias, ev_a_log, ev_d_skip, ev_ssd_norm, ev_cv_w, ev_cv_b, ev_cv_ln_g, ev_cv_ln_b, ev_w_out, od_w_in, od_mlp_ln_g, od_mlp_ln_b, od_ws, od_bs, od_sink, od_w_out):
    bsz, seq, d = x.shape
    ctx_len = ctx.shape[1]
    assert d == D_MODEL and mod_w.shape[0] == DEPTH
    for n in (seq, ctx_len):
        assert n % SSD_CHUNK == 0 and n % ATT_BLOCK == 0 and n % MLP_CHUNK == 0
        assert n % min(n, OUTPROJ_TM) == 0 and min(n, OUTPROJ_TM) % min(n, OUTPROJ_SUB) == 0
        assert n % min(n, CV_ROWS) == 0
    for n in (seq, bsz * ctx_len):
        assert n % min(n, INPROJ_TM) == 0 and min(n, INPROJ_TM) % min(n, INPROJ_SUB) == 0
    cos, sin_signed = _rope_tables(seq)
    rows = -(-(bsz + 1) // SUBLANE) * SUBLANE
    cond = jnp.concatenate([c, c_ctx[None, :], jnp.zeros((rows - bsz - 1, d), F32)], axis=0)
    mod = _modulation(cond, mod_w, mod_b)

    for layer in range(DEPTH):
        last = layer == DEPTH - 1
        i = layer // 2
        m = mod[layer]
        sh_x, sc_x, g_x = (m[:bsz, None, k * d:(k + 1) * d] for k in range(3))
        sh_c, sc_c, g_c = (jnp.broadcast_to(m[bsz:bsz + 1, None, k * d:(k + 1) * d], (bsz, 1, d)) for k in range(3))
        if layer % 2 == 0:
            w_main, w_dt = _even_weights(ev_w_in[i])
            w_out = ev_w_out[i].astype(BF16)
            ssd_args = (ev_ssd_conv_w[i], ev_ssd_conv_b[i], ev_dt_bias[i], ev_a_log[i], ev_d_skip[i])
            p_c, dt_c, dtt_c = _inproj_ctx(ctx, sc_c, sh_c, w_main, w_dt)
            y_c, h_c = _ssd(p_c, dt_c, dtt_c, None, *ssd_args)
            p_x, dt_x, dtt_x = _inproj(x, sc_x, sh_x, w_main, w_dt)
            y_x, _ = _ssd(p_x, dt_x, dtt_x, h_c, *ssd_args)
            cv_x = _cvconv(p_x, ev_cv_w[i], ev_cv_b[i])
            x = _outproj_even(y_x, cv_x, p_x, ev_ssd_norm[i], ev_cv_ln_g[i], ev_cv_ln_b[i],
                              x, g_x, ln_g[layer], ln_b[layer], w_out)
            if not last:
                cv_c = _cvconv(p_c, ev_cv_w[i], ev_cv_b[i])
                ctx = _outproj_even(y_c, cv_c, p_c, ev_ssd_norm[i], ev_cv_ln_g[i], ev_cv_ln_b[i],
                                    ctx, g_c, ln_g[layer], ln_b[layer], w_out)
        else:
            w_main = _odd_weights(od_w_in[i])
            w_out = _odd_out_weights(od_w_out[i])
            if last:
                p_c = _inproj_ctx(ctx, sc_c, sh_c, w_main[:, OD_K:OD_K + INPROJ_TN], plan=(("id", 0),))[0]
                ctx_col = 0
            else:
                p_c = _inproj_ctx(ctx, sc_c, sh_c, w_main)[0]
                ctx_col = OD_K
            p_x = _inproj(x, sc_x, sh_x, w_main)[0]
            yd_x = _attention(p_x, p_c, od_sink[i], cos, sin_signed, True, ctx_col)
            mlp_args = (od_mlp_ln_g[i], od_mlp_ln_b[i], od_ws[i], od_bs[i])
            x = _outproj_odd(p_x, yd_x, *mlp_args, x, g_x, ln_g[layer], ln_b[layer], w_out)
            if not last:
                yd_c = _attention(p_c, p_c, od_sink[i], cos, sin_signed, False)
                ctx = _outproj_odd(p_c, yd_c, *mlp_args, ctx, g_c, ln_g[layer], ln_b[layer], w_out)
    return x
```

```python
import functools
import math

import jax
import jax.numpy as jnp
from jax import lax
from jax.experimental import pallas as pl
from jax.experimental.pallas import tpu as pltpu

F32 = jnp.float32
BF16 = jnp.bfloat16

D_MODEL = 1024
DEPTH = 4
GRID_W = 64

SSD_HEADS = 16
SSD_HEAD_DIM = 64
SSD_INNER = SSD_HEADS * SSD_HEAD_DIM
SSD_GROUPS = 2
SSD_HPG = SSD_HEADS // SSD_GROUPS
SSD_STATE = 128
SSD_CHUNK = 128
SSD_CONV = 5
SSD_GROUP_CH = SSD_HPG * SSD_HEAD_DIM
CONV_CH = 1024
CONV_WIDTH = 31
MLP_CH = 1024
MLP_GROUPS = 8
MLP_GROUP_CH = MLP_CH // MLP_GROUPS
MLP_CHUNK = 128
ATT_HEADS = 16
ATT_KV_HEADS = 4
ATT_REP = ATT_HEADS // ATT_KV_HEADS
ATT_HEAD_DIM = 64
ATT_WINDOW = 128
ATT_BLOCK = 128
ATT_SCALE = ATT_HEAD_DIM ** -0.5
LOG2E = math.log2(math.e)
ROPE_BASE = 10000.0
ATT_KV_CH = ATT_KV_HEADS * ATT_HEAD_DIM
ATT_ONES = 16

DEEPNORM_ALPHA = (2 * DEPTH) ** 0.25
LN_EPS = 1e-5

LANE = 128
SUBLANE = 8
VMEM_LIMIT = 56 * 1024 * 1024


def _offsets(*widths):
    return tuple(sum(widths[:i]) for i in range(len(widths)))


SSD_BC = SSD_GROUPS * SSD_STATE
EV_Z, EV_VAL, EV_GT, EV_GATE, EV_XS, EV_B, EV_C = _offsets(SSD_INNER, CONV_CH, CONV_CH, CONV_CH, SSD_INNER, SSD_BC, SSD_BC)
EVO_Z, EVO_GLU, EVO_GATE, EVO_XS, EVO_B, EVO_C = _offsets(SSD_INNER, CONV_CH, CONV_CH, SSD_INNER, SSD_BC, SSD_BC)
OD_U, OD_V, OD_GC, OD_Q, OD_GD, OD_K, OD_VA = _offsets(MLP_CH, MLP_CH, MLP_CH, ATT_HEADS * ATT_HEAD_DIM,
                                                       ATT_HEADS * ATT_HEAD_DIM, ATT_KV_CH, ATT_KV_CH)


def _silu(t):
    return t * jax.nn.sigmoid(t)


def _gelu_tanh(t):
    c = -2.0 * math.sqrt(2.0 / math.pi) * LOG2E
    return t / (1.0 + jnp.exp2(t * (c + (c * 0.044715) * (t * t))))


def _softplus(t):
    return jnp.maximum(t, 0.0) + jnp.log1p(jnp.exp(-jnp.abs(t)))


def _layer_norm(t, g, b):
    mu = jnp.mean(t, -1, keepdims=True)
    d = t - mu
    var = jnp.mean(d * d, -1, keepdims=True)
    return d * lax.rsqrt(var + LN_EPS) * g + b


def _params(*sem):
    return pltpu.CompilerParams(dimension_semantics=sem, vmem_limit_bytes=VMEM_LIMIT)


def _mod_kernel(c_ref, w_ref, b_ref, o_ref):
    s = _silu(c_ref[...]).astype(BF16)
    o_ref[0] = jnp.dot(s, w_ref[0].astype(BF16), preferred_element_type=F32) + b_ref[0]


def _modulation(cond, mod_w, mod_b):
    rows = cond.shape[0]
    d = D_MODEL
    return pl.pallas_call(
        _mod_kernel,
        out_shape=jax.ShapeDtypeStruct((DEPTH, rows, 3 * d), F32),
        grid=(DEPTH, 3),
        in_specs=[pl.BlockSpec((rows, d), lambda l, j: (0, 0)),
                  pl.BlockSpec((1, d, d), lambda l, j: (l, 0, j)),
                  pl.BlockSpec((1, 1, d), lambda l, j: (l, 0, j))],
        out_specs=pl.BlockSpec((1, rows, d), lambda l, j: (l, 0, j)),
        compiler_params=_params("arbitrary", "arbitrary"),
        name="modulation",
    )(cond, mod_w, mod_b.reshape(DEPTH, 1, 3 * d))


INPROJ_TM = 1024
INPROJ_SUB = 512
INPROJ_TN = 512


def _tile_plan(even):
    t = lambda col: col // INPROJ_TN
    if even:
        plan = [("silu", t(EV_Z) + i) for i in range(2)]
        plan += [("glu", t(EV_VAL) + i, t(EV_GT) + i) for i in range(2)]
        plan += [("silu", t(EV_GATE) + i) for i in range(2)]
        plan += [("id", t(EV_XS) + i) for i in range(3)]
    else:
        plan = [("gelu", t(OD_U) + i) for i in range(4)]
        plan += [("silu", t(OD_GC) + i) for i in range(2)]
        plan += [("id", t(OD_Q) + i) for i in range(2)]
        plan += [("silu", t(OD_GD) + i) for i in range(2)]
        plan += [("id", t(OD_K))]
    return plan


def _inproj_kernel(x_ref, sc_ref, sh_ref, w_ref, *rest, even, plan):
    if even:
        wdtt_ref, o_ref, dt_ref, dtt_ref, h_ref = rest
    else:
        o_ref, h_ref = rest
    tm = x_ref.shape[1]
    sub = min(INPROJ_SUB, tm)

    def modulate(s):
        rows = pl.ds(s * sub, sub)
        h_ref[rows, :] = (x_ref[0, rows, :] * (1.0 + sc_ref[0]) + sh_ref[0]).astype(BF16)

    modulate(0)
    for s in range(tm // sub):
        if (s + 1) * sub < tm:
            modulate(s + 1)
        rows = pl.ds(s * sub, sub)
        if even:
            dtt = lax.dot_general(wdtt_ref[...], h_ref[rows, :], (((1,), (1,)), ((), ())),
                                  preferred_element_type=F32)
            dt = dtt.T
            per = dt.shape[1] // SSD_GROUPS
            for gi in range(SSD_GROUPS):
                dt_ref[0, gi, rows, :] = dt[:, gi * per:(gi + 1) * per]
                dtt_ref[0, gi, :, rows] = dtt[gi * per:(gi + 1) * per, :]
        tile = lambda j: jnp.dot(h_ref[rows, :], w_ref[:, pl.ds(j * INPROJ_TN, INPROJ_TN)],
                                 preferred_element_type=F32)
        for out_j, (kind, *src) in enumerate(plan):
            r = tile(src[0])
            if kind == "silu":
                r = _silu(r)
            elif kind == "gelu":
                r = _gelu_tanh(r)
            elif kind == "glu":
                r = r * jax.nn.sigmoid(tile(src[1]))
            o_ref[0, rows, pl.ds(out_j * INPROJ_TN, INPROJ_TN)] = r.astype(BF16)


def _inproj(x, sc, sh, w, w_dt=None, plan=None):
    bsz, seq, d = x.shape
    n = w.shape[1]
    tm = min(seq, INPROJ_TM)
    even = w_dt is not None
    plan = tuple(_tile_plan(even)) if plan is None else plan
    n_out = len(plan) * INPROJ_TN
    resident = dict(pipeline_mode=pl.Buffered(1))
    in_specs = [pl.BlockSpec((1, tm, d), lambda b, i: (b, i, 0)),
                pl.BlockSpec((1, 1, d), lambda b, i: (b, 0, 0)),
                pl.BlockSpec((1, 1, d), lambda b, i: (b, 0, 0)),
                pl.BlockSpec((d, n), lambda b, i: (0, 0), **resident)]
    out_shape = [jax.ShapeDtypeStruct((bsz, seq, n_out), BF16)]
    out_specs = [pl.BlockSpec((1, tm, n_out), lambda b, i: (b, i, 0))]
    args = [x, sc, sh, w]
    if even:
        ndt = w_dt.shape[1]
        per = ndt // SSD_GROUPS
        in_specs += [pl.BlockSpec((ndt, d), lambda b, i: (0, 0))]
        out_shape += [jax.ShapeDtypeStruct((bsz, SSD_GROUPS, seq, per), F32),
                      jax.ShapeDtypeStruct((bsz, SSD_GROUPS, per, seq), F32)]
        out_specs += [pl.BlockSpec((1, SSD_GROUPS, tm, per), lambda b, i: (b, 0, i, 0)),
                      pl.BlockSpec((1, SSD_GROUPS, per, tm), lambda b, i: (b, 0, 0, i))]
        args += [w_dt.T]
    return pl.pallas_call(
        functools.partial(_inproj_kernel, even=even, plan=plan),
        out_shape=out_shape,
        grid=(bsz, seq // tm),
        in_specs=in_specs,
        out_specs=out_specs,
        scratch_shapes=[pltpu.VMEM((tm, d), BF16)],
        compiler_params=_params("arbitrary", "arbitrary"),
        name="inproj",
    )(*args)


def _inproj_ctx(ctx, sc, sh, w, w_dt=None, plan=None):
    bsz, clen, d = ctx.shape
    outs = _inproj(ctx.reshape(1, bsz * clen, d), sc[:1], sh[:1], w, w_dt, plan)
    proj = outs[0].reshape(bsz, clen, -1)
    if w_dt is None:
        return (proj,)
    per = outs[1].shape[-1]
    dt = outs[1].reshape(SSD_GROUPS, bsz, clen, per).transpose(1, 0, 2, 3)
    dtt = outs[2].reshape(SSD_GROUPS, per, bsz, clen).transpose(2, 0, 1, 3)
    return proj, dt, dtt


def _conv_silu(src_ref, w_ref, b_ref, pad_ref, dst_ref, seq):
    ch = src_ref.shape[-1]
    halo = SUBLANE
    pad_ref[pl.ds(0, halo), :] = jnp.zeros((halo, ch), F32)
    pad_ref[pl.ds(seq + halo, halo), :] = jnp.zeros((halo, ch), F32)
    pad_ref[pl.ds(halo, seq), :] = src_ref[0].astype(F32)
    w = w_ref[...]
    bias = b_ref[...]
    first = halo - SSD_CONV // 2

    def body(t, carry):
        base = pl.multiple_of(t * SSD_CHUNK, SSD_CHUNK)
        win = pad_ref[pl.ds(base, SSD_CHUNK + 2 * halo), :]
        acc = bias
        for k in range(SSD_CONV):
            acc = acc + w[k:k + 1, :] * win[first + k:first + k + SSD_CHUNK, :]
        dst_ref[pl.ds(base, SSD_CHUNK), :] = _silu(acc).astype(dst_ref.dtype)
        return carry

    lax.fori_loop(0, seq // SSD_CHUNK, body, 0)


def _pair_expand(vals, col0):
    q = vals.shape[0]
    lane = lax.broadcasted_iota(jnp.int32, (q, LANE), 1)
    parts = []
    for k in range(SSD_HPG // 2):
        a = jnp.broadcast_to(vals[:, col0 + 2 * k:col0 + 2 * k + 1], (q, LANE))
        b = jnp.broadcast_to(vals[:, col0 + 2 * k + 1:col0 + 2 * k + 2], (q, LANE))
        parts.append(jnp.where(lane < SSD_HEAD_DIM, a, b))
    return jnp.concatenate(parts, axis=-1)


def _split_dot(lhs, rhs):
    if lhs.dtype == F32:
        hi = lhs.astype(BF16)
        lo = (lhs - hi.astype(F32)).astype(BF16)
        return (jnp.dot(hi, rhs, preferred_element_type=F32)
                + jnp.dot(lo, rhs, preferred_element_type=F32))
    hi = rhs.astype(BF16)
    lo = (rhs - hi.astype(F32)).astype(BF16)
    return (jnp.dot(lhs, hi, preferred_element_type=F32)
            + jnp.dot(lhs, lo, preferred_element_type=F32))


def _ssd_kernel(xs_ref, bm_ref, cm_ref, dt_ref, dtt_ref,
                wx_ref, wb_ref, wc_ref, bx_ref, bb_ref, bc_ref,
                dtb_row_ref, dtb_col_ref, alog_row_ref, alog_col_ref, dskip_ref, *rest, seq, zero_init):
    h0_ref = None if zero_init else rest[0]
    (y_ref, hout_ref, padx_ref, padn_ref, xc_ref, bcs_ref, ccs_ref,
     yfwd_ref, ybwd_ref, state_ref) = rest[(0 if zero_init else 1):]
    q = SSD_CHUNK
    n_chunks = seq // q
    nh = 2 * SSD_HPG

    _conv_silu(xs_ref, wx_ref, bx_ref, padx_ref, xc_ref, seq)
    _conv_silu(bm_ref, wb_ref, bb_ref, padn_ref, bcs_ref, seq)
    _conv_silu(cm_ref, wc_ref, bc_ref, padn_ref, ccs_ref, seq)

    a_row = -jnp.exp(alog_row_ref[0])
    a_col = -jnp.exp(alog_col_ref[0])
    dtb_row = dtb_row_ref[0]
    dtb_col = dtb_col_ref[0]
    row_i = lax.broadcasted_iota(jnp.int32, (q, q), 0)
    col_i = lax.broadcasted_iota(jnp.int32, (q, q), 1)
    lower = row_i >= col_i
    upper = row_i <= col_i
    tri_lo = jnp.where(lower, 1.0, 0.0).astype(BF16)
    tri_up = jnp.where(upper, 1.0, 0.0).astype(BF16)
    lane = lax.broadcasted_iota(jnp.int32, (q, LANE), 1)
    head_lo = lane < SSD_HEAD_DIM

    def prologue(t, direction):
        rows = pl.ds(pl.multiple_of(t * q, q), q)
        dt = _softplus(dt_ref[0, 0, rows, :] + dtb_row)
        dtt = _softplus(dtt_ref[0, 0, :, rows] + dtb_col)
        da = dt * (a_row * LOG2E)
        dat = dtt * (a_col * LOG2E)
        if direction == 0:
            cum = _split_dot(tri_lo, da)
            cumt = _split_dot(dat, tri_up)
        else:
            cum = _split_dot(tri_up, da)
            cumt = _split_dot(dat, tri_lo)
        edge = q - 1 if direction == 0 else 0
        wt = dtt * jnp.exp2(cumt[:, edge:edge + 1] - cumt)
        bc = bcs_ref[rows, :]
        cc = ccs_ref[rows, :]
        cb = lax.dot_general(cc, bc, (((1,), (1,)), ((), ())), preferred_element_type=F32)
        bctf = bc.astype(F32).T
        return cum, cumt - jnp.log2(dtt), wt, cb, bctf

    def chunk(t, direction, pro):
        cum, rowt, wt, cb, bctf = pro
        rows = pl.ds(pl.multiple_of(t * q, q), q)
        mask = lower if direction == 0 else upper
        edge = q - 1 if direction == 0 else 0
        total = cum[edge:edge + 1, :]
        col0 = direction * SSD_HPG
        xb = xc_ref[rows, :]
        ccf = ccs_ref[rows, :].astype(F32)
        st = state_ref[direction]
        stb = st.astype(BF16)
        zero = jnp.zeros((q, LANE), BF16)
        ys, news = [], []
        for k in range(SSD_HPG // 2):
            xp = xb[:, k * LANE:(k + 1) * LANE]
            sp = stb[:, k * LANE:(k + 1) * LANE]
            rhs_x = jnp.concatenate([jnp.where(head_lo, xp, zero), jnp.where(head_lo, zero, xp)], axis=0)
            rhs_s = jnp.concatenate([jnp.where(head_lo, sp, zero), jnp.where(head_lo, zero, sp)], axis=0)
            ms, cs, ws = [], [], []
            for half in range(2):
                c = col0 + 2 * k + half
                bcol = jnp.broadcast_to(cum[:, c:c + 1], (q, q))
                seg = bcol - jnp.broadcast_to(rowt[c:c + 1, :], (q, q))
                ms.append((cb * jnp.exp2(jnp.where(mask, seg, -jnp.inf))).astype(BF16))
                cs.append((ccf * jnp.exp2(bcol)).astype(BF16))
                ws.append((bctf * jnp.broadcast_to(wt[c:c + 1, :], (q, q))).astype(BF16))
            ys.append(jnp.dot(jnp.concatenate(ms + cs, axis=1), jnp.concatenate([rhs_x, rhs_s], axis=0),
                              preferred_element_type=F32))
            news.append(jnp.dot(jnp.concatenate(ws, axis=1), rhs_x, preferred_element_type=F32))
        state_ref[direction] = st * _pair_expand(jnp.exp2(total), col0) + jnp.concatenate(news, axis=-1)
        return rows, jnp.concatenate(ys, axis=-1)

    state_ref[...] = jnp.zeros(state_ref.shape, F32) if zero_init else h0_ref[0, :, 0]

    def both(i, carry):
        pro_f, pro_b = carry
        nxt_f = prologue(jnp.minimum(i + 1, n_chunks - 1), 0)
        nxt_b = prologue(jnp.maximum(n_chunks - 2 - i, 0), 1)
        rows, y = chunk(i, 0, pro_f)
        yfwd_ref[rows, :] = y
        rows, y = chunk(n_chunks - 1 - i, 1, pro_b)
        ybwd_ref[rows, :] = y
        return nxt_f, nxt_b

    lax.fori_loop(0, n_chunks, both, (prologue(0, 0), prologue(n_chunks - 1, 1)))
    hout_ref[0, :, 0] = state_ref[...]

    dskip = dskip_ref[...]

    def finish(t, carry):
        rows = pl.ds(pl.multiple_of(t * q, q), q)
        y = yfwd_ref[rows, :] + ybwd_ref[rows, :] + xc_ref[rows, :].astype(F32) * dskip
        y_ref[0, rows, :] = y.astype(y_ref.dtype)
        return carry

    lax.fori_loop(0, n_chunks, finish, 0)


def _group_dt_order():
    order = []
    for g in range(SSD_GROUPS):
        for direction in range(2):
            order += [direction * SSD_HEADS + g * SSD_HPG + r for r in range(SSD_HPG)]
    return order


def _ssd(proj, dt, dtt, h0, conv_w, conv_b, dt_bias, a_log, d_skip):
    bsz, seq, _ = proj.shape
    gc = SSD_GROUP_CH
    ns = SSD_STATE
    nh = 2 * SSD_HPG
    xs_blk, b_blk, c_blk = EVO_XS // gc, EVO_B // ns, EVO_C // ns
    cw_b0, cw_c0 = SSD_INNER // ns, (SSD_INNER + SSD_GROUPS * ns) // ns
    conv_b2 = conv_b.reshape(1, -1)
    d_skip_x = jnp.repeat(d_skip, SSD_HEAD_DIM).reshape(1, SSD_INNER)
    order = jnp.array(_group_dt_order(), jnp.int32)
    dt_bias_g = dt_bias.reshape(-1)[order].reshape(SSD_GROUPS, nh)
    a_log_g = a_log.reshape(-1)[order].reshape(SSD_GROUPS, nh)
    in_specs = [
        pl.BlockSpec((1, seq, gc), lambda b, g: (b, 0, xs_blk + g)),
        pl.BlockSpec((1, seq, ns), lambda b, g: (b, 0, b_blk + g)),
        pl.BlockSpec((1, seq, ns), lambda b, g: (b, 0, c_blk + g)),
        pl.BlockSpec((1, 1, seq, nh), lambda b, g: (b, g, 0, 0)),
        pl.BlockSpec((1, 1, nh, seq), lambda b, g: (b, g, 0, 0)),
        pl.BlockSpec((SSD_CONV, gc), lambda b, g: (0, g)),
        pl.BlockSpec((SSD_CONV, ns), lambda b, g: (0, cw_b0 + g)),
        pl.BlockSpec((SSD_CONV, ns), lambda b, g: (0, cw_c0 + g)),
        pl.BlockSpec((1, gc), lambda b, g: (0, g)),
        pl.BlockSpec((1, ns), lambda b, g: (0, cw_b0 + g)),
        pl.BlockSpec((1, ns), lambda b, g: (0, cw_c0 + g)),
        pl.BlockSpec((1, 1, nh), lambda b, g: (g, 0, 0)),
        pl.BlockSpec((1, nh, 1), lambda b, g: (g, 0, 0)),
        pl.BlockSpec((1, 1, nh), lambda b, g: (g, 0, 0)),
        pl.BlockSpec((1, nh, 1), lambda b, g: (g, 0, 0)),
        pl.BlockSpec((1, gc), lambda b, g: (0, g)),
    ]
    state_spec = pl.BlockSpec((1, 2, 1, ns, gc), lambda b, g: (b, 0, g, 0, 0))
    args = [proj, proj, proj, dt, dtt,
            conv_w, conv_w, conv_w, conv_b2, conv_b2, conv_b2,
            dt_bias_g.reshape(SSD_GROUPS, 1, nh), dt_bias_g.reshape(SSD_GROUPS, nh, 1),
            a_log_g.reshape(SSD_GROUPS, 1, nh), a_log_g.reshape(SSD_GROUPS, nh, 1), d_skip_x]
    if h0 is not None:
        in_specs.append(state_spec)
        args.append(h0)
    return pl.pallas_call(
        functools.partial(_ssd_kernel, seq=seq, zero_init=h0 is None),
        out_shape=[jax.ShapeDtypeStruct((bsz, seq, SSD_INNER), BF16),
                   jax.ShapeDtypeStruct((bsz, 2, SSD_GROUPS, ns, gc), F32)],
        grid=(bsz, SSD_GROUPS),
        in_specs=in_specs,
        out_specs=[pl.BlockSpec((1, seq, gc), lambda b, g: (b, 0, g)), state_spec],
        scratch_shapes=[pltpu.VMEM((seq + 2 * SUBLANE, gc), F32),
                        pltpu.VMEM((seq + 2 * SUBLANE, ns), F32),
                        pltpu.VMEM((seq, gc), BF16),
                        pltpu.VMEM((seq, ns), BF16),
                        pltpu.VMEM((seq, ns), BF16),
                        pltpu.VMEM((seq, gc), F32),
                        pltpu.VMEM((seq, gc), F32),
                        pltpu.VMEM((2, ns, gc), F32)],
        compiler_params=_params("arbitrary", "arbitrary"),
        name="ssd",
    )(*args)


CV_ROWS = 256
CV_HALO = 16
CV_TC = 128
CV_SPAN = CV_ROWS + 2 * CV_HALO - SUBLANE


def _cvconv_kernel(glu_ref, w_ref, b_ref, o_ref, pad_ref, sh_ref, *, seq):
    ch = glu_ref.shape[-1]
    pad_ref[pl.ds(0, CV_HALO), :] = jnp.zeros((CV_HALO, ch), F32)
    pad_ref[pl.ds(seq + CV_HALO, CV_HALO), :] = jnp.zeros((CV_HALO, ch), F32)
    pad_ref[pl.ds(CV_HALO, seq), :] = glu_ref[0].astype(F32)
    w = w_ref[...]
    bias = b_ref[...]
    first = CV_HALO - CONV_WIDTH // 2
    rows = min(CV_ROWS * CV_TC // ch, seq)
    span = rows + 2 * CV_HALO - SUBLANE

    def body(t, carry):
        base = pl.multiple_of(t * rows, rows)
        win = pad_ref[pl.ds(base, rows + 2 * CV_HALO), :]
        for s in range(SUBLANE):
            sh_ref[s, pl.ds(0, span), :] = win[s:s + span, :]
        acc = jnp.broadcast_to(bias, (rows, ch))
        for k in range(CONV_WIDTH):
            a, s = divmod(first + k, SUBLANE)
            acc = acc + w[k:k + 1, :] * sh_ref[s, pl.ds(a * SUBLANE, rows), :]
        o_ref[0, pl.ds(base, rows), :] = acc.astype(o_ref.dtype)
        return carry

    lax.fori_loop(0, seq // rows, body, 0)


def _cvconv(proj, cv_w, cv_b):
    bsz, seq, _ = proj.shape
    tc = CV_TC * max(1, 1024 // seq)
    return pl.pallas_call(
        functools.partial(_cvconv_kernel, seq=seq),
        out_shape=jax.ShapeDtypeStruct((bsz, seq, CONV_CH), BF16),
        grid=(bsz, CONV_CH // tc),
        in_specs=[pl.BlockSpec((1, seq, tc), lambda b, j: (b, 0, EVO_GLU // tc + j)),
                  pl.BlockSpec((CONV_WIDTH, tc), lambda b, j: (0, j)),
                  pl.BlockSpec((1, tc), lambda b, j: (0, j))],
        out_specs=pl.BlockSpec((1, seq, tc), lambda b, j: (b, 0, j)),
        scratch_shapes=[pltpu.VMEM((seq + 2 * CV_HALO, tc), F32),
                        pltpu.VMEM((SUBLANE, CV_SPAN, tc), F32)],
        compiler_params=_params("arbitrary", "arbitrary"),
        name="cvconv",
    )(proj, cv_w, cv_b.reshape(1, CONV_CH))


def _gmlp_rows(u_ref, v_ref, gc_ref, lng_ref, lnb_ref, ws_ref, bias_ref, row0, n_rows):
    lng = lng_ref[...]
    lnb = lnb_ref[...]
    bias = bias_ref[...]
    out = []
    for c in range(n_rows // MLP_CHUNK):
        rows = pl.ds(row0 + c * MLP_CHUNK, MLP_CHUNK)
        vn = _layer_norm(v_ref[0, rows, :].astype(F32), lng, lnb).astype(BF16)
        mixed = jnp.concatenate(
            [jnp.dot(ws_ref[gi], vn[:, gi * MLP_GROUP_CH:(gi + 1) * MLP_GROUP_CH], preferred_element_type=F32)
             for gi in range(MLP_GROUPS)], axis=-1)
        u = u_ref[0, rows, :].astype(F32)
        out.append((u * (mixed + bias) * gc_ref[0, rows, :].astype(F32)).astype(BF16))
    return jnp.concatenate(out, axis=0)


def _rope(t, cos, sin_signed):
    width = t.shape[1]
    reps = width // LANE
    half = ATT_HEAD_DIM // 2
    lane = lax.broadcasted_iota(jnp.int32, t.shape, 1)
    first_half = (lane % ATT_HEAD_DIM) < half
    swapped = jnp.where(first_half, pltpu.roll(t, width - half, 1), pltpu.roll(t, half, 1))
    c = jnp.concatenate([cos] * reps, axis=-1)
    s = jnp.concatenate([sin_signed] * reps, axis=-1)
    return t * c + swapped * s


def _att_head_order():
    order = []
    for c in range(ATT_HEADS // 2):
        j, r = divmod(c, ATT_REP)
        order += [(2 * j) * ATT_REP + r, (2 * j + 1) * ATT_REP + r]
    return order


def _att_reorder(t, axis):
    shape = t.shape
    kvp = ATT_KV_HEADS // 2
    t = jnp.moveaxis(t, axis, 0).reshape((kvp, 2, ATT_REP, ATT_HEAD_DIM) + shape[:axis] + shape[axis + 1:])
    t = jnp.swapaxes(t, 1, 2).reshape((shape[axis],) + shape[:axis] + shape[axis + 1:])
    return jnp.moveaxis(t, 0, axis)


def _attn_kernel(*refs, seq, ctx_len, latent):
    if latent:
        (q_ref, gd_ref, k_ref, v_ref, kc_ref, vc_ref, sink_ref, cos_ref, sin_ref,
         o_ref, kpad_ref, vtp_ref, vct_ref) = refs
    else:
        q_ref, gd_ref, kc_ref, vc_ref, sink_ref, o_ref, vct_ref = refs
    i = pl.program_id(1)
    w = ATT_BLOCK
    nt = (((1,), (1,)), ((), ()))

    vrows = LANE + ATT_ONES

    @pl.when(i == 0)
    def _():
        vct = vc_ref[0].astype(F32).T.astype(BF16)
        for j in range(ATT_KV_CH // LANE):
            vct_ref[pl.ds(j * vrows, LANE), :] = vct[j * LANE:(j + 1) * LANE, :]
            vct_ref[pl.ds(j * vrows + LANE, ATT_ONES), :] = jnp.ones((ATT_ONES, ctx_len), BF16)
        if latent:
            kpad_ref[pl.ds(0, w), :] = jnp.zeros((w, ATT_KV_CH), BF16)
            kpad_ref[pl.ds(seq + w, w), :] = jnp.zeros((w, ATT_KV_CH), BF16)
            kpad_ref[pl.ds(w, seq), :] = _rope(k_ref[0].astype(F32), cos_ref[...], sin_ref[...]).astype(BF16)
            for j in range(ATT_KV_CH // LANE):
                vtp_ref[pl.ds(j * vrows, LANE), pl.ds(0, w)] = jnp.zeros((LANE, w), BF16)
                vtp_ref[pl.ds(j * vrows, LANE), pl.ds(seq + w, w)] = jnp.zeros((LANE, w), BF16)
                vtp_ref[pl.ds(j * vrows + LANE, ATT_ONES), :] = jnp.ones((ATT_ONES, seq + 2 * w), BF16)
            for t in range(seq // w):
                vt = v_ref[0, pl.ds(t * w, w), :].astype(F32).T.astype(BF16)
                for j in range(ATT_KV_CH // LANE):
                    vtp_ref[pl.ds(j * vrows, LANE), pl.ds((t + 1) * w, w)] = vt[j * LANE:(j + 1) * LANE, :]

    if latent:
        base = pl.multiple_of(i * w, w)
        q = _rope(q_ref[0].astype(F32), cos_ref[pl.ds(base, w), :], sin_ref[pl.ds(base, w), :])
        kwin = kpad_ref[pl.ds(base, 3 * w), :]
        qi = lax.broadcasted_iota(jnp.int32, (w, w), 1)
        u = lax.broadcasted_iota(jnp.int32, (w, w), 0)
        before = (u - w >= qi - ATT_WINDOW) & (base + u - w >= 0)
        after = (u + w <= qi + ATT_WINDOW) & (base + u + w < seq)
        bias_lo = jnp.where(before, 0.0, -jnp.inf)
        bias_hi = jnp.where(after, 0.0, -jnp.inf)
        bias_lo = jnp.concatenate([bias_lo, bias_lo], axis=1)
        bias_hi = jnp.concatenate([bias_hi, bias_hi], axis=1)
    else:
        q = q_ref[0].astype(F32)
    qs = (q * (ATT_SCALE * LOG2E)).astype(BF16)
    kc = kc_ref[0]
    lane = lax.broadcasted_iota(jnp.int32, (w, LANE), 1)
    low_lanes = lane < ATT_HEAD_DIM
    low_rows = lax.broadcasted_iota(jnp.int32, (LANE, w), 0) < ATT_HEAD_DIM
    zero = jnp.zeros((w, LANE), BF16)
    n_cols = ATT_HEADS // 2

    def scores(c):
        kv = slice((c // ATT_REP) * LANE, (c // ATT_REP + 1) * LANE)
        qc = qs[:, c * LANE:(c + 1) * LANE]
        rhs = jnp.concatenate([jnp.where(low_lanes, qc, zero), jnp.where(low_lanes, zero, qc)], axis=0)
        s_ctx = lax.dot_general(kc[:, kv], rhs, nt, preferred_element_type=F32)
        s_lat = lax.dot_general(kwin[:, kv], rhs, nt, preferred_element_type=F32) if latent else None
        return s_ctx, s_lat

    def finish(c, s_ctx, s_lat):
        j = c // ATT_REP
        snk = sink_ref[0, :, pl.ds(c * 2 * w, 2 * w)]
        m = jnp.maximum(jnp.max(s_ctx, axis=0, keepdims=True), snk)
        if latent:
            s_lo = s_lat[:w] + bias_lo
            s_mid = s_lat[w:2 * w]
            s_hi = s_lat[2 * w:] + bias_hi
            m = jnp.maximum(m, jnp.maximum(jnp.maximum(jnp.max(s_lo, axis=0, keepdims=True),
                                                       jnp.max(s_mid, axis=0, keepdims=True)),
                                           jnp.max(s_hi, axis=0, keepdims=True)))
        probs = [jnp.exp2(s_ctx - m)]
        vals = [vct_ref[pl.ds(j * vrows, vrows), :]]
        if latent:
            probs += [jnp.exp2(s_lo - m), jnp.exp2(s_mid - m), jnp.exp2(s_hi - m)]
            vals.append(vtp_ref[pl.ds(j * vrows, vrows), pl.ds(base, 3 * w)])
        acc = jnp.dot(jnp.concatenate(vals, axis=1), jnp.concatenate(probs, axis=0).astype(BF16),
                      preferred_element_type=F32)
        den = acc[LANE:LANE + 1, :] + jnp.exp2(snk - m)
        acc = acc[:LANE, :] * (1.0 / den)
        return jnp.where(low_rows, acc[:, :w], acc[:, w:]).T

    outs = []
    ahead = 3
    pending = [scores(c) for c in range(ahead)]
    for c in range(n_cols):
        if c + ahead < n_cols:
            pending.append(scores(c + ahead))
        outs.append(finish(c, *pending.pop(0)))
    gd = gd_ref[0].astype(F32)
    o_ref[0] = (jnp.concatenate(outs, axis=-1) * gd).astype(o_ref.dtype)


def _attention(proj, proj_ctx, sink, cos, sin_signed, latent, ctx_col=OD_K):
    bsz, seq, _ = proj.shape
    ctx_len = proj_ctx.shape[1]
    w = ATT_BLOCK
    qch = ATT_HEADS * ATT_HEAD_DIM
    kvc = ATT_KV_CH
    q_spec = pl.BlockSpec((1, w, qch), lambda b, i: (b, i, OD_Q // qch))
    gd_spec = pl.BlockSpec((1, w, qch), lambda b, i: (b, i, OD_GD // qch))
    kc_spec = pl.BlockSpec((1, ctx_len, kvc), lambda b, i: (b, 0, ctx_col // kvc))
    vc_spec = pl.BlockSpec((1, ctx_len, kvc), lambda b, i: (b, 0, ctx_col // kvc + 1))
    sink2 = jnp.repeat(sink[jnp.array(_att_head_order(), jnp.int32)] * LOG2E, w).reshape(1, 1, ATT_HEADS * w)
    sink_spec = pl.BlockSpec((1, 1, ATT_HEADS * w), lambda b, i: (0, 0, 0))
    vrows = (kvc // LANE) * (LANE + ATT_ONES)
    vct = pltpu.VMEM((vrows, ctx_len), BF16)
    if latent:
        in_specs = [q_spec, gd_spec,
                    pl.BlockSpec((1, seq, kvc), lambda b, i: (b, 0, OD_K // kvc)),
                    pl.BlockSpec((1, seq, kvc), lambda b, i: (b, 0, OD_VA // kvc)),
                    kc_spec, vc_spec, sink_spec,
                    pl.BlockSpec((seq, LANE), lambda b, i: (0, 0)),
                    pl.BlockSpec((seq, LANE), lambda b, i: (0, 0))]
        args = (proj, proj, proj, proj, proj_ctx, proj_ctx, sink2, cos, sin_signed)
        scratch = [pltpu.VMEM((seq + 2 * w, kvc), BF16), pltpu.VMEM((vrows, seq + 2 * w), BF16), vct]
    else:
        in_specs = [q_spec, gd_spec, kc_spec, vc_spec, sink_spec]
        args = (proj, proj, proj_ctx, proj_ctx, sink2)
        scratch = [vct]
    return pl.pallas_call(
        functools.partial(_attn_kernel, seq=seq, ctx_len=ctx_len, latent=latent),
        out_shape=jax.ShapeDtypeStruct((bsz, seq, qch), BF16),
        grid=(bsz, seq // w),
        in_specs=in_specs,
        out_specs=pl.BlockSpec((1, w, qch), lambda b, i: (b, i, 0)),
        scratch_shapes=scratch,
        compiler_params=_params("arbitrary", "arbitrary"),
        name="attention" if latent else "ctx_attention",
    )(*args)


OUTPROJ_TM = 1024
OUTPROJ_SUB = 512


def _outproj_kernel(*refs, even):
    if even:
        (y_ref, z_ref, cv_ref, gate_ref, nw_ref, cvg_ref, cvb_ref,
         x_ref, g_ref, lng_ref, lnb_ref, w_ref, o_ref) = refs
    else:
        (u_ref, v_ref, gc_ref, mlng_ref, mlnb_ref, ws_ref, bias_ref, yb_ref,
         x_ref, g_ref, lng_ref, lnb_ref, w_ref, o_ref) = refs
    half = w_ref.shape[0] // 2
    tm = x_ref.shape[1]
    sub = min(OUTPROJ_SUB, tm)

    def project(k):
        rows = pl.ds(k * sub, sub)
        if even:
            t = y_ref[0, rows, :].astype(F32) * z_ref[0, rows, :].astype(F32)
            ya = (t * lax.rsqrt(jnp.mean(t * t, -1, keepdims=True) + LN_EPS) * nw_ref[...]).astype(BF16)
            yb = (_silu(_layer_norm(cv_ref[0, rows, :].astype(F32), cvg_ref[...], cvb_ref[...]))
                  * gate_ref[0, rows, :].astype(F32)).astype(BF16)
        else:
            ya = _gmlp_rows(u_ref, v_ref, gc_ref, mlng_ref, mlnb_ref, ws_ref, bias_ref, k * sub, sub)
            yb = yb_ref[0, rows, :]
        return (jnp.dot(ya, w_ref[pl.ds(0, half), :], preferred_element_type=F32)
                + jnp.dot(yb, w_ref[pl.ds(half, half), :], preferred_element_type=F32))

    n_sub = tm // sub
    y = project(0)
    for k in range(n_sub):
        nxt = project(k + 1) if k + 1 < n_sub else None
        rows = pl.ds(k * sub, sub)
        r = DEEPNORM_ALPHA * x_ref[0, rows, :] + g_ref[0] * y
        o_ref[0, rows, :] = _layer_norm(r, lng_ref[...], lnb_ref[...])
        y = nxt


def _outproj(mix_args, mix_specs, x, g, ln_g, ln_b, w_out, even, tm):
    bsz, seq, d = x.shape
    vec = pl.BlockSpec((1, d), lambda b, i: (0, 0))
    in_specs = list(mix_specs) + [
        pl.BlockSpec((1, tm, d), lambda b, i: (b, i, 0)),
        pl.BlockSpec((1, 1, d), lambda b, i: (b, 0, 0)),
        vec, vec,
        pl.BlockSpec(w_out.shape, lambda b, i: (0, 0))]
    return pl.pallas_call(
        functools.partial(_outproj_kernel, even=even),
        out_shape=jax.ShapeDtypeStruct((bsz, seq, d), F32),
        grid=(bsz, seq // tm),
        in_specs=in_specs,
        out_specs=pl.BlockSpec((1, tm, d), lambda b, i: (b, i, 0)),
        compiler_params=_params("arbitrary", "arbitrary"),
        name="outproj_even" if even else "outproj_odd",
    )(*mix_args, x, g, ln_g.reshape(1, d), ln_b.reshape(1, d), w_out)


def _outproj_even(y_ssd, cv, proj, norm_w, cv_ln_g, cv_ln_b, x, g, ln_g, ln_b, w_out):
    seq = x.shape[1]
    tm = min(seq, OUTPROJ_TM)
    ch = SSD_INNER
    blk = lambda col: pl.BlockSpec((1, tm, ch), lambda b, i: (b, i, col // ch))
    vec = pl.BlockSpec((1, ch), lambda b, i: (0, 0))
    specs = [blk(0), blk(EVO_Z), blk(0), blk(EVO_GATE), vec, vec, vec]
    args = (y_ssd, proj, cv, proj, norm_w.reshape(1, ch), cv_ln_g.reshape(1, ch), cv_ln_b.reshape(1, ch))
    return _outproj(args, specs, x, g, ln_g, ln_b, w_out, True, tm)


def _outproj_odd(proj, yd, mlp_ln_g, mlp_ln_b, ws, bs, x, g, ln_g, ln_b, w_out):
    seq = x.shape[1]
    tm = min(seq, OUTPROJ_TM)
    ch = MLP_CH
    blk = lambda col: pl.BlockSpec((1, tm, ch), lambda b, i: (b, i, col // ch))
    vec = pl.BlockSpec((1, ch), lambda b, i: (0, 0))
    bias = jnp.repeat(bs.T, MLP_GROUP_CH, axis=1)
    specs = [blk(OD_U), blk(OD_V), blk(OD_GC), vec, vec,
             pl.BlockSpec((MLP_GROUPS, MLP_CHUNK, MLP_CHUNK), lambda b, i: (0, 0, 0)),
             pl.BlockSpec((MLP_CHUNK, ch), lambda b, i: (0, 0)), blk(0)]
    args = (proj, proj, proj, mlp_ln_g.reshape(1, ch), mlp_ln_b.reshape(1, ch), ws.astype(BF16), bias, yd)
    return _outproj(args, specs, x, g, ln_g, ln_b, w_out, False, tm)


def _rope_tables(seq):
    t = jnp.arange(seq)
    row = (t // GRID_W).astype(F32)
    col = (t % GRID_W).astype(F32)
    n_freq = ATT_HEAD_DIM // 4
    inv = ROPE_BASE ** (-jnp.arange(n_freq, dtype=F32) / n_freq)
    ang = jnp.concatenate([row[:, None] * inv, col[:, None] * inv], -1)
    cos, sin = jnp.cos(ang), jnp.sin(ang)
    reps = LANE // ATT_HEAD_DIM
    return (jnp.tile(jnp.concatenate([cos, cos], -1), (1, reps)),
            jnp.tile(jnp.concatenate([-sin, sin], -1), (1, reps)))


def _even_weights(w_in):
    o_z, o_xbc, o_dt, o_glu, o_gate = _offsets(SSD_INNER, SSD_INNER + 2 * SSD_BC, 2 * SSD_HEADS, 2 * CONV_CH, CONV_CH)
    main = jnp.concatenate([w_in[:, o_z:o_xbc], w_in[:, o_glu:o_gate], w_in[:, o_gate:],
                            w_in[:, o_xbc:o_dt]], axis=1)
    w_dt = w_in[:, o_dt:o_glu][:, jnp.array(_group_dt_order(), jnp.int32)]
    return main.astype(BF16), w_dt.astype(BF16)


def _odd_weights(w_in):
    o_q = 3 * MLP_CH
    o_k = o_q + ATT_HEADS * ATT_HEAD_DIM
    o_gd = o_k + 2 * ATT_KV_CH
    return jnp.concatenate([w_in[:, :o_q], _att_reorder(w_in[:, o_q:o_k], 1), _att_reorder(w_in[:, o_gd:], 1),
                            w_in[:, o_k:o_gd]], axis=1).astype(BF16)


def _odd_out_weights(w_out):
    return jnp.concatenate([w_out[:MLP_CH], _att_reorder(w_out[MLP_CH:], 0)], axis=0).astype(BF16)


def kernel(x, c, ctx, c_ctx, mod_w, mod_b, ln_g, ln_b, ev_w_in, ev_ssd_conv_w, ev_ssd_conv_b, ev_dt_bias, ev_a_log, ev_d_skip, ev_ssd_norm, ev_cv_w, ev_cv_b, ev_cv_ln_g, ev_cv_ln_b, ev_w_out, od_w_in, od_mlp_ln_g, od_mlp_ln_b, od_ws, od_bs, od_sink, od_w_out):
    bsz, seq, d = x.shape
    ctx_len = ctx.shape[1]
    assert d == D_MODEL and mod_w.shape[0] == DEPTH
    for n in (seq, ctx_len):
        assert n % SSD_CHUNK == 0 and n % ATT_BLOCK == 0 and n % MLP_CHUNK == 0
        assert n % min(n, OUTPROJ_TM) == 0 and min(n, OUTPROJ_TM) % min(n, OUTPROJ_SUB) == 0
        assert n % min(n, CV_ROWS) == 0
    for n in (seq, bsz * ctx_len):
        assert n % min(n, INPROJ_TM) == 0 and min(n, INPROJ_TM) % min(n, INPROJ_SUB) == 0
    cos, sin_signed = _rope_tables(seq)
    rows = -(-(bsz + 1) // SUBLANE) * SUBLANE
    cond = jnp.concatenate([c, c_ctx[None, :], jnp.zeros((rows - bsz - 1, d), F32)], axis=0)
    mod = _modulation(cond, mod_w, mod_b)

    for layer in range(DEPTH):
        last = layer == DEPTH - 1
        i = layer // 2
        m = mod[layer]
        sh_x, sc_x, g_x = (m[:bsz, None, k * d:(k + 1) * d] for k in range(3))
        sh_c, sc_c, g_c = (jnp.broadcast_to(m[bsz:bsz + 1, None, k * d:(k + 1) * d], (bsz, 1, d)) for k in range(3))
        if layer % 2 == 0:
            w_main, w_dt = _even_weights(ev_w_in[i])
            w_out = ev_w_out[i].astype(BF16)
            ssd_args = (ev_ssd_conv_w[i], ev_ssd_conv_b[i], ev_dt_bias[i], ev_a_log[i], ev_d_skip[i])
            p_c, dt_c, dtt_c = _inproj_ctx(ctx, sc_c, sh_c, w_main, w_dt)
            y_c, h_c = _ssd(p_c, dt_c, dtt_c, None, *ssd_args)
            p_x, dt_x, dtt_x = _inproj(x, sc_x, sh_x, w_main, w_dt)
            y_x, _ = _ssd(p_x, dt_x, dtt_x, h_c, *ssd_args)
            cv_x = _cvconv(p_x, ev_cv_w[i], ev_cv_b[i])
            x = _outproj_even(y_x, cv_x, p_x, ev_ssd_norm[i], ev_cv_ln_g[i], ev_cv_ln_b[i],
                              x, g_x, ln_g[layer], ln_b[layer], w_out)
            if not last:
                cv_c = _cvconv(p_c, ev_cv_w[i], ev_cv_b[i])
                ctx = _outproj_even(y_c, cv_c, p_c, ev_ssd_norm[i], ev_cv_ln_g[i], ev_cv_ln_b[i],
                                    ctx, g_c, ln_g[layer], ln_b[layer], w_out)
        else:
            w_main = _odd_weights(od_w_in[i])
            w_out = _odd_out_weights(od_w_out[i])
            if last:
                p_c = _inproj_ctx(ctx, sc_c, sh_c, w_main[:, OD_K:OD_K + INPROJ_TN], plan=(("id", 0),))[0]
                ctx_col = 0
            else:
                p_c = _inproj_ctx(ctx, sc_c, sh_c, w_main)[0]
                ctx_col = OD_K
            p_x = _inproj(x, sc_x, sh_x, w_main)[0]
            yd_x = _attention(p_x, p_c, od_sink[i], cos, sin_signed, True, ctx_col)
            mlp_args = (od_mlp_ln_g[i], od_mlp_ln_b[i], od_ws[i], od_bs[i])
            x = _outproj_odd(p_x, yd_x, *mlp_args, x, g_x, ln_g[layer], ln_b[layer], w_out)
            if not last:
                yd_c = _attention(p_c, p_c, od_sink[i], cos, sin_signed, False)
                ctx = _outproj_odd(p_c, yd_c, *mlp_args, ctx, g_c, ln_g[layer], ln_b[layer], w_out)
    return x
```

```python
import functools
import math

import jax
import jax.numpy as jnp
from jax import lax
from jax.experimental import pallas as pl
from jax.experimental.pallas import tpu as pltpu

F32 = jnp.float32
BF16 = jnp.bfloat16

D_MODEL = 1024
DEPTH = 4
GRID_W = 64

SSD_HEADS = 16
SSD_HEAD_DIM = 64
SSD_INNER = SSD_HEADS * SSD_HEAD_DIM
SSD_GROUPS = 2
SSD_HPG = SSD_HEADS // SSD_GROUPS
SSD_STATE = 128
SSD_CHUNK = 128
SSD_CONV = 5
SSD_GROUP_CH = SSD_HPG * SSD_HEAD_DIM
CONV_CH = 1024
CONV_WIDTH = 31
MLP_CH = 1024
MLP_GROUPS = 8
MLP_GROUP_CH = MLP_CH // MLP_GROUPS
MLP_CHUNK = 128
ATT_HEADS = 16
ATT_KV_HEADS = 4
ATT_REP = ATT_HEADS // ATT_KV_HEADS
ATT_HEAD_DIM = 64
ATT_WINDOW = 128
ATT_BLOCK = 128
ATT_SCALE = ATT_HEAD_DIM ** -0.5
LOG2E = math.log2(math.e)
ROPE_BASE = 10000.0
ATT_KV_CH = ATT_KV_HEADS * ATT_HEAD_DIM
ATT_ONES = 16

DEEPNORM_ALPHA = (2 * DEPTH) ** 0.25
LN_EPS = 1e-5

LANE = 128
SUBLANE = 8
VMEM_LIMIT = 56 * 1024 * 1024


def _offsets(*widths):
    return tuple(sum(widths[:i]) for i in range(len(widths)))


SSD_BC = SSD_GROUPS * SSD_STATE
EV_Z, EV_VAL, EV_GT, EV_GATE, EV_XS, EV_B, EV_C = _offsets(SSD_INNER, CONV_CH, CONV_CH, CONV_CH, SSD_INNER, SSD_BC, SSD_BC)
EVO_Z, EVO_GLU, EVO_GATE, EVO_XS, EVO_B, EVO_C = _offsets(SSD_INNER, CONV_CH, CONV_CH, SSD_INNER, SSD_BC, SSD_BC)
OD_U, OD_V, OD_GC, OD_Q, OD_GD, OD_K, OD_VA = _offsets(MLP_CH, MLP_CH, MLP_CH, ATT_HEADS * ATT_HEAD_DIM,
                                                       ATT_HEADS * ATT_HEAD_DIM, ATT_KV_CH, ATT_KV_CH)


def _silu(t):
    return t * jax.nn.sigmoid(t)


def _gelu_tanh(t):
    c = -2.0 * math.sqrt(2.0 / math.pi) * LOG2E
    return t / (1.0 + jnp.exp2(t * (c + (c * 0.044715) * (t * t))))


def _softplus(t):
    return jnp.maximum(t, 0.0) + jnp.log1p(jnp.exp(-jnp.abs(t)))


def _layer_norm(t, g, b):
    mu = jnp.mean(t, -1, keepdims=True)
    d = t - mu
    var = jnp.mean(d * d, -1, keepdims=True)
    return d * lax.rsqrt(var + LN_EPS) * g + b


def _params(*sem):
    return pltpu.CompilerParams(dimension_semantics=sem, vmem_limit_bytes=VMEM_LIMIT)


def _mod_kernel(c_ref, w_ref, b_ref, o_ref):
    s = _silu(c_ref[...]).astype(BF16)
    o_ref[0] = jnp.dot(s, w_ref[0].astype(BF16), preferred_element_type=F32) + b_ref[0]


def _modulation(cond, mod_w, mod_b):
    rows = cond.shape[0]
    d = D_MODEL
    return pl.pallas_call(
        _mod_kernel,
        out_shape=jax.ShapeDtypeStruct((DEPTH, rows, 3 * d), F32),
        grid=(DEPTH, 3),
        in_specs=[pl.BlockSpec((rows, d), lambda l, j: (0, 0)),
                  pl.BlockSpec((1, d, d), lambda l, j: (l, 0, j)),
                  pl.BlockSpec((1, 1, d), lambda l, j: (l, 0, j))],
        out_specs=pl.BlockSpec((1, rows, d), lambda l, j: (l, 0, j)),
        compiler_params=_params("arbitrary", "arbitrary"),
        name="modulation",
    )(cond, mod_w, mod_b.reshape(DEPTH, 1, 3 * d))


INPROJ_TM = 1024
INPROJ_SUB = 512
INPROJ_TN = 512


def _tile_plan(even):
    t = lambda col: col // INPROJ_TN
    if even:
        plan = [("silu", t(EV_Z) + i) for i in range(2)]
        plan += [("glu", t(EV_VAL) + i, t(EV_GT) + i) for i in range(2)]
        plan += [("silu", t(EV_GATE) + i) for i in range(2)]
        plan += [("id", t(EV_XS) + i) for i in range(3)]
    else:
        plan = [("gelu", t(OD_U) + i) for i in range(4)]
        plan += [("silu", t(OD_GC) + i) for i in range(2)]
        plan += [("id", t(OD_Q) + i) for i in range(2)]
        plan += [("silu", t(OD_GD) + i) for i in range(2)]
        plan += [("id", t(OD_K))]
    return plan


def _inproj_kernel(x_ref, sc_ref, sh_ref, w_ref, *rest, even, plan):
    if even:
        wdtt_ref, o_ref, dt_ref, dtt_ref, h_ref = rest
    else:
        o_ref, h_ref = rest
    tm = x_ref.shape[1]
    sub = min(INPROJ_SUB, tm)

    def modulate(s):
        rows = pl.ds(s * sub, sub)
        h_ref[rows, :] = (x_ref[0, rows, :] * (1.0 + sc_ref[0]) + sh_ref[0]).astype(BF16)

    modulate(0)
    for s in range(tm // sub):
        if (s + 1) * sub < tm:
            modulate(s + 1)
        rows = pl.ds(s * sub, sub)
        if even:
            dtt = lax.dot_general(wdtt_ref[...], h_ref[rows, :], (((1,), (1,)), ((), ())),
                                  preferred_element_type=F32)
            dt = dtt.T
            per = dt.shape[1] // SSD_GROUPS
            for gi in range(SSD_GROUPS):
                dt_ref[0, gi, rows, :] = dt[:, gi * per:(gi + 1) * per]
                dtt_ref[0, gi, :, rows] = dtt[gi * per:(gi + 1) * per, :]
        tile = lambda j: jnp.dot(h_ref[rows, :], w_ref[:, pl.ds(j * INPROJ_TN, INPROJ_TN)],
                                 preferred_element_type=F32)
        for out_j, (kind, *src) in enumerate(plan):
            r = tile(src[0])
            if kind == "silu":
                r = _silu(r)
            elif kind == "gelu":
                r = _gelu_tanh(r)
            elif kind == "glu":
                r = r * jax.nn.sigmoid(tile(src[1]))
            o_ref[0, rows, pl.ds(out_j * INPROJ_TN, INPROJ_TN)] = r.astype(BF16)


def _inproj(x, sc, sh, w, w_dt=None, plan=None):
    bsz, seq, d = x.shape
    n = w.shape[1]
    tm = min(seq, INPROJ_TM)
    even = w_dt is not None
    plan = tuple(_tile_plan(even)) if plan is None else plan
    n_out = len(plan) * INPROJ_TN
    resident = dict(pipeline_mode=pl.Buffered(1))
    in_specs = [pl.BlockSpec((1, tm, d), lambda b, i: (b, i, 0)),
                pl.BlockSpec((1, 1, d), lambda b, i: (b, 0, 0)),
                pl.BlockSpec((1, 1, d), lambda b, i: (b, 0, 0)),
                pl.BlockSpec((d, n), lambda b, i: (0, 0), **resident)]
    out_shape = [jax.ShapeDtypeStruct((bsz, seq, n_out), BF16)]
    out_specs = [pl.BlockSpec((1, tm, n_out), lambda b, i: (b, i, 0))]
    args = [x, sc, sh, w]
    if even:
        ndt = w_dt.shape[1]
        per = ndt // SSD_GROUPS
        in_specs += [pl.BlockSpec((ndt, d), lambda b, i: (0, 0))]
        out_shape += [jax.ShapeDtypeStruct((bsz, SSD_GROUPS, seq, per), F32),
                      jax.ShapeDtypeStruct((bsz, SSD_GROUPS, per, seq), F32)]
        out_specs += [pl.BlockSpec((1, SSD_GROUPS, tm, per), lambda b, i: (b, 0, i, 0)),
                      pl.BlockSpec((1, SSD_GROUPS, per, tm), lambda b, i: (b, 0, 0, i))]
        args += [w_dt.T]
    return pl.pallas_call(
        functools.partial(_inproj_kernel, even=even, plan=plan),
        out_shape=out_shape,
        grid=(bsz, seq // tm),
        in_specs=in_specs,
        out_specs=out_specs,
        scratch_shapes=[pltpu.VMEM((tm, d), BF16)],
        compiler_params=_params("arbitrary", "arbitrary"),
        name="inproj",
    )(*args)


def _inproj_ctx(ctx, sc, sh, w, w_dt=None, plan=None):
    bsz, clen, d = ctx.shape
    outs = _inproj(ctx.reshape(1, bsz * clen, d), sc[:1], sh[:1], w, w_dt, plan)
    proj = outs[0].reshape(bsz, clen, -1)
    if w_dt is None:
        return (proj,)
    per = outs[1].shape[-1]
    dt = outs[1].reshape(SSD_GROUPS, bsz, clen, per).transpose(1, 0, 2, 3)
    dtt = outs[2].reshape(SSD_GROUPS, per, bsz, clen).transpose(2, 0, 1, 3)
    return proj, dt, dtt


def _conv_silu(src_ref, w_ref, b_ref, pad_ref, dst_ref, seq):
    ch = src_ref.shape[-1]
    halo = SUBLANE
    pad_ref[pl.ds(0, halo), :] = jnp.zeros((halo, ch), F32)
    pad_ref[pl.ds(seq + halo, halo), :] = jnp.zeros((halo, ch), F32)
    pad_ref[pl.ds(halo, seq), :] = src_ref[0].astype(F32)
    w = w_ref[...]
    bias = b_ref[...]
    first = halo - SSD_CONV // 2

    def body(t, carry):
        base = pl.multiple_of(t * SSD_CHUNK, SSD_CHUNK)
        win = pad_ref[pl.ds(base, SSD_CHUNK + 2 * halo), :]
        acc = bias
        for k in range(SSD_CONV):
            acc = acc + w[k:k + 1, :] * win[first + k:first + k + SSD_CHUNK, :]
        dst_ref[pl.ds(base, SSD_CHUNK), :] = _silu(acc).astype(dst_ref.dtype)
        return carry

    lax.fori_loop(0, seq // SSD_CHUNK, body, 0)


def _pair_expand(vals, col0):
    q = vals.shape[0]
    lane = lax.broadcasted_iota(jnp.int32, (q, LANE), 1)
    parts = []
    for k in range(SSD_HPG // 2):
        a = jnp.broadcast_to(vals[:, col0 + 2 * k:col0 + 2 * k + 1], (q, LANE))
        b = jnp.broadcast_to(vals[:, col0 + 2 * k + 1:col0 + 2 * k + 2], (q, LANE))
        parts.append(jnp.where(lane < SSD_HEAD_DIM, a, b))
    return jnp.concatenate(parts, axis=-1)


def _split_dot(lhs, rhs):
    if lhs.dtype == F32:
        hi = lhs.astype(BF16)
        lo = (lhs - hi.astype(F32)).astype(BF16)
        return (jnp.dot(hi, rhs, preferred_element_type=F32)
                + jnp.dot(lo, rhs, preferred_element_type=F32))
    hi = rhs.astype(BF16)
    lo = (rhs - hi.astype(F32)).astype(BF16)
    return (jnp.dot(lhs, hi, preferred_element_type=F32)
            + jnp.dot(lhs, lo, preferred_element_type=F32))


def _ssd_kernel(xs_ref, bm_ref, cm_ref, dt_ref, dtt_ref,
                wx_ref, wb_ref, wc_ref, bx_ref, bb_ref, bc_ref,
                dtb_row_ref, dtb_col_ref, alog_row_ref, alog_col_ref, dskip_ref, *rest, seq, zero_init):
    h0_ref = None if zero_init else rest[0]
    (y_ref, hout_ref, padx_ref, padn_ref, xc_ref, bcs_ref, ccs_ref,
     yfwd_ref, ybwd_ref, state_ref) = rest[(0 if zero_init else 1):]
    q = SSD_CHUNK
    n_chunks = seq // q
    nh = 2 * SSD_HPG

    _conv_silu(xs_ref, wx_ref, bx_ref, padx_ref, xc_ref, seq)
    _conv_silu(bm_ref, wb_ref, bb_ref, padn_ref, bcs_ref, seq)
    _conv_silu(cm_ref, wc_ref, bc_ref, padn_ref, ccs_ref, seq)

    a_row = -jnp.exp(alog_row_ref[0])
    a_col = -jnp.exp(alog_col_ref[0])
    dtb_row = dtb_row_ref[0]
    dtb_col = dtb_col_ref[0]
    row_i = lax.broadcasted_iota(jnp.int32, (q, q), 0)
    col_i = lax.broadcasted_iota(jnp.int32, (q, q), 1)
    lower = row_i >= col_i
    upper = row_i <= col_i
    tri_lo = jnp.where(lower, 1.0, 0.0).astype(BF16)
    tri_up = jnp.where(upper, 1.0, 0.0).astype(BF16)
    lane = lax.broadcasted_iota(jnp.int32, (q, LANE), 1)
    head_lo = lane < SSD_HEAD_DIM

    def prologue(t, direction):
        rows = pl.ds(pl.multiple_of(t * q, q), q)
        dt = _softplus(dt_ref[0, 0, rows, :] + dtb_row)
        dtt = _softplus(dtt_ref[0, 0, :, rows] + dtb_col)
        da = dt * (a_row * LOG2E)
        dat = dtt * (a_col * LOG2E)
        if direction == 0:
            cum = _split_dot(tri_lo, da)
            cumt = _split_dot(dat, tri_up)
        else:
            cum = _split_dot(tri_up, da)
            cumt = _split_dot(dat, tri_lo)
        edge = q - 1 if direction == 0 else 0
        wt = dtt * jnp.exp2(cumt[:, edge:edge + 1] - cumt)
        bc = bcs_ref[rows, :]
        cc = ccs_ref[rows, :]
        cb = lax.dot_general(cc, bc, (((1,), (1,)), ((), ())), preferred_element_type=F32)
        bctf = bc.astype(F32).T
        return cum, cumt - jnp.log2(dtt), wt, cb, bctf

    def chunk(t, direction, pro):
        cum, rowt, wt, cb, bctf = pro
        rows = pl.ds(pl.multiple_of(t * q, q), q)
        mask = lower if direction == 0 else upper
        edge = q - 1 if direction == 0 else 0
        total = cum[edge:edge + 1, :]
        col0 = direction * SSD_HPG
        xb = xc_ref[rows, :]
        ccf = ccs_ref[rows, :].astype(F32)
        st = state_ref[direction]
        stb = st.astype(BF16)
        zero = jnp.zeros((q, LANE), BF16)
        ys, news = [], []
        for k in range(SSD_HPG // 2):
            xp = xb[:, k * LANE:(k + 1) * LANE]
            sp = stb[:, k * LANE:(k + 1) * LANE]
            rhs_x = jnp.concatenate([jnp.where(head_lo, xp, zero), jnp.where(head_lo, zero, xp)], axis=0)
            rhs_s = jnp.concatenate([jnp.where(head_lo, sp, zero), jnp.where(head_lo, zero, sp)], axis=0)
            ms, cs, ws = [], [], []
            for half in range(2):
                c = col0 + 2 * k + half
                bcol = jnp.broadcast_to(cum[:, c:c + 1], (q, q))
                seg = bcol - jnp.broadcast_to(rowt[c:c + 1, :], (q, q))
                ms.append((cb * jnp.exp2(jnp.where(mask, seg, -jnp.inf))).astype(BF16))
                cs.append((ccf * jnp.exp2(bcol)).astype(BF16))
                ws.append((bctf * jnp.broadcast_to(wt[c:c + 1, :], (q, q))).astype(BF16))
            ys.append(jnp.dot(jnp.concatenate(ms + cs, axis=1), jnp.concatenate([rhs_x, rhs_s], axis=0),
                              preferred_element_type=F32))
            news.append(jnp.dot(jnp.concatenate(ws, axis=1), rhs_x, preferred_element_type=F32))
        state_ref[direction] = st * _pair_expand(jnp.exp2(total), col0) + jnp.concatenate(news, axis=-1)
        return rows, jnp.concatenate(ys, axis=-1)

    state_ref[...] = jnp.zeros(state_ref.shape, F32) if zero_init else h0_ref[0, :, 0]

    def both(i, carry):
        pro_f, pro_b = carry
        nxt_f = prologue(jnp.minimum(i + 1, n_chunks - 1), 0)
        nxt_b = prologue(jnp.maximum(n_chunks - 2 - i, 0), 1)
        rows, y = chunk(i, 0, pro_f)
        yfwd_ref[rows, :] = y
        rows, y = chunk(n_chunks - 1 - i, 1, pro_b)
        ybwd_ref[rows, :] = y
        return nxt_f, nxt_b

    lax.fori_loop(0, n_chunks, both, (prologue(0, 0), prologue(n_chunks - 1, 1)))
    hout_ref[0, :, 0] = state_ref[...]

    dskip = dskip_ref[...]

    def finish(t, carry):
        rows = pl.ds(pl.multiple_of(t * q, q), q)
        y = yfwd_ref[rows, :] + ybwd_ref[rows, :] + xc_ref[rows, :].astype(F32) * dskip
        y_ref[0, rows, :] = y.astype(y_ref.dtype)
        return carry

    lax.fori_loop(0, n_chunks, finish, 0)


def _group_dt_order():
    order = []
    for g in range(SSD_GROUPS):
        for direction in range(2):
            order += [direction * SSD_HEADS + g * SSD_HPG + r for r in range(SSD_HPG)]
    return order


def _ssd(proj, dt, dtt, h0, conv_w, conv_b, dt_bias, a_log, d_skip):
    bsz, seq, _ = proj.shape
    gc = SSD_GROUP_CH
    ns = SSD_STATE
    nh = 2 * SSD_HPG
    xs_blk, b_blk, c_blk = EVO_XS // gc, EVO_B // ns, EVO_C // ns
    cw_b0, cw_c0 = SSD_INNER // ns, (SSD_INNER + SSD_GROUPS * ns) // ns
    conv_b2 = conv_b.reshape(1, -1)
    d_skip_x = jnp.repeat(d_skip, SSD_HEAD_DIM).reshape(1, SSD_INNER)
    order = jnp.array(_group_dt_order(), jnp.int32)
    dt_bias_g = dt_bias.reshape(-1)[order].reshape(SSD_GROUPS, nh)
    a_log_g = a_log.reshape(-1)[order].reshape(SSD_GROUPS, nh)
    in_specs = [
        pl.BlockSpec((1, seq, gc), lambda b, g: (b, 0, xs_blk + g)),
        pl.BlockSpec((1, seq, ns), lambda b, g: (b, 0, b_blk + g)),
        pl.BlockSpec((1, seq, ns), lambda b, g: (b, 0, c_blk + g)),
        pl.BlockSpec((1, 1, seq, nh), lambda b, g: (b, g, 0, 0)),
        pl.BlockSpec((1, 1, nh, seq), lambda b, g: (b, g, 0, 0)),
        pl.BlockSpec((SSD_CONV, gc), lambda b, g: (0, g)),
        pl.BlockSpec((SSD_CONV, ns), lambda b, g: (0, cw_b0 + g)),
        pl.BlockSpec((SSD_CONV, ns), lambda b, g: (0, cw_c0 + g)),
        pl.BlockSpec((1, gc), lambda b, g: (0, g)),
        pl.BlockSpec((1, ns), lambda b, g: (0, cw_b0 + g)),
        pl.BlockSpec((1, ns), lambda b, g: (0, cw_c0 + g)),
        pl.BlockSpec((1, 1, nh), lambda b, g: (g, 0, 0)),
        pl.BlockSpec((1, nh, 1), lambda b, g: (g, 0, 0)),
        pl.BlockSpec((1, 1, nh), lambda b, g: (g, 0, 0)),
        pl.BlockSpec((1, nh, 1), lambda b, g: (g, 0, 0)),
        pl.BlockSpec((1, gc), lambda b, g: (0, g)),
    ]
    state_spec = pl.BlockSpec((1, 2, 1, ns, gc), lambda b, g: (b, 0, g, 0, 0))
    args = [proj, proj, proj, dt, dtt,
            conv_w, conv_w, conv_w, conv_b2, conv_b2, conv_b2,
            dt_bias_g.reshape(SSD_GROUPS, 1, nh), dt_bias_g.reshape(SSD_GROUPS, nh, 1),
            a_log_g.reshape(SSD_GROUPS, 1, nh), a_log_g.reshape(SSD_GROUPS, nh, 1), d_skip_x]
    if h0 is not None:
        in_specs.append(state_spec)
        args.append(h0)
    return pl.pallas_call(
        functools.partial(_ssd_kernel, seq=seq, zero_init=h0 is None),
        out_shape=[jax.ShapeDtypeStruct((bsz, seq, SSD_INNER), BF16),
                   jax.ShapeDtypeStruct((bsz, 2, SSD_GROUPS, ns, gc), F32)],
        grid=(bsz, SSD_GROUPS),
        in_specs=in_specs,
        out_specs=[pl.BlockSpec((1, seq, gc), lambda b, g: (b, 0, g)), state_spec],
        scratch_shapes=[pltpu.VMEM((seq + 2 * SUBLANE, gc), F32),
                        pltpu.VMEM((seq + 2 * SUBLANE, ns), F32),
                        pltpu.VMEM((seq, gc), BF16),
                        pltpu.VMEM((seq, ns), BF16),
                        pltpu.VMEM((seq, ns), BF16),
                        pltpu.VMEM((seq, gc), F32),
                        pltpu.VMEM((seq, gc), F32),
                        pltpu.VMEM((2, ns, gc), F32)],
        compiler_params=_params("arbitrary", "arbitrary"),
        name="ssd",
    )(*args)


CV_ROWS = 256
CV_HALO = 16
CV_TC = 128
CV_SPAN = CV_ROWS + 2 * CV_HALO - SUBLANE


def _cvconv_kernel(glu_ref, w_ref, b_ref, o_ref, pad_ref, sh_ref, *, seq):
    ch = glu_ref.shape[-1]
    pad_ref[pl.ds(0, CV_HALO), :] = jnp.zeros((CV_HALO, ch), F32)
    pad_ref[pl.ds(seq + CV_HALO, CV_HALO), :] = jnp.zeros((CV_HALO, ch), F32)
    pad_ref[pl.ds(CV_HALO, seq), :] = glu_ref[0].astype(F32)
    w = w_ref[...]
    bias = b_ref[...]
    first = CV_HALO - CONV_WIDTH // 2
    rows = min(CV_ROWS * CV_TC // ch, seq)
    span = rows + 2 * CV_HALO - SUBLANE

    def body(t, carry):
        base = pl.multiple_of(t * rows, rows)
        win = pad_ref[pl.ds(base, rows + 2 * CV_HALO), :]
        for s in range(SUBLANE):
            sh_ref[s, pl.ds(0, span), :] = win[s:s + span, :]
        acc = jnp.broadcast_to(bias, (rows, ch))
        for k in range(CONV_WIDTH):
            a, s = divmod(first + k, SUBLANE)
            acc = acc + w[k:k + 1, :] * sh_ref[s, pl.ds(a * SUBLANE, rows), :]
        o_ref[0, pl.ds(base, rows), :] = acc.astype(o_ref.dtype)
        return carry

    lax.fori_loop(0, seq // rows, body, 0)


def _cvconv(proj, cv_w, cv_b):
    bsz, seq, _ = proj.shape
    tc = CV_TC * max(1, 1024 // seq)
    return pl.pallas_call(
        functools.partial(_cvconv_kernel, seq=seq),
        out_shape=jax.ShapeDtypeStruct((bsz, seq, CONV_CH), BF16),
        grid=(bsz, CONV_CH // tc),
        in_specs=[pl.BlockSpec((1, seq, tc), lambda b, j: (b, 0, EVO_GLU // tc + j)),
                  pl.BlockSpec((CONV_WIDTH, tc), lambda b, j: (0, j)),
                  pl.BlockSpec((1, tc), lambda b, j: (0, j))],
        out_specs=pl.BlockSpec((1, seq, tc), lambda b, j: (b, 0, j)),
        scratch_shapes=[pltpu.VMEM((seq + 2 * CV_HALO, tc), F32),
                        pltpu.VMEM((SUBLANE, CV_SPAN, tc), F32)],
        compiler_params=_params("arbitrary", "arbitrary"),
        name="cvconv",
    )(proj, cv_w, cv_b.reshape(1, CONV_CH))


def _gmlp_rows(u_ref, v_ref, gc_ref, lng_ref, lnb_ref, ws_ref, bias_ref, row0, n_rows):
    lng = lng_ref[...]
    lnb = lnb_ref[...]
    bias = bias_ref[...]
    out = []
    for c in range(n_rows // MLP_CHUNK):
        rows = pl.ds(row0 + c * MLP_CHUNK, MLP_CHUNK)
        vn = _layer_norm(v_ref[0, rows, :].astype(F32), lng, lnb).astype(BF16)
        mixed = jnp.concatenate(
            [jnp.dot(ws_ref[gi], vn[:, gi * MLP_GROUP_CH:(gi + 1) * MLP_GROUP_CH], preferred_element_type=F32)
             for gi in range(MLP_GROUPS)], axis=-1)
        u = u_ref[0, rows, :].astype(F32)
        out.append((u * (mixed + bias) * gc_ref[0, rows, :].astype(F32)).astype(BF16))
    return jnp.concatenate(out, axis=0)


def _rope(t, cos, sin_signed):
    width = t.shape[1]
    reps = width // LANE
    half = ATT_HEAD_DIM // 2
    lane = lax.broadcasted_iota(jnp.int32, t.shape, 1)
    first_half = (lane % ATT_HEAD_DIM) < half
    swapped = jnp.where(first_half, pltpu.roll(t, width - half, 1), pltpu.roll(t, half, 1))
    c = jnp.concatenate([cos] * reps, axis=-1)
    s = jnp.concatenate([sin_signed] * reps, axis=-1)
    return t * c + swapped * s


def _att_head_order():
    order = []
    for c in range(ATT_HEADS // 2):
        j, r = divmod(c, ATT_REP)
        order += [(2 * j) * ATT_REP + r, (2 * j + 1) * ATT_REP + r]
    return order


def _att_reorder(t, axis):
    shape = t.shape
    kvp = ATT_KV_HEADS // 2
    t = jnp.moveaxis(t, axis, 0).reshape((kvp, 2, ATT_REP, ATT_HEAD_DIM) + shape[:axis] + shape[axis + 1:])
    t = jnp.swapaxes(t, 1, 2).reshape((shape[axis],) + shape[:axis] + shape[axis + 1:])
    return jnp.moveaxis(t, 0, axis)


def _attn_kernel(*refs, seq, ctx_len, latent):
    if latent:
        (q_ref, gd_ref, k_ref, v_ref, kc_ref, vc_ref, sink_ref, cos_ref, sin_ref,
         o_ref, kpad_ref, vtp_ref, vct_ref) = refs
    else:
        q_ref, gd_ref, kc_ref, vc_ref, sink_ref, o_ref, vct_ref = refs
    i = pl.program_id(1)
    w = ATT_BLOCK
    nt = (((1,), (1,)), ((), ()))

    vrows = LANE + ATT_ONES

    @pl.when(i == 0)
    def _():
        vct = vc_ref[0].astype(F32).T.astype(BF16)
        for j in range(ATT_KV_CH // LANE):
            vct_ref[pl.ds(j * vrows, LANE), :] = vct[j * LANE:(j + 1) * LANE, :]
            vct_ref[pl.ds(j * vrows + LANE, ATT_ONES), :] = jnp.ones((ATT_ONES, ctx_len), BF16)
        if latent:
            kpad_ref[pl.ds(0, w), :] = jnp.zeros((w, ATT_KV_CH), BF16)
            kpad_ref[pl.ds(seq + w, w), :] = jnp.zeros((w, ATT_KV_CH), BF16)
            kpad_ref[pl.ds(w, seq), :] = _rope(k_ref[0].astype(F32), cos_ref[...], sin_ref[...]).astype(BF16)
            for j in range(ATT_KV_CH // LANE):
                vtp_ref[pl.ds(j * vrows, LANE), pl.ds(0, w)] = jnp.zeros((LANE, w), BF16)
                vtp_ref[pl.ds(j * vrows, LANE), pl.ds(seq + w, w)] = jnp.zeros((LANE, w), BF16)
                vtp_ref[pl.ds(j * vrows + LANE, ATT_ONES), :] = jnp.ones((ATT_ONES, seq + 2 * w), BF16)
            for t in range(seq // w):
                vt = v_ref[0, pl.ds(t * w, w), :].astype(F32).T.astype(BF16)
                for j in range(ATT_KV_CH // LANE):
                    vtp_ref[pl.ds(j * vrows, LANE), pl.ds((t + 1) * w, w)] = vt[j * LANE:(j + 1) * LANE, :]

    if latent:
        base = pl.multiple_of(i * w, w)
        q = _rope(q_ref[0].astype(F32), cos_ref[pl.ds(base, w), :], sin_ref[pl.ds(base, w), :])
        kwin = kpad_ref[pl.ds(base, 3 * w), :]
        qi = lax.broadcasted_iota(jnp.int32, (w, w), 1)
        u = lax.broadcasted_iota(jnp.int32, (w, w), 0)
        before = (u - w >= qi - ATT_WINDOW) & (base + u - w >= 0)
        after = (u + w <= qi + ATT_WINDOW) & (base + u + w < seq)
        bias_lo = jnp.where(before, 0.0, -jnp.inf)
        bias_hi = jnp.where(after, 0.0, -jnp.inf)
        bias_lo = jnp.concatenate([bias_lo, bias_lo], axis=1)
        bias_hi = jnp.concatenate([bias_hi, bias_hi], axis=1)
    else:
        q = q_ref[0].astype(F32)
    qs = (q * (ATT_SCALE * LOG2E)).astype(BF16)
    kc = kc_ref[0]
    lane = lax.broadcasted_iota(jnp.int32, (w, LANE), 1)
    low_lanes = lane < ATT_HEAD_DIM
    low_rows = lax.broadcasted_iota(jnp.int32, (LANE, w), 0) < ATT_HEAD_DIM
    zero = jnp.zeros((w, LANE), BF16)
    n_cols = ATT_HEADS // 2

    def scores(c):
        kv = slice((c // ATT_REP) * LANE, (c // ATT_REP + 1) * LANE)
        qc = qs[:, c * LANE:(c + 1) * LANE]
        rhs = jnp.concatenate([jnp.where(low_lanes, qc, zero), jnp.where(low_lanes, zero, qc)], axis=0)
        s_ctx = lax.dot_general(kc[:, kv], rhs, nt, preferred_element_type=F32)
        s_lat = lax.dot_general(kwin[:, kv], rhs, nt, preferred_element_type=F32) if latent else None
        return s_ctx, s_lat

    def finish(c, s_ctx, s_lat):
        j = c // ATT_REP
        snk = sink_ref[0, :, pl.ds(c * 2 * w, 2 * w)]
        m = jnp.maximum(jnp.max(s_ctx, axis=0, keepdims=True), snk)
        if latent:
            s_lo = s_lat[:w] + bias_lo
            s_mid = s_lat[w:2 * w]
            s_hi = s_lat[2 * w:] + bias_hi
            m = jnp.maximum(m, jnp.maximum(jnp.maximum(jnp.max(s_lo, axis=0, keepdims=True),
                                                       jnp.max(s_mid, axis=0, keepdims=True)),
                                           jnp.max(s_hi, axis=0, keepdims=True)))
        probs = [jnp.exp2(s_ctx - m)]
        vals = [vct_ref[pl.ds(j * vrows, vrows), :]]
        if latent:
            probs += [jnp.exp2(s_lo - m), jnp.exp2(s_mid - m), jnp.exp2(s_hi - m)]
            vals.append(vtp_ref[pl.ds(j * vrows, vrows), pl.ds(base, 3 * w)])
        acc = jnp.dot(jnp.concatenate(vals, axis=1), jnp.concatenate(probs, axis=0).astype(BF16),
                      preferred_element_type=F32)
        den = acc[LANE:LANE + 1, :] + jnp.exp2(snk - m)
        acc = acc[:LANE, :] * (1.0 / den)
        return jnp.where(low_rows, acc[:, :w], acc[:, w:]).T

    outs = []
    ahead = 3
    pending = [scores(c) for c in range(ahead)]
    for c in range(n_cols):
        if c + ahead < n_cols:
            pending.append(scores(c + ahead))
        outs.append(finish(c, *pending.pop(0)))
    gd = gd_ref[0].astype(F32)
    o_ref[0] = (jnp.concatenate(outs, axis=-1) * gd).astype(o_ref.dtype)


def _attention(proj, proj_ctx, sink, cos, sin_signed, latent, ctx_col=OD_K):
    bsz, seq, _ = proj.shape
    ctx_len = proj_ctx.shape[1]
    w = ATT_BLOCK
    qch = ATT_HEADS * ATT_HEAD_DIM
    kvc = ATT_KV_CH
    q_spec = pl.BlockSpec((1, w, qch), lambda b, i: (b, i, OD_Q // qch))
    gd_spec = pl.BlockSpec((1, w, qch), lambda b, i: (b, i, OD_GD // qch))
    kc_spec = pl.BlockSpec((1, ctx_len, kvc), lambda b, i: (b, 0, ctx_col // kvc))
    vc_spec = pl.BlockSpec((1, ctx_len, kvc), lambda b, i: (b, 0, ctx_col // kvc + 1))
    sink2 = jnp.repeat(sink[jnp.array(_att_head_order(), jnp.int32)] * LOG2E, w).reshape(1, 1, ATT_HEADS * w)
    sink_spec = pl.BlockSpec((1, 1, ATT_HEADS * w), lambda b, i: (0, 0, 0))
    vrows = (kvc // LANE) * (LANE + ATT_ONES)
    vct = pltpu.VMEM((vrows, ctx_len), BF16)
    if latent:
        in_specs = [q_spec, gd_spec,
                    pl.BlockSpec((1, seq, kvc), lambda b, i: (b, 0, OD_K // kvc)),
                    pl.BlockSpec((1, seq, kvc), lambda b, i: (b, 0, OD_VA // kvc)),
                    kc_spec, vc_spec, sink_spec,
                    pl.BlockSpec((seq, LANE), lambda b, i: (0, 0)),
                    pl.BlockSpec((seq, LANE), lambda b, i: (0, 0))]
        args = (proj, proj, proj, proj, proj_ctx, proj_ctx, sink2, cos, sin_signed)
        scratch = [pltpu.VMEM((seq + 2 * w, kvc), BF16), pltpu.VMEM((vrows, seq + 2 * w), BF16), vct]
    else:
        in_specs = [q_spec, gd_spec, kc_spec, vc_spec, sink_spec]
        args = (proj, proj, proj_ctx, proj_ctx, sink2)
        scratch = [vct]
    return pl.pallas_call(
        functools.partial(_attn_kernel, seq=seq, ctx_len=ctx_len, latent=latent),
        out_shape=jax.ShapeDtypeStruct((bsz, seq, qch), BF16),
        grid=(bsz, seq // w),
        in_specs=in_specs,
        out_specs=pl.BlockSpec((1, w, qch), lambda b, i: (b, i, 0)),
        scratch_shapes=scratch,
        compiler_params=_params("arbitrary", "arbitrary"),
        name="attention" if latent else "ctx_attention",
    )(*args)


OUTPROJ_TM = 1024
OUTPROJ_SUB = 512


def _outproj_kernel(*refs, even):
    if even:
        (y_ref, z_ref, cv_ref, gate_ref, nw_ref, cvg_ref, cvb_ref,
         x_ref, g_ref, lng_ref, lnb_ref, w_ref, o_ref) = refs
    else:
        (u_ref, v_ref, gc_ref, mlng_ref, mlnb_ref, ws_ref, bias_ref, yb_ref,
         x_ref, g_ref, lng_ref, lnb_ref, w_ref, o_ref) = refs
    half = w_ref.shape[0] // 2
    tm = x_ref.shape[1]
    sub = min(OUTPROJ_SUB, tm)

    def project(k):
        rows = pl.ds(k * sub, sub)
        if even:
            t = y_ref[0, rows, :].astype(F32) * z_ref[0, rows, :].astype(F32)
            ya = (t * lax.rsqrt(jnp.mean(t * t, -1, keepdims=True) + LN_EPS) * nw_ref[...]).astype(BF16)
            yb = (_silu(_layer_norm(cv_ref[0, rows, :].astype(F32), cvg_ref[...], cvb_ref[...]))
                  * gate_ref[0, rows, :].astype(F32)).astype(BF16)
        else:
            ya = _gmlp_rows(u_ref, v_ref, gc_ref, mlng_ref, mlnb_ref, ws_ref, bias_ref, k * sub, sub)
            yb = yb_ref[0, rows, :]
        return (jnp.dot(ya, w_ref[pl.ds(0, half), :], preferred_element_type=F32)
                + jnp.dot(yb, w_ref[pl.ds(half, half), :], preferred_element_type=F32))

    n_sub = tm // sub
    y = project(0)
    for k in range(n_sub):
        nxt = project(k + 1) if k + 1 < n_sub else None
        rows = pl.ds(k * sub, sub)
        r = DEEPNORM_ALPHA * x_ref[0, rows, :] + g_ref[0] * y
        o_ref[0, rows, :] = _layer_norm(r, lng_ref[...], lnb_ref[...])
        y = nxt


def _outproj(mix_args, mix_specs, x, g, ln_g, ln_b, w_out, even, tm):
    bsz, seq, d = x.shape
    vec = pl.BlockSpec((1, d), lambda b, i: (0, 0))
    in_specs = list(mix_specs) + [
        pl.BlockSpec((1, tm, d), lambda b, i: (b, i, 0)),
        pl.BlockSpec((1, 1, d), lambda b, i: (b, 0, 0)),
        vec, vec,
        pl.BlockSpec(w_out.shape, lambda b, i: (0, 0))]
    return pl.pallas_call(
        functools.partial(_outproj_kernel, even=even),
        out_shape=jax.ShapeDtypeStruct((bsz, seq, d), F32),
        grid=(bsz, seq // tm),
        in_specs=in_specs,
        out_specs=pl.BlockSpec((1, tm, d), lambda b, i: (b, i, 0)),
        compiler_params=_params("arbitrary", "arbitrary"),
        name="outproj_even" if even else "outproj_odd",
    )(*mix_args, x, g, ln_g.reshape(1, d), ln_b.reshape(1, d), w_out)


def _outproj_even(y_ssd, cv, proj, norm_w, cv_ln_g, cv_ln_b, x, g, ln_g, ln_b, w_out):
    seq = x.shape[1]
    tm = min(seq, OUTPROJ_TM)
    ch = SSD_INNER
    blk = lambda col: pl.BlockSpec((1, tm, ch), lambda b, i: (b, i, col // ch))
    vec = pl.BlockSpec((1, ch), lambda b, i: (0, 0))
    specs = [blk(0), blk(EVO_Z), blk(0), blk(EVO_GATE), vec, vec, vec]
    args = (y_ssd, proj, cv, proj, norm_w.reshape(1, ch), cv_ln_g.reshape(1, ch), cv_ln_b.reshape(1, ch))
    return _outproj(args, specs, x, g, ln_g, ln_b, w_out, True, tm)


def _outproj_odd(proj, yd, mlp_ln_g, mlp_ln_b, ws, bs, x, g, ln_g, ln_b, w_out):
    seq = x.shape[1]
    tm = min(seq, OUTPROJ_TM)
    ch = MLP_CH
    blk = lambda col: pl.BlockSpec((1, tm, ch), lambda b, i: (b, i, col // ch))
    vec = pl.BlockSpec((1, ch), lambda b, i: (0, 0))
    bias = jnp.repeat(bs.T, MLP_GROUP_CH, axis=1)
    specs = [blk(OD_U), blk(OD_V), blk(OD_GC), vec, vec,
             pl.BlockSpec((MLP_GROUPS, MLP_CHUNK, MLP_CHUNK), lambda b, i: (0, 0, 0)),
             pl.BlockSpec((MLP_CHUNK, ch), lambda b, i: (0, 0)), blk(0)]
    args = (proj, proj, proj, mlp_ln_g.reshape(1, ch), mlp_ln_b.reshape(1, ch), ws.astype(BF16), bias, yd)
    return _outproj(args, specs, x, g, ln_g, ln_b, w_out, False, tm)


def _rope_tables(seq):
    t = jnp.arange(seq)
    row = (t // GRID_W).astype(F32)
    col = (t % GRID_W).astype(F32)
    n_freq = ATT_HEAD_DIM // 4
    inv = ROPE_BASE ** (-jnp.arange(n_freq, dtype=F32) / n_freq)
    ang = jnp.concatenate([row[:, None] * inv, col[:, None] * inv], -1)
    cos, sin = jnp.cos(ang), jnp.sin(ang)
    reps = LANE // ATT_HEAD_DIM
    return (jnp.tile(jnp.concatenate([cos, cos], -1), (1, reps)),
            jnp.tile(jnp.concatenate([-sin, sin], -1), (1, reps)))


def _even_weights(w_in):
    o_z, o_xbc, o_dt, o_glu, o_gate = _offsets(SSD_INNER, SSD_INNER + 2 * SSD_BC, 2 * SSD_HEADS, 2 * CONV_CH, CONV_CH)
    w_in = w_in.astype(BF16)
    main = jnp.concatenate([w_in[:, o_z:o_xbc], w_in[:, o_glu:o_gate], w_in[:, o_gate:],
                            w_in[:, o_xbc:o_dt]], axis=1)
    w_dt = w_in[:, o_dt:o_glu][:, jnp.array(_group_dt_order(), jnp.int32)]
    return main, w_dt


def _odd_weights(w_in):
    o_q = 3 * MLP_CH
    o_k = o_q + ATT_HEADS * ATT_HEAD_DIM
    o_gd = o_k + 2 * ATT_KV_CH
    w_in = w_in.astype(BF16)
    return jnp.concatenate([w_in[:, :o_q], _att_reorder(w_in[:, o_q:o_k], 1), _att_reorder(w_in[:, o_gd:], 1),
                            w_in[:, o_k:o_gd]], axis=1)


def _odd_out_weights(w_out):
    w_out = w_out.astype(BF16)
    return jnp.concatenate([w_out[:MLP_CH], _att_reorder(w_out[MLP_CH:], 0)], axis=0)


def kernel(x, c, ctx, c_ctx, mod_w, mod_b, ln_g, ln_b, ev_w_in, ev_ssd_conv_w, ev_ssd_conv_b, ev_dt_bias, ev_a_log, ev_d_skip, ev_ssd_norm, ev_cv_w, ev_cv_b, ev_cv_ln_g, ev_cv_ln_b, ev_w_out, od_w_in, od_mlp_ln_g, od_mlp_ln_b, od_ws, od_bs, od_sink, od_w_out):
    bsz, seq, d = x.shape
    ctx_len = ctx.shape[1]
    assert d == D_MODEL and mod_w.shape[0] == DEPTH
    for n in (seq, ctx_len):
        assert n % SSD_CHUNK == 0 and n % ATT_BLOCK == 0 and n % MLP_CHUNK == 0
        assert n % min(n, OUTPROJ_TM) == 0 and min(n, OUTPROJ_TM) % min(n, OUTPROJ_SUB) == 0
        assert n % min(n, CV_ROWS) == 0
    for n in (seq, bsz * ctx_len):
        assert n % min(n, INPROJ_TM) == 0 and min(n, INPROJ_TM) % min(n, INPROJ_SUB) == 0
    cos, sin_signed = _rope_tables(seq)
    rows = -(-(bsz + 1) // SUBLANE) * SUBLANE
    cond = jnp.concatenate([c, c_ctx[None, :], jnp.zeros((rows - bsz - 1, d), F32)], axis=0)
    mod = _modulation(cond, mod_w, mod_b)

    for layer in range(DEPTH):
        last = layer == DEPTH - 1
        i = layer // 2
        m = mod[layer]
        sh_x, sc_x, g_x = (m[:bsz, None, k * d:(k + 1) * d] for k in range(3))
        sh_c, sc_c, g_c = (jnp.broadcast_to(m[bsz:bsz + 1, None, k * d:(k + 1) * d], (bsz, 1, d)) for k in range(3))
        if layer % 2 == 0:
            w_main, w_dt = _even_weights(ev_w_in[i])
            w_out = ev_w_out[i].astype(BF16)
            ssd_args = (ev_ssd_conv_w[i], ev_ssd_conv_b[i], ev_dt_bias[i], ev_a_log[i], ev_d_skip[i])
            p_c, dt_c, dtt_c = _inproj_ctx(ctx, sc_c, sh_c, w_main, w_dt)
            y_c, h_c = _ssd(p_c, dt_c, dtt_c, None, *ssd_args)
            p_x, dt_x, dtt_x = _inproj(x, sc_x, sh_x, w_main, w_dt)
            y_x, _ = _ssd(p_x, dt_x, dtt_x, h_c, *ssd_args)
            cv_x = _cvconv(p_x, ev_cv_w[i], ev_cv_b[i])
            x = _outproj_even(y_x, cv_x, p_x, ev_ssd_norm[i], ev_cv_ln_g[i], ev_cv_ln_b[i],
                              x, g_x, ln_g[layer], ln_b[layer], w_out)
            if not last:
                cv_c = _cvconv(p_c, ev_cv_w[i], ev_cv_b[i])
                ctx = _outproj_even(y_c, cv_c, p_c, ev_ssd_norm[i], ev_cv_ln_g[i], ev_cv_ln_b[i],
                                    ctx, g_c, ln_g[layer], ln_b[layer], w_out)
        else:
            w_main = _odd_weights(od_w_in[i])
            w_out = _odd_out_weights(od_w_out[i])
            if last:
                p_c = _inproj_ctx(ctx, sc_c, sh_c, w_main[:, OD_K:OD_K + INPROJ_TN], plan=(("id", 0),))[0]
                ctx_col = 0
            else:
                p_c = _inproj_ctx(ctx, sc_c, sh_c, w_main)[0]
                ctx_col = OD_K
            p_x = _inproj(x, sc_x, sh_x, w_main)[0]
            yd_x = _attention(p_x, p_c, od_sink[i], cos, sin_signed, True, ctx_col)
            mlp_args = (od_mlp_ln_g[i], od_mlp_ln_b[i], od_ws[i], od_bs[i])
            x = _outproj_odd(p_x, yd_x, *mlp_args, x, g_x, ln_g[layer], ln_b[layer], w_out)
            if not last:
                yd_c = _attention(p_c, p_c, od_sink[i], cos, sin_signed, False)
                ctx = _outproj_odd(p_c, yd_c, *mlp_args, ctx, g_c, ln_g[layer], ln_b[layer], w_out)
    return x
```

```python
import functools
import math

import jax
import jax.numpy as jnp
from jax import lax
from jax.experimental import pallas as pl
from jax.experimental.pallas import tpu as pltpu

F32 = jnp.float32
BF16 = jnp.bfloat16

D_MODEL = 1024
DEPTH = 4
GRID_W = 64

SSD_HEADS = 16
SSD_HEAD_DIM = 64
SSD_INNER = SSD_HEADS * SSD_HEAD_DIM
SSD_GROUPS = 2
SSD_HPG = SSD_HEADS // SSD_GROUPS
SSD_STATE = 128
SSD_CHUNK = 128
SSD_CONV = 5
SSD_GROUP_CH = SSD_HPG * SSD_HEAD_DIM
CONV_CH = 1024
CONV_WIDTH = 31
MLP_CH = 1024
MLP_GROUPS = 8
MLP_GROUP_CH = MLP_CH // MLP_GROUPS
MLP_CHUNK = 128
ATT_HEADS = 16
ATT_KV_HEADS = 4
ATT_REP = ATT_HEADS // ATT_KV_HEADS
ATT_HEAD_DIM = 64
ATT_WINDOW = 128
ATT_BLOCK = 128
ATT_SCALE = ATT_HEAD_DIM ** -0.5
LOG2E = math.log2(math.e)
ROPE_BASE = 10000.0
ATT_KV_CH = ATT_KV_HEADS * ATT_HEAD_DIM
ATT_ONES = 16

DEEPNORM_ALPHA = (2 * DEPTH) ** 0.25
LN_EPS = 1e-5

LANE = 128
SUBLANE = 8
VMEM_LIMIT = 56 * 1024 * 1024


def _offsets(*widths):
    return tuple(sum(widths[:i]) for i in range(len(widths)))


SSD_BC = SSD_GROUPS * SSD_STATE
EV_Z, EV_VAL, EV_GT, EV_GATE, EV_XS, EV_B, EV_C = _offsets(SSD_INNER, CONV_CH, CONV_CH, CONV_CH, SSD_INNER, SSD_BC, SSD_BC)
EVO_Z, EVO_GLU, EVO_GATE, EVO_XS, EVO_B, EVO_C = _offsets(SSD_INNER, CONV_CH, CONV_CH, SSD_INNER, SSD_BC, SSD_BC)
OD_U, OD_V, OD_GC, OD_Q, OD_GD, OD_K, OD_VA = _offsets(MLP_CH, MLP_CH, MLP_CH, ATT_HEADS * ATT_HEAD_DIM,
                                                       ATT_HEADS * ATT_HEAD_DIM, ATT_KV_CH, ATT_KV_CH)


def _silu(t):
    return t * jax.nn.sigmoid(t)


def _gelu_tanh(t):
    c = -2.0 * math.sqrt(2.0 / math.pi) * LOG2E
    return t / (1.0 + jnp.exp2(t * (c + (c * 0.044715) * (t * t))))


def _softplus(t):
    return jnp.maximum(t, 0.0) + jnp.log1p(jnp.exp(-jnp.abs(t)))


def _layer_norm(t, g, b):
    mu = jnp.mean(t, -1, keepdims=True)
    d = t - mu
    var = jnp.mean(d * d, -1, keepdims=True)
    return d * lax.rsqrt(var + LN_EPS) * g + b


def _params(*sem):
    return pltpu.CompilerParams(dimension_semantics=sem, vmem_limit_bytes=VMEM_LIMIT)


def _mod_kernel(c_ref, w_ref, b_ref, o_ref):
    s = _silu(c_ref[...]).astype(BF16)
    o_ref[0] = jnp.dot(s, w_ref[0].astype(BF16), preferred_element_type=F32) + b_ref[0]


def _modulation(cond, mod_w, mod_b):
    rows = cond.shape[0]
    d = D_MODEL
    return pl.pallas_call(
        _mod_kernel,
        out_shape=jax.ShapeDtypeStruct((DEPTH, rows, 3 * d), F32),
        grid=(DEPTH, 3),
        in_specs=[pl.BlockSpec((rows, d), lambda l, j: (0, 0)),
                  pl.BlockSpec((1, d, d), lambda l, j: (l, 0, j)),
                  pl.BlockSpec((1, 1, d), lambda l, j: (l, 0, j))],
        out_specs=pl.BlockSpec((1, rows, d), lambda l, j: (l, 0, j)),
        compiler_params=_params("arbitrary", "arbitrary"),
        name="modulation",
    )(cond, mod_w, mod_b.reshape(DEPTH, 1, 3 * d))


INPROJ_TM = 1024
INPROJ_SUB = 512
INPROJ_TN = 512


def _tile_plan(even):
    t = lambda col: col // INPROJ_TN
    if even:
        plan = [("silu", t(EV_Z) + i) for i in range(2)]
        plan += [("glu", t(EV_VAL) + i, t(EV_GT) + i) for i in range(2)]
        plan += [("silu", t(EV_GATE) + i) for i in range(2)]
        plan += [("id", t(EV_XS) + i) for i in range(3)]
    else:
        plan = [("gelu", t(OD_U) + i) for i in range(4)]
        plan += [("silu", t(OD_GC) + i) for i in range(2)]
        plan += [("id", t(OD_Q) + i) for i in range(2)]
        plan += [("silu", t(OD_GD) + i) for i in range(2)]
        plan += [("id", t(OD_K))]
    return plan


def _inproj_kernel(x_ref, sc_ref, sh_ref, w_ref, *rest, even, plan):
    if even:
        wdtt_ref, o_ref, dt_ref, dtt_ref, h_ref = rest
    else:
        o_ref, h_ref = rest
    tm = x_ref.shape[1]
    sub = min(INPROJ_SUB, tm)

    def modulate(s):
        rows = pl.ds(s * sub, sub)
        h_ref[rows, :] = (x_ref[0, rows, :] * (1.0 + sc_ref[0]) + sh_ref[0]).astype(BF16)

    modulate(0)
    for s in range(tm // sub):
        if (s + 1) * sub < tm:
            modulate(s + 1)
        rows = pl.ds(s * sub, sub)
        if even:
            dtt = lax.dot_general(wdtt_ref[...], h_ref[rows, :], (((1,), (1,)), ((), ())),
                                  preferred_element_type=F32)
            dt = dtt.T
            per = dt.shape[1] // SSD_GROUPS
            for gi in range(SSD_GROUPS):
                dt_ref[0, gi, rows, :] = dt[:, gi * per:(gi + 1) * per]
                dtt_ref[0, gi, :, rows] = dtt[gi * per:(gi + 1) * per, :]
        tile = lambda j: jnp.dot(h_ref[rows, :], w_ref[:, pl.ds(j * INPROJ_TN, INPROJ_TN)],
                                 preferred_element_type=F32)
        for out_j, (kind, *src) in enumerate(plan):
            r = tile(src[0])
            if kind == "silu":
                r = _silu(r)
            elif kind == "gelu":
                r = _gelu_tanh(r)
            elif kind == "glu":
                r = r * jax.nn.sigmoid(tile(src[1]))
            o_ref[0, rows, pl.ds(out_j * INPROJ_TN, INPROJ_TN)] = r.astype(BF16)


def _inproj(x, sc, sh, w, w_dt=None, plan=None):
    bsz, seq, d = x.shape
    n = w.shape[1]
    tm = min(seq, INPROJ_TM)
    even = w_dt is not None
    plan = tuple(_tile_plan(even)) if plan is None else plan
    n_out = len(plan) * INPROJ_TN
    resident = dict(pipeline_mode=pl.Buffered(1))
    in_specs = [pl.BlockSpec((1, tm, d), lambda b, i: (b, i, 0)),
                pl.BlockSpec((1, 1, d), lambda b, i: (b, 0, 0)),
                pl.BlockSpec((1, 1, d), lambda b, i: (b, 0, 0)),
                pl.BlockSpec((d, n), lambda b, i: (0, 0), **resident)]
    out_shape = [jax.ShapeDtypeStruct((bsz, seq, n_out), BF16)]
    out_specs = [pl.BlockSpec((1, tm, n_out), lambda b, i: (b, i, 0))]
    args = [x, sc, sh, w]
    if even:
        ndt = w_dt.shape[1]
        per = ndt // SSD_GROUPS
        in_specs += [pl.BlockSpec((ndt, d), lambda b, i: (0, 0))]
        out_shape += [jax.ShapeDtypeStruct((bsz, SSD_GROUPS, seq, per), F32),
                      jax.ShapeDtypeStruct((bsz, SSD_GROUPS, per, seq), F32)]
        out_specs += [pl.BlockSpec((1, SSD_GROUPS, tm, per), lambda b, i: (b, 0, i, 0)),
                      pl.BlockSpec((1, SSD_GROUPS, per, tm), lambda b, i: (b, 0, 0, i))]
        args += [w_dt.T]
    return pl.pallas_call(
        functools.partial(_inproj_kernel, even=even, plan=plan),
        out_shape=out_shape,
        grid=(bsz, seq // tm),
        in_specs=in_specs,
        out_specs=out_specs,
        scratch_shapes=[pltpu.VMEM((tm, d), BF16)],
        compiler_params=_params("arbitrary", "arbitrary"),
        name="inproj",
    )(*args)


def _inproj_ctx(ctx, sc, sh, w, w_dt=None, plan=None):
    bsz, clen, d = ctx.shape
    outs = _inproj(ctx.reshape(1, bsz * clen, d), sc[:1], sh[:1], w, w_dt, plan)
    proj = outs[0].reshape(bsz, clen, -1)
    if w_dt is None:
        return (proj,)
    per = outs[1].shape[-1]
    dt = outs[1].reshape(SSD_GROUPS, bsz, clen, per).transpose(1, 0, 2, 3)
    dtt = outs[2].reshape(SSD_GROUPS, per, bsz, clen).transpose(2, 0, 1, 3)
    return proj, dt, dtt


def _conv_silu(src_ref, w_ref, b_ref, pad_ref, dst_ref, seq):
    ch = src_ref.shape[-1]
    halo = SUBLANE
    pad_ref[pl.ds(0, halo), :] = jnp.zeros((halo, ch), F32)
    pad_ref[pl.ds(seq + halo, halo), :] = jnp.zeros((halo, ch), F32)
    pad_ref[pl.ds(halo, seq), :] = src_ref[0].astype(F32)
    w = w_ref[...]
    bias = b_ref[...]
    first = halo - SSD_CONV // 2

    def body(t, carry):
        base = pl.multiple_of(t * SSD_CHUNK, SSD_CHUNK)
        win = pad_ref[pl.ds(base, SSD_CHUNK + 2 * halo), :]
        acc = bias
        for k in range(SSD_CONV):
            acc = acc + w[k:k + 1, :] * win[first + k:first + k + SSD_CHUNK, :]
        dst_ref[pl.ds(base, SSD_CHUNK), :] = _silu(acc).astype(dst_ref.dtype)
        return carry

    lax.fori_loop(0, seq // SSD_CHUNK, body, 0)


def _pair_expand(vals, col0):
    q = vals.shape[0]
    lane = lax.broadcasted_iota(jnp.int32, (q, LANE), 1)
    parts = []
    for k in range(SSD_HPG // 2):
        a = jnp.broadcast_to(vals[:, col0 + 2 * k:col0 + 2 * k + 1], (q, LANE))
        b = jnp.broadcast_to(vals[:, col0 + 2 * k + 1:col0 + 2 * k + 2], (q, LANE))
        parts.append(jnp.where(lane < SSD_HEAD_DIM, a, b))
    return jnp.concatenate(parts, axis=-1)


def _split_dot(lhs, rhs):
    if lhs.dtype == F32:
        hi = lhs.astype(BF16)
        lo = (lhs - hi.astype(F32)).astype(BF16)
        return (jnp.dot(hi, rhs, preferred_element_type=F32)
                + jnp.dot(lo, rhs, preferred_element_type=F32))
    hi = rhs.astype(BF16)
    lo = (rhs - hi.astype(F32)).astype(BF16)
    return (jnp.dot(lhs, hi, preferred_element_type=F32)
            + jnp.dot(lhs, lo, preferred_element_type=F32))


def _ssd_kernel(xs_ref, bm_ref, cm_ref, dt_ref, dtt_ref,
                wx_ref, wb_ref, wc_ref, bx_ref, bb_ref, bc_ref,
                dtb_row_ref, dtb_col_ref, alog_row_ref, alog_col_ref, dskip_ref, *rest, seq, zero_init):
    h0_ref = None if zero_init else rest[0]
    (y_ref, hout_ref, padx_ref, padn_ref, xc_ref, bcs_ref, ccs_ref,
     yfwd_ref, ybwd_ref, state_ref) = rest[(0 if zero_init else 1):]
    q = SSD_CHUNK
    n_chunks = seq // q
    nh = 2 * SSD_HPG

    _conv_silu(xs_ref, wx_ref, bx_ref, padx_ref, xc_ref, seq)
    _conv_silu(bm_ref, wb_ref, bb_ref, padn_ref, bcs_ref, seq)
    _conv_silu(cm_ref, wc_ref, bc_ref, padn_ref, ccs_ref, seq)

    a_row = -jnp.exp(alog_row_ref[0])
    a_col = -jnp.exp(alog_col_ref[0])
    dtb_row = dtb_row_ref[0]
    dtb_col = dtb_col_ref[0]
    row_i = lax.broadcasted_iota(jnp.int32, (q, q), 0)
    col_i = lax.broadcasted_iota(jnp.int32, (q, q), 1)
    lower = row_i >= col_i
    upper = row_i <= col_i
    tri_lo = jnp.where(lower, 1.0, 0.0).astype(BF16)
    tri_up = jnp.where(upper, 1.0, 0.0).astype(BF16)
    lane = lax.broadcasted_iota(jnp.int32, (q, LANE), 1)
    head_lo = lane < SSD_HEAD_DIM

    def prologue(t, direction):
        rows = pl.ds(pl.multiple_of(t * q, q), q)
        dt = _softplus(dt_ref[0, 0, rows, :] + dtb_row)
        dtt = _softplus(dtt_ref[0, 0, :, rows] + dtb_col)
        da = dt * (a_row * LOG2E)
        dat = dtt * (a_col * LOG2E)
        if direction == 0:
            cum = _split_dot(tri_lo, da)
            cumt = _split_dot(dat, tri_up)
        else:
            cum = _split_dot(tri_up, da)
            cumt = _split_dot(dat, tri_lo)
        edge = q - 1 if direction == 0 else 0
        wt = dtt * jnp.exp2(cumt[:, edge:edge + 1] - cumt)
        bc = bcs_ref[rows, :]
        cc = ccs_ref[rows, :]
        cb = lax.dot_general(cc, bc, (((1,), (1,)), ((), ())), preferred_element_type=F32)
        bctf = bc.astype(F32).T
        return cum, cumt - jnp.log2(dtt), wt, cb, bctf

    def chunk(t, direction, pro):
        cum, rowt, wt, cb, bctf = pro
        rows = pl.ds(pl.multiple_of(t * q, q), q)
        mask = lower if direction == 0 else upper
        edge = q - 1 if direction == 0 else 0
        total = cum[edge:edge + 1, :]
        col0 = direction * SSD_HPG
        xb = xc_ref[rows, :]
        ccf = ccs_ref[rows, :].astype(F32)
        st = state_ref[direction]
        stb = st.astype(BF16)
        zero = jnp.zeros((q, LANE), BF16)
        ys, news = [], []
        for k in range(SSD_HPG // 2):
            xp = xb[:, k * LANE:(k + 1) * LANE]
            sp = stb[:, k * LANE:(k + 1) * LANE]
            rhs_x = jnp.concatenate([jnp.where(head_lo, xp, zero), jnp.where(head_lo, zero, xp)], axis=0)
            rhs_s = jnp.concatenate([jnp.where(head_lo, sp, zero), jnp.where(head_lo, zero, sp)], axis=0)
            ms, cs, ws = [], [], []
            for half in range(2):
                c = col0 + 2 * k + half
                bcol = jnp.broadcast_to(cum[:, c:c + 1], (q, q))
                seg = bcol - jnp.broadcast_to(rowt[c:c + 1, :], (q, q))
                ms.append((cb * jnp.exp2(jnp.where(mask, seg, -jnp.inf))).astype(BF16))
                cs.append((ccf * jnp.exp2(bcol)).astype(BF16))
                ws.append((bctf * jnp.broadcast_to(wt[c:c + 1, :], (q, q))).astype(BF16))
            ys.append(jnp.dot(jnp.concatenate(ms + cs, axis=1), jnp.concatenate([rhs_x, rhs_s], axis=0),
                              preferred_element_type=F32))
            news.append(jnp.dot(jnp.concatenate(ws, axis=1), rhs_x, preferred_element_type=F32))
        state_ref[direction] = st * _pair_expand(jnp.exp2(total), col0) + jnp.concatenate(news, axis=-1)
        return rows, jnp.concatenate(ys, axis=-1)

    state_ref[...] = jnp.zeros(state_ref.shape, F32) if zero_init else h0_ref[0, :, 0]

    def both(i, carry):
        pro_f, pro_b = carry
        nxt_f = prologue(jnp.minimum(i + 1, n_chunks - 1), 0)
        nxt_b = prologue(jnp.maximum(n_chunks - 2 - i, 0), 1)
        rows, y = chunk(i, 0, pro_f)
        yfwd_ref[rows, :] = y
        rows, y = chunk(n_chunks - 1 - i, 1, pro_b)
        ybwd_ref[rows, :] = y
        return nxt_f, nxt_b

    lax.fori_loop(0, n_chunks, both, (prologue(0, 0), prologue(n_chunks - 1, 1)))
    hout_ref[0, :, 0] = state_ref[...]

    dskip = dskip_ref[...]

    def finish(t, carry):
        rows = pl.ds(pl.multiple_of(t * q, q), q)
        y = yfwd_ref[rows, :] + ybwd_ref[rows, :] + xc_ref[rows, :].astype(F32) * dskip
        y_ref[0, rows, :] = y.astype(y_ref.dtype)
        return carry

    lax.fori_loop(0, n_chunks, finish, 0)


def _group_dt_order():
    order = []
    for g in range(SSD_GROUPS):
        for direction in range(2):
            order += [direction * SSD_HEADS + g * SSD_HPG + r for r in range(SSD_HPG)]
    return order


def _ssd(proj, dt, dtt, h0, conv_w, conv_b, dt_bias, a_log, d_skip):
    bsz, seq, _ = proj.shape
    gc = SSD_GROUP_CH
    ns = SSD_STATE
    nh = 2 * SSD_HPG
    xs_blk, b_blk, c_blk = EVO_XS // gc, EVO_B // ns, EVO_C // ns
    cw_b0, cw_c0 = SSD_INNER // ns, (SSD_INNER + SSD_GROUPS * ns) // ns
    conv_b2 = conv_b.reshape(1, -1)
    d_skip_x = jnp.repeat(d_skip, SSD_HEAD_DIM).reshape(1, SSD_INNER)
    order = jnp.array(_group_dt_order(), jnp.int32)
    dt_bias_g = dt_bias.reshape(-1)[order].reshape(SSD_GROUPS, nh)
    a_log_g = a_log.reshape(-1)[order].reshape(SSD_GROUPS, nh)
    in_specs = [
        pl.BlockSpec((1, seq, gc), lambda b, g: (b, 0, xs_blk + g)),
        pl.BlockSpec((1, seq, ns), lambda b, g: (b, 0, b_blk + g)),
        pl.BlockSpec((1, seq, ns), lambda b, g: (b, 0, c_blk + g)),
        pl.BlockSpec((1, 1, seq, nh), lambda b, g: (b, g, 0, 0)),
        pl.BlockSpec((1, 1, nh, seq), lambda b, g: (b, g, 0, 0)),
        pl.BlockSpec((SSD_CONV, gc), lambda b, g: (0, g)),
        pl.BlockSpec((SSD_CONV, ns), lambda b, g: (0, cw_b0 + g)),
        pl.BlockSpec((SSD_CONV, ns), lambda b, g: (0, cw_c0 + g)),
        pl.BlockSpec((1, gc), lambda b, g: (0, g)),
        pl.BlockSpec((1, ns), lambda b, g: (0, cw_b0 + g)),
        pl.BlockSpec((1, ns), lambda b, g: (0, cw_c0 + g)),
        pl.BlockSpec((1, 1, nh), lambda b, g: (g, 0, 0)),
        pl.BlockSpec((1, nh, 1), lambda b, g: (g, 0, 0)),
        pl.BlockSpec((1, 1, nh), lambda b, g: (g, 0, 0)),
        pl.BlockSpec((1, nh, 1), lambda b, g: (g, 0, 0)),
        pl.BlockSpec((1, gc), lambda b, g: (0, g)),
    ]
    state_spec = pl.BlockSpec((1, 2, 1, ns, gc), lambda b, g: (b, 0, g, 0, 0))
    args = [proj, proj, proj, dt, dtt,
            conv_w, conv_w, conv_w, conv_b2, conv_b2, conv_b2,
            dt_bias_g.reshape(SSD_GROUPS, 1, nh), dt_bias_g.reshape(SSD_GROUPS, nh, 1),
            a_log_g.reshape(SSD_GROUPS, 1, nh), a_log_g.reshape(SSD_GROUPS, nh, 1), d_skip_x]
    if h0 is not None:
        in_specs.append(state_spec)
        args.append(h0)
    return pl.pallas_call(
        functools.partial(_ssd_kernel, seq=seq, zero_init=h0 is None),
        out_shape=[jax.ShapeDtypeStruct((bsz, seq, SSD_INNER), BF16),
                   jax.ShapeDtypeStruct((bsz, 2, SSD_GROUPS, ns, gc), F32)],
        grid=(bsz, SSD_GROUPS),
        in_specs=in_specs,
        out_specs=[pl.BlockSpec((1, seq, gc), lambda b, g: (b, 0, g)), state_spec],
        scratch_shapes=[pltpu.VMEM((seq + 2 * SUBLANE, gc), F32),
                        pltpu.VMEM((seq + 2 * SUBLANE, ns), F32),
                        pltpu.VMEM((seq, gc), BF16),
                        pltpu.VMEM((seq, ns), BF16),
                        pltpu.VMEM((seq, ns), BF16),
                        pltpu.VMEM((seq, gc), F32),
                        pltpu.VMEM((seq, gc), F32),
                        pltpu.VMEM((2, ns, gc), F32)],
        compiler_params=_params("arbitrary", "arbitrary"),
        name="ssd",
    )(*args)


CV_ROWS = 256
CV_HALO = 16
CV_TC = 128
CV_SPAN = CV_ROWS + 2 * CV_HALO - SUBLANE


def _cvconv_kernel(glu_ref, w_ref, b_ref, o_ref, pad_ref, sh_ref, *, seq):
    ch = glu_ref.shape[-1]
    pad_ref[pl.ds(0, CV_HALO), :] = jnp.zeros((CV_HALO, ch), F32)
    pad_ref[pl.ds(seq + CV_HALO, CV_HALO), :] = jnp.zeros((CV_HALO, ch), F32)
    pad_ref[pl.ds(CV_HALO, seq), :] = glu_ref[0].astype(F32)
    w = w_ref[...]
    bias = b_ref[...]
    first = CV_HALO - CONV_WIDTH // 2
    rows = min(CV_ROWS * CV_TC // ch, seq)
    span = rows + 2 * CV_HALO - SUBLANE

    def body(t, carry):
        base = pl.multiple_of(t * rows, rows)
        win = pad_ref[pl.ds(base, rows + 2 * CV_HALO), :]
        for s in range(SUBLANE):
            sh_ref[s, pl.ds(0, span), :] = win[s:s + span, :]
        acc = jnp.broadcast_to(bias, (rows, ch))
        for k in range(CONV_WIDTH):
            a, s = divmod(first + k, SUBLANE)
            acc = acc + w[k:k + 1, :] * sh_ref[s, pl.ds(a * SUBLANE, rows), :]
        o_ref[0, pl.ds(base, rows), :] = acc.astype(o_ref.dtype)
        return carry

    lax.fori_loop(0, seq // rows, body, 0)


def _cvconv(proj, cv_w, cv_b):
    bsz, seq, _ = proj.shape
    tc = CV_TC * max(1, 1024 // seq)
    return pl.pallas_call(
        functools.partial(_cvconv_kernel, seq=seq),
        out_shape=jax.ShapeDtypeStruct((bsz, seq, CONV_CH), BF16),
        grid=(bsz, CONV_CH // tc),
        in_specs=[pl.BlockSpec((1, seq, tc), lambda b, j: (b, 0, EVO_GLU // tc + j)),
                  pl.BlockSpec((CONV_WIDTH, tc), lambda b, j: (0, j)),
                  pl.BlockSpec((1, tc), lambda b, j: (0, j))],
        out_specs=pl.BlockSpec((1, seq, tc), lambda b, j: (b, 0, j)),
        scratch_shapes=[pltpu.VMEM((seq + 2 * CV_HALO, tc), F32),
                        pltpu.VMEM((SUBLANE, CV_SPAN, tc), F32)],
        compiler_params=_params("arbitrary", "arbitrary"),
        name="cvconv",
    )(proj, cv_w, cv_b.reshape(1, CONV_CH))


def _gmlp_rows(u_ref, v_ref, gc_ref, lng_ref, lnb_ref, ws_ref, bias_ref, row0, n_rows):
    lng = lng_ref[...]
    lnb = lnb_ref[...]
    bias = bias_ref[...]
    out = []
    for c in range(n_rows // MLP_CHUNK):
        rows = pl.ds(row0 + c * MLP_CHUNK, MLP_CHUNK)
        vn = _layer_norm(v_ref[0, rows, :].astype(F32), lng, lnb).astype(BF16)
        mixed = jnp.concatenate(
            [jnp.dot(ws_ref[gi], vn[:, gi * MLP_GROUP_CH:(gi + 1) * MLP_GROUP_CH], preferred_element_type=F32)
             for gi in range(MLP_GROUPS)], axis=-1)
        u = u_ref[0, rows, :].astype(F32)
        out.append((u * (mixed + bias) * gc_ref[0, rows, :].astype(F32)).astype(BF16))
    return jnp.concatenate(out, axis=0)


def _rope(t, cos, sin_signed):
    width = t.shape[1]
    reps = width // LANE
    half = ATT_HEAD_DIM // 2
    lane = lax.broadcasted_iota(jnp.int32, t.shape, 1)
    first_half = (lane % ATT_HEAD_DIM) < half
    swapped = jnp.where(first_half, pltpu.roll(t, width - half, 1), pltpu.roll(t, half, 1))
    c = jnp.concatenate([cos] * reps, axis=-1)
    s = jnp.concatenate([sin_signed] * reps, axis=-1)
    return t * c + swapped * s


def _att_head_order():
    order = []
    for c in range(ATT_HEADS // 2):
        j, r = divmod(c, ATT_REP)
        order += [(2 * j) * ATT_REP + r, (2 * j + 1) * ATT_REP + r]
    return order


def _att_reorder(t, axis):
    shape = t.shape
    kvp = ATT_KV_HEADS // 2
    t = jnp.moveaxis(t, axis, 0).reshape((kvp, 2, ATT_REP, ATT_HEAD_DIM) + shape[:axis] + shape[axis + 1:])
    t = jnp.swapaxes(t, 1, 2).reshape((shape[axis],) + shape[:axis] + shape[axis + 1:])
    return jnp.moveaxis(t, 0, axis)


def _attn_kernel(*refs, seq, ctx_len, latent):
    if latent:
        (q_ref, gd_ref, k_ref, v_ref, kc_ref, vc_ref, sink_ref, cos_ref, sin_ref,
         o_ref, kpad_ref, vtp_ref, vct_ref) = refs
    else:
        q_ref, gd_ref, kc_ref, vc_ref, sink_ref, o_ref, vct_ref = refs
    i = pl.program_id(1)
    w = ATT_BLOCK
    nt = (((1,), (1,)), ((), ()))

    vrows = LANE + ATT_ONES

    @pl.when(i == 0)
    def _():
        vct = vc_ref[0].astype(F32).T.astype(BF16)
        for j in range(ATT_KV_CH // LANE):
            vct_ref[pl.ds(j * vrows, LANE), :] = vct[j * LANE:(j + 1) * LANE, :]
            vct_ref[pl.ds(j * vrows + LANE, ATT_ONES), :] = jnp.ones((ATT_ONES, ctx_len), BF16)
        if latent:
            kpad_ref[pl.ds(0, w), :] = jnp.zeros((w, ATT_KV_CH), BF16)
            kpad_ref[pl.ds(seq + w, w), :] = jnp.zeros((w, ATT_KV_CH), BF16)
            kpad_ref[pl.ds(w, seq), :] = _rope(k_ref[0].astype(F32), cos_ref[...], sin_ref[...]).astype(BF16)
            for j in range(ATT_KV_CH // LANE):
                vtp_ref[pl.ds(j * vrows, LANE), pl.ds(0, w)] = jnp.zeros((LANE, w), BF16)
                vtp_ref[pl.ds(j * vrows, LANE), pl.ds(seq + w, w)] = jnp.zeros((LANE, w), BF16)
                vtp_ref[pl.ds(j * vrows + LANE, ATT_ONES), :] = jnp.ones((ATT_ONES, seq + 2 * w), BF16)
            for t in range(seq // w):
                vt = v_ref[0, pl.ds(t * w, w), :].astype(F32).T.astype(BF16)
                for j in range(ATT_KV_CH // LANE):
                    vtp_ref[pl.ds(j * vrows, LANE), pl.ds((t + 1) * w, w)] = vt[j * LANE:(j + 1) * LANE, :]

    if latent:
        base = pl.multiple_of(i * w, w)
        q = _rope(q_ref[0].astype(F32), cos_ref[pl.ds(base, w), :], sin_ref[pl.ds(base, w), :])
        kwin = kpad_ref[pl.ds(base, 3 * w), :]
        qi = lax.broadcasted_iota(jnp.int32, (w, w), 1)
        u = lax.broadcasted_iota(jnp.int32, (w, w), 0)
        before = (u - w >= qi - ATT_WINDOW) & (base + u - w >= 0)
        after = (u + w <= qi + ATT_WINDOW) & (base + u + w < seq)
        bias_lo = jnp.where(before, 0.0, -jnp.inf)
        bias_hi = jnp.where(after, 0.0, -jnp.inf)
        bias_lo = jnp.concatenate([bias_lo, bias_lo], axis=1)
        bias_hi = jnp.concatenate([bias_hi, bias_hi], axis=1)
    else:
        q = q_ref[0].astype(F32)
    qs = (q * (ATT_SCALE * LOG2E)).astype(BF16)
    kc = kc_ref[0]
    lane = lax.broadcasted_iota(jnp.int32, (w, LANE), 1)
    low_lanes = lane < ATT_HEAD_DIM
    low_rows = lax.broadcasted_iota(jnp.int32, (LANE, w), 0) < ATT_HEAD_DIM
    zero = jnp.zeros((w, LANE), BF16)
    n_cols = ATT_HEADS // 2

    def scores(c):
        kv = slice((c // ATT_REP) * LANE, (c // ATT_REP + 1) * LANE)
        qc = qs[:, c * LANE:(c + 1) * LANE]
        rhs = jnp.concatenate([jnp.where(low_lanes, qc, zero), jnp.where(low_lanes, zero, qc)], axis=0)
        s_ctx = lax.dot_general(kc[:, kv], rhs, nt, preferred_element_type=F32)
        s_lat = lax.dot_general(kwin[:, kv], rhs, nt, preferred_element_type=F32) if latent else None
        return s_ctx, s_lat

    def finish(c, s_ctx, s_lat):
        j = c // ATT_REP
        snk = sink_ref[0, :, pl.ds(c * 2 * w, 2 * w)]
        m = jnp.maximum(jnp.max(s_ctx, axis=0, keepdims=True), snk)
        if latent:
            s_lo = s_lat[:w] + bias_lo
            s_mid = s_lat[w:2 * w]
            s_hi = s_lat[2 * w:] + bias_hi
            m = jnp.maximum(m, jnp.maximum(jnp.maximum(jnp.max(s_lo, axis=0, keepdims=True),
                                                       jnp.max(s_mid, axis=0, keepdims=True)),
                                           jnp.max(s_hi, axis=0, keepdims=True)))
        probs = [jnp.exp2(s_ctx - m)]
        vals = [vct_ref[pl.ds(j * vrows, vrows), :]]
        if latent:
            probs += [jnp.exp2(s_lo - m), jnp.exp2(s_mid - m), jnp.exp2(s_hi - m)]
            vals.append(vtp_ref[pl.ds(j * vrows, vrows), pl.ds(base, 3 * w)])
        acc = jnp.dot(jnp.concatenate(vals, axis=1), jnp.concatenate(probs, axis=0).astype(BF16),
                      preferred_element_type=F32)
        den = acc[LANE:LANE + 1, :] + jnp.exp2(snk - m)
        acc = acc[:LANE, :] * (1.0 / den)
        return jnp.where(low_rows, acc[:, :w], acc[:, w:]).T

    outs = []
    ahead = 3
    pending = [scores(c) for c in range(ahead)]
    for c in range(n_cols):
        if c + ahead < n_cols:
            pending.append(scores(c + ahead))
        outs.append(finish(c, *pending.pop(0)))
    gd = gd_ref[0].astype(F32)
    o_ref[0] = (jnp.concatenate(outs, axis=-1) * gd).astype(o_ref.dtype)


def _attention(proj, proj_ctx, sink, cos, sin_signed, latent, ctx_col=OD_K):
    bsz, seq, _ = proj.shape
    ctx_len = proj_ctx.shape[1]
    w = ATT_BLOCK
    qch = ATT_HEADS * ATT_HEAD_DIM
    kvc = ATT_KV_CH
    q_spec = pl.BlockSpec((1, w, qch), lambda b, i: (b, i, OD_Q // qch))
    gd_spec = pl.BlockSpec((1, w, qch), lambda b, i: (b, i, OD_GD // qch))
    kc_spec = pl.BlockSpec((1, ctx_len, kvc), lambda b, i: (b, 0, ctx_col // kvc))
    vc_spec = pl.BlockSpec((1, ctx_len, kvc), lambda b, i: (b, 0, ctx_col // kvc + 1))
    sink2 = jnp.repeat(sink[jnp.array(_att_head_order(), jnp.int32)] * LOG2E, w).reshape(1, 1, ATT_HEADS * w)
    sink_spec = pl.BlockSpec((1, 1, ATT_HEADS * w), lambda b, i: (0, 0, 0))
    vrows = (kvc // LANE) * (LANE + ATT_ONES)
    vct = pltpu.VMEM((vrows, ctx_len), BF16)
    if latent:
        in_specs = [q_spec, gd_spec,
                    pl.BlockSpec((1, seq, kvc), lambda b, i: (b, 0, OD_K // kvc)),
                    pl.BlockSpec((1, seq, kvc), lambda b, i: (b, 0, OD_VA // kvc)),
                    kc_spec, vc_spec, sink_spec,
                    pl.BlockSpec((seq, LANE), lambda b, i: (0, 0)),
                    pl.BlockSpec((seq, LANE), lambda b, i: (0, 0))]
        args = (proj, proj, proj, proj, proj_ctx, proj_ctx, sink2, cos, sin_signed)
        scratch = [pltpu.VMEM((seq + 2 * w, kvc), BF16), pltpu.VMEM((vrows, seq + 2 * w), BF16), vct]
    else:
        in_specs = [q_spec, gd_spec, kc_spec, vc_spec, sink_spec]
        args = (proj, proj, proj_ctx, proj_ctx, sink2)
        scratch = [vct]
    return pl.pallas_call(
        functools.partial(_attn_kernel, seq=seq, ctx_len=ctx_len, latent=latent),
        out_shape=jax.ShapeDtypeStruct((bsz, seq, qch), BF16),
        grid=(bsz, seq // w),
        in_specs=in_specs,
        out_specs=pl.BlockSpec((1, w, qch), lambda b, i: (b, i, 0)),
        scratch_shapes=scratch,
        compiler_params=_params("arbitrary", "arbitrary"),
        name="attention" if latent else "ctx_attention",
    )(*args)


OUTPROJ_TM = 1024
OUTPROJ_SUB = 512


def _outproj_kernel(*refs, even):
    if even:
        (y_ref, z_ref, cv_ref, gate_ref, nw_ref, cvg_ref, cvb_ref,
         x_ref, g_ref, lng_ref, lnb_ref, w_ref, o_ref) = refs
    else:
        (u_ref, v_ref, gc_ref, mlng_ref, mlnb_ref, ws_ref, bias_ref, yb_ref,
         x_ref, g_ref, lng_ref, lnb_ref, w_ref, o_ref) = refs
    half = w_ref.shape[0] // 2
    tm = x_ref.shape[1]
    sub = min(OUTPROJ_SUB, tm)

    def project(k):
        rows = pl.ds(k * sub, sub)
        if even:
            t = y_ref[0, rows, :].astype(F32) * z_ref[0, rows, :].astype(F32)
            ya = (t * lax.rsqrt(jnp.mean(t * t, -1, keepdims=True) + LN_EPS) * nw_ref[...]).astype(BF16)
            yb = (_silu(_layer_norm(cv_ref[0, rows, :].astype(F32), cvg_ref[...], cvb_ref[...]))
                  * gate_ref[0, rows, :].astype(F32)).astype(BF16)
        else:
            ya = _gmlp_rows(u_ref, v_ref, gc_ref, mlng_ref, mlnb_ref, ws_ref, bias_ref, k * sub, sub)
            yb = yb_ref[0, rows, :]
        return (jnp.dot(ya, w_ref[pl.ds(0, half), :], preferred_element_type=F32)
                + jnp.dot(yb, w_ref[pl.ds(half, half), :], preferred_element_type=F32))

    n_sub = tm // sub
    y = project(0)
    for k in range(n_sub):
        nxt = project(k + 1) if k + 1 < n_sub else None
        rows = pl.ds(k * sub, sub)
        r = DEEPNORM_ALPHA * x_ref[0, rows, :] + g_ref[0] * y
        o_ref[0, rows, :] = _layer_norm(r, lng_ref[...], lnb_ref[...])
        y = nxt


def _outproj(mix_args, mix_specs, x, g, ln_g, ln_b, w_out, even, tm):
    bsz, seq, d = x.shape
    vec = pl.BlockSpec((1, d), lambda b, i: (0, 0))
    in_specs = list(mix_specs) + [
        pl.BlockSpec((1, tm, d), lambda b, i: (b, i, 0)),
        pl.BlockSpec((1, 1, d), lambda b, i: (b, 0, 0)),
        vec, vec,
        pl.BlockSpec(w_out.shape, lambda b, i: (0, 0), pipeline_mode=pl.Buffered(1))]
    return pl.pallas_call(
        functools.partial(_outproj_kernel, even=even),
        out_shape=jax.ShapeDtypeStruct((bsz, seq, d), F32),
        grid=(bsz, seq // tm),
        in_specs=in_specs,
        out_specs=pl.BlockSpec((1, tm, d), lambda b, i: (b, i, 0)),
        compiler_params=_params("arbitrary", "arbitrary"),
        name="outproj_even" if even else "outproj_odd",
    )(*mix_args, x, g, ln_g.reshape(1, d), ln_b.reshape(1, d), w_out)


def _outproj_even(y_ssd, cv, proj, norm_w, cv_ln_g, cv_ln_b, x, g, ln_g, ln_b, w_out):
    seq = x.shape[1]
    tm = min(seq, OUTPROJ_TM)
    ch = SSD_INNER
    blk = lambda col: pl.BlockSpec((1, tm, ch), lambda b, i: (b, i, col // ch))
    vec = pl.BlockSpec((1, ch), lambda b, i: (0, 0))
    specs = [blk(0), blk(EVO_Z), blk(0), blk(EVO_GATE), vec, vec, vec]
    args = (y_ssd, proj, cv, proj, norm_w.reshape(1, ch), cv_ln_g.reshape(1, ch), cv_ln_b.reshape(1, ch))
    return _outproj(args, specs, x, g, ln_g, ln_b, w_out, True, tm)


def _outproj_odd(proj, yd, mlp_ln_g, mlp_ln_b, ws, bs, x, g, ln_g, ln_b, w_out):
    seq = x.shape[1]
    tm = min(seq, OUTPROJ_TM)
    ch = MLP_CH
    blk = lambda col: pl.BlockSpec((1, tm, ch), lambda b, i: (b, i, col // ch))
    vec = pl.BlockSpec((1, ch), lambda b, i: (0, 0))
    bias = jnp.repeat(bs.T, MLP_GROUP_CH, axis=1)
    specs = [blk(OD_U), blk(OD_V), blk(OD_GC), vec, vec,
             pl.BlockSpec((MLP_GROUPS, MLP_CHUNK, MLP_CHUNK), lambda b, i: (0, 0, 0)),
             pl.BlockSpec((MLP_CHUNK, ch), lambda b, i: (0, 0)), blk(0)]
    args = (proj, proj, proj, mlp_ln_g.reshape(1, ch), mlp_ln_b.reshape(1, ch), ws.astype(BF16), bias, yd)
    return _outproj(args, specs, x, g, ln_g, ln_b, w_out, False, tm)


def _rope_tables(seq):
    t = jnp.arange(seq)
    row = (t // GRID_W).astype(F32)
    col = (t % GRID_W).astype(F32)
    n_freq = ATT_HEAD_DIM // 4
    inv = ROPE_BASE ** (-jnp.arange(n_freq, dtype=F32) / n_freq)
    ang = jnp.concatenate([row[:, None] * inv, col[:, None] * inv], -1)
    cos, sin = jnp.cos(ang), jnp.sin(ang)
    reps = LANE // ATT_HEAD_DIM
    return (jnp.tile(jnp.concatenate([cos, cos], -1), (1, reps)),
            jnp.tile(jnp.concatenate([-sin, sin], -1), (1, reps)))


def _even_weights(w_in):
    o_z, o_xbc, o_dt, o_glu, o_gate = _offsets(SSD_INNER, SSD_INNER + 2 * SSD_BC, 2 * SSD_HEADS, 2 * CONV_CH, CONV_CH)
    main = jnp.concatenate([w_in[:, o_z:o_xbc], w_in[:, o_glu:o_gate], w_in[:, o_gate:],
                            w_in[:, o_xbc:o_dt]], axis=1)
    w_dt = w_in[:, o_dt:o_glu][:, jnp.array(_group_dt_order(), jnp.int32)]
    return main.astype(BF16), w_dt.astype(BF16)


def _odd_weights(w_in):
    o_q = 3 * MLP_CH
    o_k = o_q + ATT_HEADS * ATT_HEAD_DIM
    o_gd = o_k + 2 * ATT_KV_CH
    return jnp.concatenate([w_in[:, :o_q], _att_reorder(w_in[:, o_q:o_k], 1), _att_reorder(w_in[:, o_gd:], 1),
                            w_in[:, o_k:o_gd]], axis=1).astype(BF16)


def _odd_out_weights(w_out):
    return jnp.concatenate([w_out[:MLP_CH], _att_reorder(w_out[MLP_CH:], 0)], axis=0).astype(BF16)


def kernel(x, c, ctx, c_ctx, mod_w, mod_b, ln_g, ln_b, ev_w_in, ev_ssd_conv_w, ev_ssd_conv_b, ev_dt_bias, ev_a_log, ev_d_skip, ev_ssd_norm, ev_cv_w, ev_cv_b, ev_cv_ln_g, ev_cv_ln_b, ev_w_out, od_w_in, od_mlp_ln_g, od_mlp_ln_b, od_ws, od_bs, od_sink, od_w_out):
    bsz, seq, d = x.shape
    ctx_len = ctx.shape[1]
    assert d == D_MODEL and mod_w.shape[0] == DEPTH
    for n in (seq, ctx_len):
        assert n % SSD_CHUNK == 0 and n % ATT_BLOCK == 0 and n % MLP_CHUNK == 0
        assert n % min(n, OUTPROJ_TM) == 0 and min(n, OUTPROJ_TM) % min(n, OUTPROJ_SUB) == 0
        assert n % min(n, CV_ROWS) == 0
    for n in (seq, bsz * ctx_len):
        assert n % min(n, INPROJ_TM) == 0 and min(n, INPROJ_TM) % min(n, INPROJ_SUB) == 0
    cos, sin_signed = _rope_tables(seq)
    rows = -(-(bsz + 1) // SUBLANE) * SUBLANE
    cond = jnp.concatenate([c, c_ctx[None, :], jnp.zeros((rows - bsz - 1, d), F32)], axis=0)
    mod = _modulation(cond, mod_w, mod_b)

    for layer in range(DEPTH):
        last = layer == DEPTH - 1
        i = layer // 2
        m = mod[layer]
        sh_x, sc_x, g_x = (m[:bsz, None, k * d:(k + 1) * d] for k in range(3))
        sh_c, sc_c, g_c = (jnp.broadcast_to(m[bsz:bsz + 1, None, k * d:(k + 1) * d], (bsz, 1, d)) for k in range(3))
        if layer % 2 == 0:
            w_main, w_dt = _even_weights(ev_w_in[i])
            w_out = ev_w_out[i].astype(BF16)
            ssd_args = (ev_ssd_conv_w[i], ev_ssd_conv_b[i], ev_dt_bias[i], ev_a_log[i], ev_d_skip[i])
            p_c, dt_c, dtt_c = _inproj_ctx(ctx, sc_c, sh_c, w_main, w_dt)
            y_c, h_c = _ssd(p_c, dt_c, dtt_c, None, *ssd_args)
            p_x, dt_x, dtt_x = _inproj(x, sc_x, sh_x, w_main, w_dt)
            y_x, _ = _ssd(p_x, dt_x, dtt_x, h_c, *ssd_args)
            cv_x = _cvconv(p_x, ev_cv_w[i], ev_cv_b[i])
            x = _outproj_even(y_x, cv_x, p_x, ev_ssd_norm[i], ev_cv_ln_g[i], ev_cv_ln_b[i],
                              x, g_x, ln_g[layer], ln_b[layer], w_out)
            if not last:
                cv_c = _cvconv(p_c, ev_cv_w[i], ev_cv_b[i])
                ctx = _outproj_even(y_c, cv_c, p_c, ev_ssd_norm[i], ev_cv_ln_g[i], ev_cv_ln_b[i],
                                    ctx, g_c, ln_g[layer], ln_b[layer], w_out)
        else:
            w_main = _odd_weights(od_w_in[i])
            w_out = _odd_out_weights(od_w_out[i])
            if last:
                p_c = _inproj_ctx(ctx, sc_c, sh_c, w_main[:, OD_K:OD_K + INPROJ_TN], plan=(("id", 0),))[0]
                ctx_col = 0
            else:
                p_c = _inproj_ctx(ctx, sc_c, sh_c, w_main)[0]
                ctx_col = OD_K
            p_x = _inproj(x, sc_x, sh_x, w_main)[0]
            yd_x = _attention(p_x, p_c, od_sink[i], cos, sin_signed, True, ctx_col)
            mlp_args = (od_mlp_ln_g[i], od_mlp_ln_b[i], od_ws[i], od_bs[i])
            x = _outproj_odd(p_x, yd_x, *mlp_args, x, g_x, ln_g[layer], ln_b[layer], w_out)
            if not last:
                yd_c = _attention(p_c, p_c, od_sink[i], cos, sin_signed, False)
                ctx = _outproj_odd(p_c, yd_c, *mlp_args, ctx, g_c, ln_g[layer], ln_b[layer], w_out)
    return x
```
